```python
import jax, jax.numpy as jnp
from jax import lax
import numpy as np

D_MODEL = 1024
BATCH = 8
SEQ = 2048
DEPTH = 4

GRID_W = 64
N_HEADS = 16
HEAD_DIM = D_MODEL // N_HEADS
NA_ROWS = 8
NA_COLS = 16
NA_QCOLS = NA_COLS
NA_KCOLS = 2 * NA_COLS
GQA_KV_HEADS = 4
GQA_GROUP = N_HEADS // GQA_KV_HEADS
Q_BLOCK = 128
ROPE_THETA = 10000.0
ROPE_PAIRS = HEAD_DIM // 4
N_GROUPS = 4
EXPERTS_PER_GROUP = 8
N_EXPERTS = N_GROUPS * EXPERTS_PER_GROUP
TOP_K = 2
D_EXPERT = D_MODEL // 2
MOE_BLOCK = 128
N_MIXERS = 2
N_NA_LAYERS = (DEPTH + 1) // 2
N_GQA_LAYERS = DEPTH // 2
DEEPNORM_ALPHA = (2 * DEPTH) ** 0.25
DEEPNORM_BETA = (8 * DEPTH) ** -0.25
ADA_INIT = 0.2
LN_EPS = 1e-5
RMS_EPS = 1e-6

kernel_name = "hybrid_natten_gqa_hmoe_deepnorm"


def layer_norm(x, g, b):
    xf = x.astype(jnp.float32)
    mu = xf.mean(-1, keepdims=True)
    var = jnp.square(xf - mu).mean(-1, keepdims=True)
    return ((xf - mu) * lax.rsqrt(var + LN_EPS) * g.astype(jnp.float32) + b.astype(jnp.float32)).astype(x.dtype)


def rms_norm(x, g):
    xf = x.astype(jnp.float32)
    ms = jnp.square(xf).mean(-1, keepdims=True)
    return (xf * lax.rsqrt(ms + RMS_EPS) * g.astype(jnp.float32)).astype(x.dtype)


def neighbourhood_attention(q, k, v, rpb):
    B, S, H, hd = q.shape
    rows = S // GRID_W
    kr = min(NA_ROWS, rows)
    r = jnp.arange(rows, dtype=jnp.int32)
    row_start = jnp.clip(r - kr // 2, 0, rows - kr)
    key_rows = row_start[:, None] + jnp.arange(kr, dtype=jnp.int32)[None, :]
    dr = key_rows - r[:, None] + (NA_ROWS - 1)
    qg = q.reshape(B, rows, GRID_W, H, hd)
    kg = k.reshape(B, rows, GRID_W, H, hd)[:, key_rows]
    vg = v.reshape(B, rows, GRID_W, H, hd)[:, key_rows]
    outs = []
    for q0 in range(0, GRID_W, NA_QCOLS):
        b0 = min(max(q0 - NA_COLS // 2, 0), GRID_W - NA_KCOLS)
        qc = q0 + jnp.arange(NA_QCOLS, dtype=jnp.int32)
        kc = b0 + jnp.arange(NA_KCOLS, dtype=jnp.int32)
        win_start = jnp.clip(qc - NA_COLS // 2, 0, GRID_W - NA_COLS)
        valid = (kc[None, :] >= win_start[:, None]) & (kc[None, :] < win_start[:, None] + NA_COLS)
        dc = jnp.clip(kc[None, :] - qc[:, None] + NA_COLS - 1, 0, 2 * NA_COLS - 2)
        bias = rpb[:, dr[:, None, :, None], dc[None, :, None, :]]
        kb = kg[:, :, :, b0:b0 + NA_KCOLS]
        vb = vg[:, :, :, b0:b0 + NA_KCOLS]
        s = jnp.einsum('brqhd,brikhd->bhrqik', qg[:, :, q0:q0 + NA_QCOLS], kb).astype(jnp.float32)
        s = jnp.where(valid[:, None, :], s + bias.astype(jnp.float32), -jnp.inf)
        p = jax.nn.softmax(s.reshape(B, H, rows, NA_QCOLS, kr * NA_KCOLS), axis=-1)
        p = p.reshape(s.shape).astype(v.dtype)
        outs.append(jnp.einsum('bhrqik,brikhd->brqhd', p, vb))
    return jnp.concatenate(outs, axis=2).reshape(B, S, H, hd)


def neighbourhood_mixer(h, w_qkv, rpb, w_o):
    B, S, D = h.shape
    qkv = (h @ w_qkv).reshape(B, S, 3, N_HEADS, HEAD_DIM)
    q = qkv[:, :, 0] * (HEAD_DIM ** -0.5)
    o = neighbourhood_attention(q, qkv[:, :, 1], qkv[:, :, 2], rpb)
    return o.reshape(B, S, D) @ w_o


def axial_rope_tables(S):
    t = jnp.arange(S, dtype=jnp.int32)
    pos = jnp.stack([t // GRID_W, t % GRID_W], axis=-1).astype(jnp.float32)
    inv_freq = ROPE_THETA ** (-jnp.arange(ROPE_PAIRS, dtype=jnp.float32) / ROPE_PAIRS)
    ang = pos[:, :, None] * inv_freq
    return jnp.cos(ang), jnp.sin(ang)


def apply_axial_rope(x, cos, sin):
    B, S, H, hd = x.shape
    xr = x.astype(jnp.float32).reshape(B, S, H, 2, 2, ROPE_PAIRS)
    x1, x2 = xr[..., 0, :], xr[..., 1, :]
    c, s = cos[None, :, None], sin[None, :, None]
    out = jnp.stack([x1 * c - x2 * s, x2 * c + x1 * s], axis=-2)
    return out.reshape(B, S, H, hd).astype(x.dtype)


def block_gqa_attention(q, k, v):
    B, S, H, hd = q.shape
    nblk = S // Q_BLOCK
    qg = q.reshape(B, nblk, Q_BLOCK, GQA_KV_HEADS, GQA_GROUP, hd).transpose(1, 0, 2, 3, 4, 5)
    scale = HEAD_DIM ** -0.5

    def one_block(qb):
        s = jnp.einsum('bqkgd,bskd->bkgqs', qb, k).astype(jnp.float32) * scale
        p = jax.nn.softmax(s, axis=-1).astype(v.dtype)
        return jnp.einsum('bkgqs,bskd->bqkgd', p, v)

    o = lax.map(one_block, qg)
    return o.transpose(1, 0, 2, 3, 4, 5).reshape(B, S, H * hd)


def gqa_mixer(h, cos, sin, w_qkv, q_gain, k_gain, w_o):
    B, S, D = h.shape
    kvd = GQA_KV_HEADS * HEAD_DIM
    qkv = h @ w_qkv
    q = qkv[..., :D].reshape(B, S, N_HEADS, HEAD_DIM)
    k = qkv[..., D:D + kvd].reshape(B, S, GQA_KV_HEADS, HEAD_DIM)
    v = qkv[..., D + kvd:].reshape(B, S, GQA_KV_HEADS, HEAD_DIM)
    q = apply_axial_rope(rms_norm(q, q_gain), cos, sin)
    k = apply_axial_rope(rms_norm(k, k_gain), cos, sin)
    return block_gqa_attention(q, k, v) @ w_o


def hierarchical_moe(h, w_group, b_group, w_expert, b_expert, w_gate, w_up, w_down):
    B, S, D = h.shape
    N = B * S
    xt = h.reshape(N, D)
    pg = jax.nn.softmax((xt @ w_group).astype(jnp.float32) + b_group.astype(jnp.float32), axis=-1)
    g_prob, g_idx = lax.top_k(pg, 1)
    le = ((xt @ w_expert).astype(jnp.float32) + b_expert.astype(jnp.float32)).reshape(N, N_GROUPS, EXPERTS_PER_GROUP)
    le = le[jnp.arange(N), g_idx[:, 0]]
    e_prob, e_local = lax.top_k(jax.nn.softmax(le, axis=-1), TOP_K)
    e_prob = e_prob / e_prob.sum(-1, keepdims=True)
    gates = g_prob * e_prob
    experts = g_idx * EXPERTS_PER_GROUP + e_local
    A = N * TOP_K
    flat_e = experts.reshape(A).astype(jnp.int32)
    flat_tok = jnp.repeat(jnp.arange(N, dtype=jnp.int32), TOP_K)
    flat_gate = gates.reshape(A)
    order = jnp.argsort(flat_e)
    e_sorted = flat_e[order]
    counts = jnp.bincount(flat_e, length=N_EXPERTS).astype(jnp.int32)
    padded = (counts + MOE_BLOCK - 1) // MOE_BLOCK * MOE_BLOCK
    pad_end = jnp.cumsum(padded)
    pad_start = pad_end - padded
    seg_start = jnp.cumsum(counts) - counts
    dest = pad_start[e_sorted] + (jnp.arange(A, dtype=jnp.int32) - seg_start[e_sorted])
    n_blocks = -(-A // MOE_BLOCK) + N_EXPERTS
    n_rows = n_blocks * MOE_BLOCK
    row_tok = jnp.full((n_rows,), N, jnp.int32).at[dest].set(flat_tok[order])
    row_gate = jnp.zeros((n_rows,), jnp.float32).at[dest].set(flat_gate[order])
    block_expert = jnp.minimum(
        jnp.searchsorted(pad_end, jnp.arange(n_blocks, dtype=jnp.int32) * MOE_BLOCK, side='right'),
        N_EXPERTS - 1)
    x_pad = jnp.concatenate([xt, jnp.zeros((1, D), xt.dtype)], axis=0)
    xs = x_pad[row_tok].reshape(n_blocks, MOE_BLOCK, D)

    def expert_block(args):
        xb, e = args
        return (jax.nn.silu(xb @ w_gate[e]) * (xb @ w_up[e])) @ w_down[e]

    ys = lax.map(expert_block, (xs, block_expert)).reshape(n_rows, D)
    ys = ys * row_gate[:, None].astype(ys.dtype)
    out = jnp.zeros((N + 1, D), ys.dtype).at[row_tok].add(ys)[:N]
    return out.reshape(B, S, D)


def setup_inputs(seed: int = 0) -> dict:
    key = jax.random.key(seed)
    ks = jax.random.split(key, 20)
    f32 = jnp.float32
    D = D_MODEL
    inv = D ** -0.5
    kvd = GQA_KV_HEADS * HEAD_DIM

    def nrm(k, shape, scale):
        return jax.random.normal(k, shape, f32) * scale

    na_cols = jnp.concatenate([jnp.ones((2 * D,), f32), jnp.full((D,), DEEPNORM_BETA, f32)])
    gqa_cols = jnp.concatenate([jnp.ones((D + kvd,), f32), jnp.full((kvd,), DEEPNORM_BETA, f32)])
    return {
        "x": nrm(ks[0], (BATCH, SEQ, D), 1.0),
        "c": nrm(ks[1], (BATCH, D), 1.0),
        "ada_w": nrm(ks[2], (DEPTH, D, 6 * D), ADA_INIT * inv),
        "ada_b": nrm(ks[3], (DEPTH, 6 * D), 0.01),
        "ln_g": 1.0 + nrm(ks[4], (DEPTH, 2, D), 0.01),
        "ln_b": nrm(ks[5], (DEPTH, 2, D), 0.01),
        "na_w_qkv": nrm(ks[6], (N_NA_LAYERS, D, 3 * D), inv) * na_cols,
        "na_rpb": nrm(ks[7], (N_NA_LAYERS, N_HEADS, 2 * NA_ROWS - 1, 2 * NA_COLS - 1), 0.1),
        "na_w_o": nrm(ks[8], (N_NA_LAYERS, D, D), inv * DEEPNORM_BETA),
        "gqa_w_qkv": nrm(ks[9], (N_GQA_LAYERS, D, D + 2 * kvd), inv) * gqa_cols,
        "gqa_q_norm": 1.0 + nrm(ks[10], (N_GQA_LAYERS, HEAD_DIM), 0.01),
        "gqa_k_norm": 1.0 + nrm(ks[11], (N_GQA_LAYERS, HEAD_DIM), 0.01),
        "gqa_w_o": nrm(ks[12], (N_GQA_LAYERS, D, D), inv * DEEPNORM_BETA),
        "moe_w_group": nrm(ks[13], (DEPTH, D, N_GROUPS), inv),
        "moe_b_group": nrm(ks[14], (DEPTH, N_GROUPS), 0.01),
        "moe_w_expert": nrm(ks[15], (DEPTH, D, N_EXPERTS), inv),
        "moe_b_expert": nrm(ks[16], (DEPTH, N_EXPERTS), 0.01),
        "moe_w_gate": nrm(ks[17], (DEPTH, N_EXPERTS, D, D_EXPERT), inv),
        "moe_w_up": nrm(ks[18], (DEPTH, N_EXPERTS, D, D_EXPERT), inv),
        "moe_w_down": nrm(ks[19], (DEPTH, N_EXPERTS, D_EXPERT, D), (D_EXPERT ** -0.5) * DEEPNORM_BETA),
    }


def reference(x, c, ada_w, ada_b, ln_g, ln_b, na_w_qkv, na_rpb, na_w_o, gqa_w_qkv, gqa_q_norm,
              gqa_k_norm, gqa_w_o, moe_w_group, moe_b_group, moe_w_expert, moe_b_expert,
              moe_w_gate, moe_w_up, moe_w_down):
    B, S, D = x.shape
    cos, sin = axial_rope_tables(S)
    c_act = jax.nn.silu(c)
    for i in range(DEPTH):
        mod = c_act @ ada_w[i] + ada_b[i]
        sh1, sc1, g1, sh2, sc2, g2 = jnp.split(mod[:, None, :], 6, axis=-1)
        h = x * (1 + sc1) + sh1
        j = i // N_MIXERS
        if i % N_MIXERS == 0:
            y = neighbourhood_mixer(h, na_w_qkv[j], na_rpb[j], na_w_o[j])
        else:
            y = gqa_mixer(h, cos, sin, gqa_w_qkv[j], gqa_q_norm[j], gqa_k_norm[j], gqa_w_o[j])
        x = layer_norm(DEEPNORM_ALPHA * x + (1 + g1) * y, ln_g[i, 0], ln_b[i, 0])
        h = x * (1 + sc2) + sh2
        y = hierarchical_moe(h, moe_w_group[i], moe_b_group[i], moe_w_expert[i], moe_b_expert[i],
                             moe_w_gate[i], moe_w_up[i], moe_w_down[i])
        x = layer_norm(DEEPNORM_ALPHA * x + (1 + g2) * y, ln_g[i, 1], ln_b[i, 1])
    return x
```

```python
import functools

import jax
import jax.numpy as jnp
from jax import lax
from jax.experimental import pallas as pl
from jax.experimental.pallas import tpu as pltpu

GRID_W = 64
N_HEADS = 16
HEAD_DIM = 64
NA_ROWS = 8
NA_COLS = 16
GQA_KV_HEADS = 4
ROPE_THETA = 10000.0
ROPE_PAIRS = HEAD_DIM // 4
N_GROUPS = 4
EXPERTS_PER_GROUP = 8
N_EXPERTS = N_GROUPS * EXPERTS_PER_GROUP
TOP_K = 2
LN_EPS = 1e-5
RMS_EPS = 1e-6

LANES = 128
HEAD_PAIR = 2 * HEAD_DIM
VMEM_LIMIT_BYTES = 52 * 1024 * 1024
ADA_TN = 1536
ROW_TILE = 512
QKV_CHUNK = 512
GQA_TQ = 256
MOE_TB = 256
COMBINE_TM = 256
NEG_BIG = -1e30

F32 = jnp.float32
BF16 = jnp.bfloat16


def _cparams(sem):
    return pltpu.CompilerParams(dimension_semantics=sem, vmem_limit_bytes=VMEM_LIMIT_BYTES)


def _split_bf16(a):
    hi = a.astype(BF16)
    lo = (a - hi.astype(F32)).astype(BF16)
    return hi, lo


def _ada_kernel(c_ref, w_ref, b_ref, o_ref):
    c = c_ref[...]
    ca = (c * jax.nn.sigmoid(c)).astype(BF16)
    o_ref[0] = jnp.dot(ca, w_ref[0].astype(BF16), preferred_element_type=F32) + b_ref[0]


def _ada_modulation(c, ada_w, ada_b):
    depth, d, n = ada_w.shape
    b = c.shape[0]
    return pl.pallas_call(
        _ada_kernel,
        grid=(depth, n // ADA_TN),
        in_specs=[
            pl.BlockSpec((b, d), lambda i, j: (0, 0)),
            pl.BlockSpec((1, d, ADA_TN), lambda i, j: (i, 0, j)),
            pl.BlockSpec((1, 1, ADA_TN), lambda i, j: (i, 0, j)),
        ],
        out_specs=pl.BlockSpec((1, b, ADA_TN), lambda i, j: (i, 0, j)),
        out_shape=jax.ShapeDtypeStruct((depth, b, n), F32),
        compiler_params=_cparams(("arbitrary", "arbitrary")),
        name="ada_modulation",
    )(c, ada_w, ada_b.reshape(depth, 1, n))


def _qkv_na_kernel(x_ref, sc_ref, sh_ref, w_ref, o_ref, *, d_model):
    h = (x_ref[0] * (1.0 + sc_ref[0]) + sh_ref[0]).astype(BF16)
    n = w_ref.shape[1]
    for n0 in range(0, n, QKV_CHUNK):
        acc = jnp.dot(h, w_ref[:, n0:n0 + QKV_CHUNK], preferred_element_type=F32)
        if n0 < d_model:
            acc = acc * (HEAD_DIM ** -0.5)
        o_ref[0, :, n0:n0 + QKV_CHUNK] = acc.astype(BF16)


def _qkv_na(x, sc, sh, w_bf16):
    b, s, d = x.shape
    n = w_bf16.shape[1]
    return pl.pallas_call(
        functools.partial(_qkv_na_kernel, d_model=d),
        grid=(b, s // ROW_TILE),
        in_specs=[
            pl.BlockSpec((1, ROW_TILE, d), lambda i, j: (i, j, 0)),
            pl.BlockSpec((1, 1, d), lambda i, j: (i, 0, 0)),
            pl.BlockSpec((1, 1, d), lambda i, j: (i, 0, 0)),
            pl.BlockSpec((d, n), lambda i, j: (0, 0)),
        ],
        out_specs=pl.BlockSpec((1, ROW_TILE, n), lambda i, j: (i, j, 0)),
        out_shape=jax.ShapeDtypeStruct((b, s, n), BF16),
        compiler_params=_cparams(("arbitrary", "arbitrary")),
        name="qkv_na",
    )(x, sc, sh, w_bf16)


def _na_kernel(q_ref, k_ref, v_ref, bias_ref, o_ref, *, rows):
    r = pl.program_id(1)
    kr = min(NA_ROWS, rows)
    row_start = jnp.clip(r - kr // 2, 0, rows - kr)
    start = pl.multiple_of(row_start * GRID_W, GRID_W)
    nk = kr * GRID_W
    lane = lax.broadcasted_iota(jnp.int32, (GRID_W, HEAD_PAIR), 1)
    first = lane < HEAD_DIM
    for hp in range(N_HEADS // 2):
        cols = slice(hp * HEAD_PAIR, (hp + 1) * HEAD_PAIR)
        q2 = q_ref[0, :, cols]
        k2 = k_ref[0, pl.ds(start, nk), cols]
        v2 = v_ref[0, pl.ds(start, nk), cols]
        zero = jnp.zeros_like(q2)
        qs = jnp.concatenate([jnp.where(first, q2, zero), jnp.where(first, zero, q2)], axis=0)
        s = lax.dot_general(qs, k2, (((1,), (1,)), ((), ())), preferred_element_type=F32)
        s = s + bias_ref[0, hp]
        m = jnp.max(s, axis=-1, keepdims=True)
        p = jnp.exp(s - m)
        l = jnp.sum(p, axis=-1, keepdims=True)
        o = jnp.dot(p.astype(BF16), v2, preferred_element_type=F32) / l
        o_ref[0, :, cols] = jnp.where(first, o[:GRID_W], o[GRID_W:]).astype(BF16)


def _na_bias_table(rpb, rows):
    kr = min(NA_ROWS, rows)
    qc = jnp.arange(GRID_W, dtype=jnp.int32)
    kc = jnp.arange(GRID_W, dtype=jnp.int32)
    win_start = jnp.clip(qc - NA_COLS // 2, 0, GRID_W - NA_COLS)
    valid = (kc[None, :] >= win_start[:, None]) & (kc[None, :] < win_start[:, None] + NA_COLS)
    dc = jnp.clip(kc[None, :] - qc[:, None] + NA_COLS - 1, 0, 2 * NA_COLS - 2)
    cls = jnp.arange(kr, dtype=jnp.int32)
    dr = jnp.arange(kr, dtype=jnp.int32)[None, :] - cls[:, None] + (NA_ROWS - 1)
    bias = rpb[:, dr[:, None, :, None], dc[None, :, None, :]]
    bias = jnp.where(valid[None, None, :, None, :], bias.astype(F32), NEG_BIG)
    bias = bias.transpose(1, 0, 2, 3, 4).reshape(kr, N_HEADS // 2, 2 * GRID_W, kr * GRID_W)
    return bias


def _na_attention(qkv, bias, rows):
    b, s, n3 = qkv.shape
    d = n3 // 3
    kr = min(NA_ROWS, rows)

    def bias_index(i, r):
        row_start = jnp.clip(r - kr // 2, 0, rows - kr)
        return (r - row_start, 0, 0, 0)

    return pl.pallas_call(
        functools.partial(_na_kernel, rows=rows),
        grid=(b, rows),
        in_specs=[
            pl.BlockSpec((1, GRID_W, d), lambda i, r: (i, r, 0)),
            pl.BlockSpec((1, s, d), lambda i, r: (i, 0, 1)),
            pl.BlockSpec((1, s, d), lambda i, r: (i, 0, 2)),
            pl.BlockSpec((1, N_HEADS // 2, 2 * GRID_W, kr * GRID_W), bias_index),
        ],
        out_specs=pl.BlockSpec((1, GRID_W, d), lambda i, r: (i, r, 0)),
        out_shape=jax.ShapeDtypeStruct((b, s, d), BF16),
        compiler_params=_cparams(("arbitrary", "arbitrary")),
        name="na_attention",
    )(qkv, qkv, qkv, bias)


def _rms_rope(a, gain, cosf, sinf, seg, seg_t):
    qq_hi, qq_lo = _split_bf16(a * a)
    ms = jnp.dot(qq_hi, seg, preferred_element_type=F32) + jnp.dot(qq_lo, seg, preferred_element_type=F32)
    r_hi, r_lo = _split_bf16(lax.rsqrt(ms + RMS_EPS))
    rf = jnp.dot(r_hi, seg_t, preferred_element_type=F32) + jnp.dot(r_lo, seg_t, preferred_element_type=F32)
    an = a * rf * gain
    n = a.shape[1]
    lane = lax.broadcasted_iota(jnp.int32, a.shape, 1)
    partner = jnp.where(lane % (2 * ROPE_PAIRS) < ROPE_PAIRS,
                        pltpu.roll(an, n - ROPE_PAIRS, 1), pltpu.roll(an, ROPE_PAIRS, 1))
    return an * cosf + partner * sinf


def _qkv_gqa_kernel(x_ref, sc_ref, sh_ref, w_ref, qg_ref, kg_ref, cos_ref, sin_ref, seg_ref, segt_ref, o_ref,
                    *, d_model):
    h = (x_ref[0] * (1.0 + sc_ref[0]) + sh_ref[0]).astype(BF16)
    n = w_ref.shape[1]
    kv_cols = (n - d_model) // 2
    cosf, sinf = cos_ref[...], sin_ref[...]
    seg, seg_t = seg_ref[...], segt_ref[...]
    for n0 in range(0, n, QKV_CHUNK):
        acc = jnp.dot(h, w_ref[:, n0:n0 + QKV_CHUNK], preferred_element_type=F32)
        if n0 < d_model:
            acc = _rms_rope(acc, qg_ref[...], cosf, sinf, seg, seg_t) * (HEAD_DIM ** -0.5)
        elif n0 < d_model + kv_cols:
            acc = _rms_rope(acc, kg_ref[...], cosf, sinf, seg, seg_t)
        o_ref[0, :, n0:n0 + QKV_CHUNK] = acc.astype(BF16)


def _qkv_gqa(x, sc, sh, w_bf16, q_gain, k_gain, cosf, sinf, seg, seg_t):
    b, s, d = x.shape
    n = w_bf16.shape[1]
    const = lambda j, i: (0, 0)
    return pl.pallas_call(
        functools.partial(_qkv_gqa_kernel, d_model=d),
        grid=(s // ROW_TILE, b),
        in_specs=[
            pl.BlockSpec((1, ROW_TILE, d), lambda j, i: (i, j, 0)),
            pl.BlockSpec((1, 1, d), lambda j, i: (i, 0, 0)),
            pl.BlockSpec((1, 1, d), lambda j, i: (i, 0, 0)),
            pl.BlockSpec((d, n), const),
            pl.BlockSpec((1, QKV_CHUNK), const),
            pl.BlockSpec((1, QKV_CHUNK), const),
            pl.BlockSpec((ROW_TILE, QKV_CHUNK), lambda j, i: (j, 0)),
            pl.BlockSpec((ROW_TILE, QKV_CHUNK), lambda j, i: (j, 0)),
            pl.BlockSpec((QKV_CHUNK, LANES), const),
            pl.BlockSpec((LANES, QKV_CHUNK), const),
        ],
        out_specs=pl.BlockSpec((1, ROW_TILE, n), lambda j, i: (i, j, 0)),
        out_shape=jax.ShapeDtypeStruct((b, s, n), BF16),
        compiler_params=_cparams(("arbitrary", "arbitrary")),
        name="qkv_gqa",
    )(x, sc, sh, w_bf16, q_gain, k_gain, cosf, sinf, seg, seg_t)


def _gqa_kernel(q_ref, k_ref, v_ref, o_ref):
    k2 = k_ref[0]
    v2 = v_ref[0]
    tq = q_ref.shape[1]
    lane = lax.broadcasted_iota(jnp.int32, (tq, HEAD_PAIR), 1)
    first = lane < HEAD_DIM
    for pair in range(q_ref.shape[2] // HEAD_PAIR):
        cols = slice(pair * HEAD_PAIR, (pair + 1) * HEAD_PAIR)
        q2 = q_ref[0, :, cols]
        zero = jnp.zeros_like(q2)
        qs = jnp.concatenate([jnp.where(first, q2, zero), jnp.where(first, zero, q2)], axis=0)
        s = lax.dot_general(qs, k2, (((1,), (1,)), ((), ())), preferred_element_type=F32)
        m = jnp.max(s, axis=-1, keepdims=True)
        p = jnp.exp(s - m)
        l = jnp.sum(p, axis=-1, keepdims=True)
        o = jnp.dot(p.astype(BF16), v2, preferred_element_type=F32) / l
        o_ref[0, :, cols] = jnp.where(first, o[:tq], o[tq:]).astype(BF16)


def _gqa_attention(qkv, d_model):
    b, s, n = qkv.shape
    group_cols = d_model // GQA_KV_HEADS
    k_block0 = d_model // HEAD_PAIR
    v_block0 = k_block0 + GQA_KV_HEADS
    return pl.pallas_call(
        _gqa_kernel,
        grid=(b, GQA_KV_HEADS, s // GQA_TQ),
        in_specs=[
            pl.BlockSpec((1, GQA_TQ, group_cols), lambda i, g, t: (i, t, g)),
            pl.BlockSpec((1, s, HEAD_PAIR), lambda i, g, t: (i, 0, k_block0 + g)),
            pl.BlockSpec((1, s, HEAD_PAIR), lambda i, g, t: (i, 0, v_block0 + g)),
        ],
        out_specs=pl.BlockSpec((1, GQA_TQ, group_cols), lambda i, g, t: (i, t, g)),
        out_shape=jax.ShapeDtypeStruct((b, s, d_model), BF16),
        compiler_params=_cparams(("arbitrary", "arbitrary", "arbitrary")),
        name="gqa_attention",
    )(qkv, qkv, qkv)


def _layer_norm_rows(z, g, b):
    mu = jnp.mean(z, axis=-1, keepdims=True)
    zc = z - mu
    var = jnp.mean(zc * zc, axis=-1, keepdims=True)
    return zc * lax.rsqrt(var + LN_EPS) * g + b


def _router(lg):
    lane = lax.broadcasted_iota(jnp.int32, lg.shape, 1)
    lanef = lane.astype(F32)
    far = float(LANES)
    gmask = lane < N_GROUPS
    lgm = jnp.where(gmask, lg, NEG_BIG)
    mg = jnp.max(lgm, axis=-1, keepdims=True)
    sg = jnp.sum(jnp.exp(lgm - mg), axis=-1, keepdims=True)
    g_prob = 1.0 / sg
    g_idx = jnp.min(jnp.where(lgm == mg, lanef, far), axis=-1, keepdims=True)
    e_lo = N_GROUPS + g_idx * EXPERTS_PER_GROUP
    emask = (lanef >= e_lo) & (lanef < e_lo + EXPERTS_PER_GROUP)
    lem = jnp.where(emask, lg, NEG_BIG)
    me = jnp.max(lem, axis=-1, keepdims=True)
    ee = jnp.exp(lem - me)
    pe = jnp.where(emask, ee / jnp.sum(ee, axis=-1, keepdims=True), -1.0)
    p1 = jnp.max(pe, axis=-1, keepdims=True)
    i1 = jnp.min(jnp.where(pe == p1, lanef, far), axis=-1, keepdims=True)
    pe2 = jnp.where(lanef == i1, -1.0, pe)
    p2 = jnp.max(pe2, axis=-1, keepdims=True)
    i2 = jnp.min(jnp.where(pe2 == p2, lanef, far), axis=-1, keepdims=True)
    den = p1 + p2
    gate1 = g_prob * (p1 / den)
    gate2 = g_prob * (p2 / den)
    out = jnp.where(lane == 0, gate1, 0.0)
    out = jnp.where(lane == 1, gate2, out)
    out = jnp.where(lane == 2, i1 - N_GROUPS, out)
    out = jnp.where(lane == 3, i2 - N_GROUPS, out)
    return out


def _post_attn_kernel(a_ref, wo_ref, x_ref, g1_ref, sc2_ref, sh2_ref, lng_ref, lnb_ref, wr_ref, br_ref,
                      xo_ref, h2_ref, route_ref, *, alpha):
    y = jnp.dot(a_ref[0], wo_ref[...], preferred_element_type=F32)
    z = alpha * x_ref[0] + (1.0 + g1_ref[0]) * y
    xn = _layer_norm_rows(z, lng_ref[...], lnb_ref[...])
    xo_ref[0] = xn
    h2 = xn * (1.0 + sc2_ref[0]) + sh2_ref[0]
    h2_ref[0] = h2
    h_hi, h_lo = _split_bf16(h2)
    t = jnp.dot(h_hi, wr_ref[...], preferred_element_type=F32)
    lg = (t[:, :LANES] + t[:, LANES:] + jnp.dot(h_lo, wr_ref[:, :LANES], preferred_element_type=F32)
          + br_ref[...])
    route_ref[0] = _router(lg)


def _post_attn(a, wo_bf16, x, g1, sc2, sh2, ln_g, ln_b, wr, br, alpha):
    b, s, d = x.shape
    const = lambda i, j: (0, 0)
    tile = lambda i, j: (i, j, 0)
    per_b = lambda i, j: (i, 0, 0)
    return pl.pallas_call(
        functools.partial(_post_attn_kernel, alpha=alpha),
        grid=(b, s // ROW_TILE),
        in_specs=[
            pl.BlockSpec((1, ROW_TILE, d), tile),
            pl.BlockSpec((d, d), const),
            pl.BlockSpec((1, ROW_TILE, d), tile),
            pl.BlockSpec((1, 1, d), per_b),
            pl.BlockSpec((1, 1, d), per_b),
            pl.BlockSpec((1, 1, d), per_b),
            pl.BlockSpec((1, d), const),
            pl.BlockSpec((1, d), const),
            pl.BlockSpec((d, 2 * LANES), const),
            pl.BlockSpec((1, LANES), const),
        ],
        out_specs=[
            pl.BlockSpec((1, ROW_TILE, d), tile),
            pl.BlockSpec((1, ROW_TILE, d), tile),
            pl.BlockSpec((1, ROW_TILE, LANES), tile),
        ],
        out_shape=[
            jax.ShapeDtypeStruct((b, s, d), F32),
            jax.ShapeDtypeStruct((b, s, d), F32),
            jax.ShapeDtypeStruct((b, s, LANES), F32),
        ],
        compiler_params=_cparams(("arbitrary", "arbitrary")),
        name="post_attn",
    )(a, wo_bf16, x, g1, sc2, sh2, ln_g, ln_b, wr, br)


def _row_copy(src_hbm, dst_vmem, src_row, dst_row, sem):
    return pltpu.make_async_copy(src_hbm.at[pl.ds(src_row, 1)], dst_vmem.at[pl.ds(dst_row, 1)], sem)


def _expert_kernel(be_ref, nused_ref, tok_ref, h_hbm, wg_ref, wu_ref, wd_ref, y_ref,
                   xbuf, wgb, wub, wdb, sem):
    i = pl.program_id(0)
    tb = xbuf.shape[0]

    @pl.when(i < nused_ref[0])
    def _():
        def issue(j, carry):
            _row_copy(h_hbm, xbuf, tok_ref[0, 0, j], j, sem).start()
            return carry

        lax.fori_loop(0, tb, issue, 0)

        @pl.when((i == 0) | (be_ref[i] != be_ref[jnp.maximum(i - 1, 0)]))
        def _():
            wgb[...] = wg_ref[0].astype(BF16)
            wub[...] = wu_ref[0].astype(BF16)
            wdb[...] = wd_ref[0].astype(BF16)

        pltpu.make_async_copy(h_hbm.at[pl.ds(0, tb)], xbuf, sem).wait()
        xb = xbuf[...].astype(BF16)
        g = jnp.dot(xb, wgb[...], preferred_element_type=F32)
        u = jnp.dot(xb, wub[...], preferred_element_type=F32)
        act = (g * jax.nn.sigmoid(g) * u).astype(BF16)
        y_ref[...] = jnp.dot(act, wdb[...], preferred_element_type=F32)

    @pl.when(i >= nused_ref[0])
    def _():
        y_ref[...] = jnp.zeros_like(y_ref)


def _expert_mlps(block_expert, n_used, row_tok, h2, w_gate, w_up, w_down):
    n_blocks = block_expert.shape[0]
    n_tok, d = h2.shape
    de = w_gate.shape[2]
    grid_spec = pltpu.PrefetchScalarGridSpec(
        num_scalar_prefetch=2,
        grid=(n_blocks,),
        in_specs=[
            pl.BlockSpec((1, 1, MOE_TB), lambda i, be, nu: (i, 0, 0), memory_space=pltpu.SMEM),
            pl.BlockSpec(memory_space=pl.ANY),
            pl.BlockSpec((1, d, de), lambda i, be, nu: (be[i], 0, 0)),
            pl.BlockSpec((1, d, de), lambda i, be, nu: (be[i], 0, 0)),
            pl.BlockSpec((1, de, d), lambda i, be, nu: (be[i], 0, 0)),
        ],
        out_specs=pl.BlockSpec((MOE_TB, d), lambda i, be, nu: (i, 0)),
        scratch_shapes=[
            pltpu.VMEM((MOE_TB, d), F32),
            pltpu.VMEM((d, de), BF16),
            pltpu.VMEM((d, de), BF16),
            pltpu.VMEM((de, d), BF16),
            pltpu.SemaphoreType.DMA,
        ],
    )
    return pl.pallas_call(
        _expert_kernel,
        grid_spec=grid_spec,
        out_shape=jax.ShapeDtypeStruct((n_blocks * MOE_TB, d), F32),
        compiler_params=_cparams(("arbitrary",)),
        name="expert_mlps",
    )(block_expert, n_used, row_tok.reshape(n_blocks, 1, MOE_TB), h2, w_gate, w_up, w_down)


def _combine_kernel(pos_ref, ys_hbm, x_ref, route_ref, g2_ref, lng_ref, lnb_ref, xo_ref, buf0, buf1, sem,
                    *, alpha):
    tm = buf0.shape[0]

    def issue(j, carry):
        _row_copy(ys_hbm, buf0, pos_ref[0, 0, 2 * j], j, sem).start()
        _row_copy(ys_hbm, buf1, pos_ref[0, 0, 2 * j + 1], j, sem).start()
        return carry

    lax.fori_loop(0, tm, issue, 0)
    route = route_ref[...]
    lane = lax.broadcasted_iota(jnp.int32, route.shape, 1)
    gate0 = jnp.sum(jnp.where(lane == 0, route, 0.0), axis=-1, keepdims=True)
    gate1 = jnp.sum(jnp.where(lane == 1, route, 0.0), axis=-1, keepdims=True)
    pltpu.make_async_copy(ys_hbm.at[pl.ds(0, tm)], buf0, sem).wait()
    pltpu.make_async_copy(ys_hbm.at[pl.ds(0, tm)], buf1, sem).wait()
    y = buf0[...] * gate0 + buf1[...] * gate1
    z = alpha * x_ref[...] + (1.0 + g2_ref[0]) * y
    xo_ref[...] = _layer_norm_rows(z, lng_ref[...], lnb_ref[...])


def _combine(pos, ys, x2d, route2d, g2, ln_g, ln_b, seq, alpha):
    n_tok, d = x2d.shape
    tm = COMBINE_TM
    n_tiles = n_tok // tm
    tiles_per_batch = seq // tm
    const = lambda i: (0, 0)
    return pl.pallas_call(
        functools.partial(_combine_kernel, alpha=alpha),
        grid=(n_tiles,),
        in_specs=[
            pl.BlockSpec((1, 1, TOP_K * tm), lambda i: (i, 0, 0), memory_space=pltpu.SMEM),
            pl.BlockSpec(memory_space=pl.ANY),
            pl.BlockSpec((tm, d), lambda i: (i, 0)),
            pl.BlockSpec((tm, LANES), lambda i: (i, 0)),
            pl.BlockSpec((1, 1, d), lambda i: (i // tiles_per_batch, 0, 0)),
            pl.BlockSpec((1, d), const),
            pl.BlockSpec((1, d), const),
        ],
        out_specs=pl.BlockSpec((tm, d), lambda i: (i, 0)),
        out_shape=jax.ShapeDtypeStruct((n_tok, d), F32),
        scratch_shapes=[
            pltpu.VMEM((tm, d), F32),
            pltpu.VMEM((tm, d), F32),
            pltpu.SemaphoreType.DMA,
        ],
        compiler_params=_cparams(("arbitrary",)),
        name="moe_combine",
    )(pos.reshape(n_tiles, 1, TOP_K * tm), ys, x2d, route2d, g2, ln_g, ln_b)


def _routing_tables(experts, n_blocks):
    a = experts.size
    flat_e = experts.reshape(a)
    onehot = (flat_e[:, None] == jnp.arange(N_EXPERTS, dtype=jnp.int32)[None, :]).astype(jnp.int32)
    csum = jnp.cumsum(onehot, axis=0)
    counts = csum[-1]
    rank = jnp.take_along_axis(csum, flat_e[:, None], axis=1)[:, 0] - 1
    padded = (counts + MOE_TB - 1) // MOE_TB * MOE_TB
    pad_end = jnp.cumsum(padded)
    pad_start = pad_end - padded
    pos = pad_start[flat_e] + rank
    row_tok = jnp.zeros((n_blocks * MOE_TB,), jnp.int32).at[pos].set(
        jnp.arange(a, dtype=jnp.int32) // TOP_K)
    n_used = pad_end[-1] // MOE_TB
    blk = jnp.arange(n_blocks, dtype=jnp.int32)
    be = jnp.minimum(jnp.searchsorted(pad_end, blk * MOE_TB, side="right"), N_EXPERTS - 1).astype(jnp.int32)
    be = jnp.where(blk < n_used, be, be[jnp.maximum(n_used - 1, 0)])
    return pos.astype(jnp.int32), row_tok, be, n_used.reshape(1).astype(jnp.int32)


def _rope_tables(seq, width):
    t = jnp.arange(seq, dtype=jnp.int32)
    pos = jnp.stack([t // GRID_W, t % GRID_W], axis=-1).astype(F32)
    inv_freq = ROPE_THETA ** (-jnp.arange(ROPE_PAIRS, dtype=F32) / ROPE_PAIRS)
    ang = pos[:, :, None] * inv_freq
    cos, sin = jnp.cos(ang), jnp.sin(ang)
    cosf = jnp.stack([cos, cos], axis=2).reshape(seq, HEAD_DIM)
    sinf = jnp.stack([-sin, sin], axis=2).reshape(seq, HEAD_DIM)
    reps = width // HEAD_DIM
    return jnp.tile(cosf, (1, reps)), jnp.tile(sinf, (1, reps))


def _dup_heads(w):
    d = w.shape[0]
    w4 = w.reshape(d, GQA_KV_HEADS, 1, HEAD_DIM)
    return jnp.concatenate([w4, w4], axis=2).reshape(d, GQA_KV_HEADS * HEAD_PAIR)


def kernel(x, c, ada_w, ada_b, ln_g, ln_b, na_w_qkv, na_rpb, na_w_o, gqa_w_qkv, gqa_q_norm, gqa_k_norm, gqa_w_o,
           moe_w_group, moe_b_group, moe_w_expert, moe_b_expert, moe_w_gate, moe_w_up, moe_w_down):
    b, s, d = x.shape
    depth = ada_w.shape[0]
    n_tok = b * s
    rows = s // GRID_W
    kvd = GQA_KV_HEADS * HEAD_DIM
    alpha = (2 * depth) ** 0.25
    n_blocks = (n_tok * TOP_K) // MOE_TB + N_EXPERTS

    mod = _ada_modulation(c, ada_w, ada_b).reshape(depth, b, 6, 1, d)

    cosf, sinf = _rope_tables(s, QKV_CHUNK)
    head_of_lane = jnp.arange(QKV_CHUNK, dtype=jnp.int32) // HEAD_DIM
    seg_t = (head_of_lane[None, :] == jnp.arange(LANES, dtype=jnp.int32)[:, None])
    seg = (seg_t.T.astype(F32) / HEAD_DIM).astype(BF16)
    seg_t = seg_t.astype(BF16)

    for i in range(depth):
        sh1, sc1, g1, sh2, sc2, g2 = (mod[i, :, k] for k in range(6))
        j = i // 2
        if i % 2 == 0:
            qkv = _qkv_na(x, sc1, sh1, na_w_qkv[j].astype(BF16))
            attn = _na_attention(qkv, _na_bias_table(na_rpb[j], rows), rows)
            w_o = na_w_o[j]
        else:
            w = gqa_w_qkv[j]
            w_all = jnp.concatenate([w[:, :d], _dup_heads(w[:, d:d + kvd]), _dup_heads(w[:, d + kvd:])], axis=1)
            reps = QKV_CHUNK // HEAD_DIM
            qkv = _qkv_gqa(x, sc1, sh1, w_all.astype(BF16),
                           jnp.tile(gqa_q_norm[j], reps)[None, :], jnp.tile(gqa_k_norm[j], reps)[None, :],
                           cosf, sinf, seg, seg_t)
            attn = _gqa_attention(qkv, d)
            w_o = gqa_w_o[j]

        w_r = jnp.concatenate([moe_w_group[i], moe_w_expert[i]], axis=1)
        w_r = jnp.pad(w_r, ((0, 0), (0, LANES - w_r.shape[1])))
        wr_hi, wr_lo = _split_bf16(w_r)
        b_r = jnp.concatenate([moe_b_group[i], moe_b_expert[i]])
        b_r = jnp.pad(b_r, (0, LANES - b_r.shape[0]))[None, :]
        x, h2, route = _post_attn(attn, w_o.astype(BF16), x, g1, sc2, sh2, ln_g[i, 0][None, :], ln_b[i, 0][None, :],
                                  jnp.concatenate([wr_hi, wr_lo], axis=1), b_r, alpha)

        route2d = route.reshape(n_tok, LANES)
        experts = route2d[:, 2:2 + TOP_K].astype(jnp.int32)
        pos, row_tok, block_expert, n_used = _routing_tables(experts, n_blocks)
        ys = _expert_mlps(block_expert, n_used, row_tok, h2.reshape(n_tok, d),
                          moe_w_gate[i], moe_w_up[i], moe_w_down[i])
        x = _combine(pos, ys, x.reshape(n_tok, d), route2d, g2, ln_g[i, 1][None, :], ln_b[i, 1][None, :],
                     s, alpha).reshape(b, s, d)
    return x
```

```python
import functools

import jax
import jax.numpy as jnp
from jax import lax
from jax.experimental import pallas as pl
from jax.experimental.pallas import tpu as pltpu

GRID_W = 64
N_HEADS = 16
HEAD_DIM = 64
NA_ROWS = 8
NA_COLS = 16
GQA_KV_HEADS = 4
ROPE_THETA = 10000.0
ROPE_PAIRS = HEAD_DIM // 4
N_GROUPS = 4
EXPERTS_PER_GROUP = 8
N_EXPERTS = N_GROUPS * EXPERTS_PER_GROUP
TOP_K = 2
LN_EPS = 1e-5
RMS_EPS = 1e-6

LANES = 128
HEAD_PAIR = 2 * HEAD_DIM
VMEM_LIMIT_BYTES = 52 * 1024 * 1024
ADA_TN = 1536
ROW_TILE = 512
QKV_CHUNK = 512
GQA_TQ = 256
MOE_TB = 256
COMBINE_TM = 256
ROUTE_T = 512
DMA_UNROLL = 8
NEG_BIG = -1e30

F32 = jnp.float32
BF16 = jnp.bfloat16


def _cparams(sem):
    return pltpu.CompilerParams(dimension_semantics=sem, vmem_limit_bytes=VMEM_LIMIT_BYTES)


def _split_bf16(a):
    hi = a.astype(BF16)
    lo = (a - hi.astype(F32)).astype(BF16)
    return hi, lo


def _ada_kernel(c_ref, w_ref, b_ref, o_ref):
    c = c_ref[...]
    ca = (c * jax.nn.sigmoid(c)).astype(BF16)
    o_ref[0] = jnp.dot(ca, w_ref[0].astype(BF16), preferred_element_type=F32) + b_ref[0]


def _ada_modulation(c, ada_w, ada_b):
    depth, d, n = ada_w.shape
    b = c.shape[0]
    return pl.pallas_call(
        _ada_kernel,
        grid=(depth, n // ADA_TN),
        in_specs=[
            pl.BlockSpec((b, d), lambda i, j: (0, 0)),
            pl.BlockSpec((1, d, ADA_TN), lambda i, j: (i, 0, j)),
            pl.BlockSpec((1, 1, ADA_TN), lambda i, j: (i, 0, j)),
        ],
        out_specs=pl.BlockSpec((1, b, ADA_TN), lambda i, j: (i, 0, j)),
        out_shape=jax.ShapeDtypeStruct((depth, b, n), F32),
        compiler_params=_cparams(("arbitrary", "arbitrary")),
        name="ada_modulation",
    )(c, ada_w, ada_b.reshape(depth, 1, n))


def _qkv_na_kernel(x_ref, sc_ref, sh_ref, w_ref, o_ref, *, d_model):
    h = (x_ref[0] * (1.0 + sc_ref[0]) + sh_ref[0]).astype(BF16)
    n = w_ref.shape[1]
    for n0 in range(0, n, QKV_CHUNK):
        acc = jnp.dot(h, w_ref[:, n0:n0 + QKV_CHUNK], preferred_element_type=F32)
        if n0 < d_model:
            acc = acc * (HEAD_DIM ** -0.5)
        o_ref[0, :, n0:n0 + QKV_CHUNK] = acc.astype(BF16)


def _qkv_na(x, sc, sh, w_bf16):
    b, s, d = x.shape
    n = w_bf16.shape[1]
    return pl.pallas_call(
        functools.partial(_qkv_na_kernel, d_model=d),
        grid=(b, s // ROW_TILE),
        in_specs=[
            pl.BlockSpec((1, ROW_TILE, d), lambda i, j: (i, j, 0)),
            pl.BlockSpec((1, 1, d), lambda i, j: (i, 0, 0)),
            pl.BlockSpec((1, 1, d), lambda i, j: (i, 0, 0)),
            pl.BlockSpec((d, n), lambda i, j: (0, 0)),
        ],
        out_specs=pl.BlockSpec((1, ROW_TILE, n), lambda i, j: (i, j, 0)),
        out_shape=jax.ShapeDtypeStruct((b, s, n), BF16),
        compiler_params=_cparams(("arbitrary", "arbitrary")),
        name="qkv_na",
    )(x, sc, sh, w_bf16)


N_DR = 2 * NA_ROWS - 1
N_DC = 2 * NA_COLS - 1


def _na_bias_kernel(r_ref, m_ref, valid_ref, o_ref):
    r = r_ref[...]
    hi = r.astype(BF16)
    r1 = r - hi.astype(F32)
    mid = r1.astype(BF16)
    lo = (r1 - mid.astype(F32)).astype(BF16)
    m = m_ref[...]
    acc = (jnp.dot(hi, m, preferred_element_type=F32) + jnp.dot(mid, m, preferred_element_type=F32)
           + jnp.dot(lo, m, preferred_element_type=F32))
    o_ref[...] = jnp.where(valid_ref[...] > 0.0, acc, NEG_BIG)


def _na_bias_table(rpb):
    qc = jnp.arange(GRID_W, dtype=jnp.int32)
    kc = jnp.arange(GRID_W, dtype=jnp.int32)
    win_start = jnp.clip(qc - NA_COLS // 2, 0, GRID_W - NA_COLS)
    valid = (kc[None, :] >= win_start[:, None]) & (kc[None, :] < win_start[:, None] + NA_COLS)
    dc = jnp.clip(kc[None, :] - qc[:, None] + NA_COLS - 1, 0, N_DC - 1)
    onehot = (jnp.arange(LANES, dtype=jnp.int32)[:, None] == dc.reshape(1, GRID_W * GRID_W)).astype(BF16)
    r2d = jnp.pad(rpb.reshape(N_HEADS * N_DR, N_DC).astype(F32), ((0, 0), (0, LANES - N_DC)))
    tiles = pl.pallas_call(
        _na_bias_kernel,
        out_shape=jax.ShapeDtypeStruct((N_HEADS * N_DR, GRID_W * GRID_W), F32),
        compiler_params=pltpu.CompilerParams(vmem_limit_bytes=VMEM_LIMIT_BYTES),
        name="na_bias_tiles",
    )(r2d, onehot, valid.reshape(1, GRID_W * GRID_W).astype(F32))
    tiles = tiles.reshape(N_HEADS, N_DR, GRID_W, GRID_W)
    pairs = jnp.concatenate([tiles[:, :N_DR - 1], tiles[:, 1:]], axis=-1)
    pairs = pairs.reshape(N_HEADS // 2, 2, N_DR - 1, GRID_W, HEAD_PAIR).transpose(0, 2, 1, 3, 4)
    return pairs.reshape(N_HEADS // 2, N_DR - 1, 2 * GRID_W, HEAD_PAIR)


def _na_kernel(q_ref, k_ref, v_ref, bias_ref, o_ref, *, rows):
    r = pl.program_id(1)
    row_start = jnp.clip(r - NA_ROWS // 2, 0, rows - NA_ROWS)
    start = pl.multiple_of(row_start * GRID_W, GRID_W)
    nk = NA_ROWS * GRID_W
    dr_base = row_start - r + (NA_ROWS - 1)
    lane = lax.broadcasted_iota(jnp.int32, (GRID_W, HEAD_PAIR), 1)
    first = lane < HEAD_DIM
    for hp in range(N_HEADS // 2):
        cols = slice(hp * HEAD_PAIR, (hp + 1) * HEAD_PAIR)
        q2 = q_ref[0, :, cols]
        k2 = k_ref[0, pl.ds(start, nk), cols]
        v2 = v_ref[0, pl.ds(start, nk), cols]
        zero = jnp.zeros_like(q2)
        qs = jnp.concatenate([jnp.where(first, q2, zero), jnp.where(first, zero, q2)], axis=0)
        s = lax.dot_general(qs, k2, (((1,), (1,)), ((), ())), preferred_element_type=F32)
        s = s + jnp.concatenate([bias_ref[hp, dr_base + 2 * m] for m in range(NA_ROWS // 2)], axis=1)
        m = jnp.max(s, axis=-1, keepdims=True)
        p = jnp.exp(s - m)
        l = jnp.sum(p, axis=-1, keepdims=True)
        o = jnp.dot(p.astype(BF16), v2, preferred_element_type=F32) / l
        o_ref[0, :, cols] = jnp.where(first, o[:GRID_W], o[GRID_W:]).astype(BF16)


def _na_attention(qkv, bias, rows):
    b, s, n3 = qkv.shape
    d = n3 // 3
    assert rows >= NA_ROWS and NA_ROWS % 2 == 0
    return pl.pallas_call(
        functools.partial(_na_kernel, rows=rows),
        grid=(b, rows),
        in_specs=[
            pl.BlockSpec((1, GRID_W, d), lambda i, r: (i, r, 0)),
            pl.BlockSpec((1, s, d), lambda i, r: (i, 0, 1)),
            pl.BlockSpec((1, s, d), lambda i, r: (i, 0, 2)),
            pl.BlockSpec(bias.shape, lambda i, r: (0, 0, 0, 0)),
        ],
        out_specs=pl.BlockSpec((1, GRID_W, d), lambda i, r: (i, r, 0)),
        out_shape=jax.ShapeDtypeStruct((b, s, d), BF16),
        compiler_params=_cparams(("arbitrary", "arbitrary")),
        name="na_attention",
    )(qkv, qkv, qkv, bias)


def _rms_rope(a, gain, cosf, sinf, seg, seg_t):
    qq_hi, qq_lo = _split_bf16(a * a)
    ms = jnp.dot(qq_hi, seg, preferred_element_type=F32) + jnp.dot(qq_lo, seg, preferred_element_type=F32)
    r_hi, r_lo = _split_bf16(lax.rsqrt(ms + RMS_EPS))
    rf = jnp.dot(r_hi, seg_t, preferred_element_type=F32) + jnp.dot(r_lo, seg_t, preferred_element_type=F32)
    an = a * rf * gain
    n = a.shape[1]
    lane = lax.broadcasted_iota(jnp.int32, a.shape, 1)
    partner = jnp.where(lane % (2 * ROPE_PAIRS) < ROPE_PAIRS,
                        pltpu.roll(an, n - ROPE_PAIRS, 1), pltpu.roll(an, ROPE_PAIRS, 1))
    return an * cosf + partner * sinf


def _qkv_gqa_kernel(x_ref, sc_ref, sh_ref, w_ref, qg_ref, kg_ref, cos_ref, sin_ref, seg_ref, segt_ref, o_ref,
                    *, d_model):
    h = (x_ref[0] * (1.0 + sc_ref[0]) + sh_ref[0]).astype(BF16)
    n = w_ref.shape[1]
    kv_cols = (n - d_model) // 2
    cosf, sinf = cos_ref[...], sin_ref[...]
    seg, seg_t = seg_ref[...], segt_ref[...]
    for n0 in range(0, n, QKV_CHUNK):
        acc = jnp.dot(h, w_ref[:, n0:n0 + QKV_CHUNK], preferred_element_type=F32)
        if n0 < d_model:
            acc = _rms_rope(acc, qg_ref[...], cosf, sinf, seg, seg_t) * (HEAD_DIM ** -0.5)
        elif n0 < d_model + kv_cols:
            acc = _rms_rope(acc, kg_ref[...], cosf, sinf, seg, seg_t)
        o_ref[0, :, n0:n0 + QKV_CHUNK] = acc.astype(BF16)


def _qkv_gqa(x, sc, sh, w_bf16, q_gain, k_gain, cosf, sinf, seg, seg_t):
    b, s, d = x.shape
    n = w_bf16.shape[1]
    const = lambda j, i: (0, 0)
    return pl.pallas_call(
        functools.partial(_qkv_gqa_kernel, d_model=d),
        grid=(s // ROW_TILE, b),
        in_specs=[
            pl.BlockSpec((1, ROW_TILE, d), lambda j, i: (i, j, 0)),
            pl.BlockSpec((1, 1, d), lambda j, i: (i, 0, 0)),
            pl.BlockSpec((1, 1, d), lambda j, i: (i, 0, 0)),
            pl.BlockSpec((d, n), const),
            pl.BlockSpec((1, QKV_CHUNK), const),
            pl.BlockSpec((1, QKV_CHUNK), const),
            pl.BlockSpec((ROW_TILE, QKV_CHUNK), lambda j, i: (j, 0)),
            pl.BlockSpec((ROW_TILE, QKV_CHUNK), lambda j, i: (j, 0)),
            pl.BlockSpec((QKV_CHUNK, LANES), const),
            pl.BlockSpec((LANES, QKV_CHUNK), const),
        ],
        out_specs=pl.BlockSpec((1, ROW_TILE, n), lambda j, i: (i, j, 0)),
        out_shape=jax.ShapeDtypeStruct((b, s, n), BF16),
        compiler_params=_cparams(("arbitrary", "arbitrary")),
        name="qkv_gqa",
    )(x, sc, sh, w_bf16, q_gain, k_gain, cosf, sinf, seg, seg_t)


def _gqa_kernel(q_ref, k_ref, v_ref, o_ref):
    k2 = k_ref[0]
    v2 = v_ref[0]
    tq = q_ref.shape[1]
    lane = lax.broadcasted_iota(jnp.int32, (tq, HEAD_PAIR), 1)
    first = lane < HEAD_DIM
    for pair in range(q_ref.shape[2] // HEAD_PAIR):
        cols = slice(pair * HEAD_PAIR, (pair + 1) * HEAD_PAIR)
        q2 = q_ref[0, :, cols]
        zero = jnp.zeros_like(q2)
        qs = jnp.concatenate([jnp.where(first, q2, zero), jnp.where(first, zero, q2)], axis=0)
        s = lax.dot_general(qs, k2, (((1,), (1,)), ((), ())), preferred_element_type=F32)
        m = jnp.max(s, axis=-1, keepdims=True)
        p = jnp.exp(s - m)
        l = jnp.sum(p, axis=-1, keepdims=True)
        o = jnp.dot(p.astype(BF16), v2, preferred_element_type=F32) / l
        o_ref[0, :, cols] = jnp.where(first, o[:tq], o[tq:]).astype(BF16)


def _gqa_attention(qkv, d_model):
    b, s, n = qkv.shape
    group_cols = d_model // GQA_KV_HEADS
    k_block0 = d_model // HEAD_PAIR
    v_block0 = k_block0 + GQA_KV_HEADS
    return pl.pallas_call(
        _gqa_kernel,
        grid=(b, GQA_KV_HEADS, s // GQA_TQ),
        in_specs=[
            pl.BlockSpec((1, GQA_TQ, group_cols), lambda i, g, t: (i, t, g)),
            pl.BlockSpec((1, s, HEAD_PAIR), lambda i, g, t: (i, 0, k_block0 + g)),
            pl.BlockSpec((1, s, HEAD_PAIR), lambda i, g, t: (i, 0, v_block0 + g)),
        ],
        out_specs=pl.BlockSpec((1, GQA_TQ, group_cols), lambda i, g, t: (i, t, g)),
        out_shape=jax.ShapeDtypeStruct((b, s, d_model), BF16),
        compiler_params=_cparams(("arbitrary", "arbitrary", "arbitrary")),
        name="gqa_attention",
    )(qkv, qkv, qkv)


def _store_token_tiles(ref, val):
    t, d = val.shape
    sub = d // LANES
    for s in range(sub):
        ref[pl.ds(s, t, stride=sub), :] = val[:, s * LANES:(s + 1) * LANES]


def _load_token_tiles(ref, t, dtype):
    sub = ref.shape[0] // t
    return jnp.concatenate([ref[pl.ds(s, t, stride=sub), :].astype(dtype) for s in range(sub)], axis=1)


def _tile_copy(src_hbm, dst_vmem, src_tok, dst_tok, sub, sem):
    return pltpu.make_async_copy(src_hbm.at[pl.ds(pl.multiple_of(src_tok * sub, sub), sub)],
                                 dst_vmem.at[pl.ds(pl.multiple_of(dst_tok * sub, sub), sub)], sem)


def _layer_norm_rows(z, g, b):
    mu = jnp.mean(z, axis=-1, keepdims=True)
    zc = z - mu
    var = jnp.mean(zc * zc, axis=-1, keepdims=True)
    return zc * lax.rsqrt(var + LN_EPS) * g + b


def _router(lg):
    lane = lax.broadcasted_iota(jnp.int32, lg.shape, 1)
    lanef = lane.astype(F32)
    far = float(LANES)
    gmask = lane < N_GROUPS
    lgm = jnp.where(gmask, lg, NEG_BIG)
    mg = jnp.max(lgm, axis=-1, keepdims=True)
    sg = jnp.sum(jnp.exp(lgm - mg), axis=-1, keepdims=True)
    g_prob = 1.0 / sg
    g_idx = jnp.min(jnp.where(lgm == mg, lanef, far), axis=-1, keepdims=True)
    e_lo = N_GROUPS + g_idx * EXPERTS_PER_GROUP
    emask = (lanef >= e_lo) & (lanef < e_lo + EXPERTS_PER_GROUP)
    lem = jnp.where(emask, lg, NEG_BIG)
    me = jnp.max(lem, axis=-1, keepdims=True)
    ee = jnp.exp(lem - me)
    pe = jnp.where(emask, ee / jnp.sum(ee, axis=-1, keepdims=True), -1.0)
    p1 = jnp.max(pe, axis=-1, keepdims=True)
    i1 = jnp.min(jnp.where(pe == p1, lanef, far), axis=-1, keepdims=True)
    pe2 = jnp.where(lanef == i1, -1.0, pe)
    p2 = jnp.max(pe2, axis=-1, keepdims=True)
    i2 = jnp.min(jnp.where(pe2 == p2, lanef, far), axis=-1, keepdims=True)
    den = p1 + p2
    gate1 = g_prob * (p1 / den)
    gate2 = g_prob * (p2 / den)
    out = jnp.where(lane == 0, gate1, 0.0)
    out = jnp.where(lane == 1, gate2, out)
    out = jnp.where(lane == 2, i1 - N_GROUPS, out)
    out = jnp.where(lane == 3, i2 - N_GROUPS, out)
    return out


def _post_attn_kernel(a_ref, wo_ref, x_ref, g1_ref, sc2_ref, sh2_ref, lng_ref, lnb_ref, wr_ref, br_ref,
                      xo_ref, h2_ref, route_ref, *, alpha):
    y = jnp.dot(a_ref[0], wo_ref[...], preferred_element_type=F32)
    z = alpha * x_ref[0] + (1.0 + g1_ref[0]) * y
    xn = _layer_norm_rows(z, lng_ref[...], lnb_ref[...])
    xo_ref[0] = xn
    h2 = xn * (1.0 + sc2_ref[0]) + sh2_ref[0]
    _store_token_tiles(h2_ref, h2)
    h_hi, h_lo = _split_bf16(h2)
    t = jnp.dot(h_hi, wr_ref[...], preferred_element_type=F32)
    lg = (t[:, :LANES] + t[:, LANES:] + jnp.dot(h_lo, wr_ref[:, :LANES], preferred_element_type=F32)
          + br_ref[...])
    route_ref[0] = _router(lg)


def _post_attn(a, wo_bf16, x, g1, sc2, sh2, ln_g, ln_b, wr, br, alpha):
    b, s, d = x.shape
    sub = d // LANES
    tiles_per_batch = s // ROW_TILE
    const = lambda i, j: (0, 0)
    tile = lambda i, j: (i, j, 0)
    per_b = lambda i, j: (i, 0, 0)
    return pl.pallas_call(
        functools.partial(_post_attn_kernel, alpha=alpha),
        grid=(b, s // ROW_TILE),
        in_specs=[
            pl.BlockSpec((1, ROW_TILE, d), tile),
            pl.BlockSpec((d, d), const),
            pl.BlockSpec((1, ROW_TILE, d), tile),
            pl.BlockSpec((1, 1, d), per_b),
            pl.BlockSpec((1, 1, d), per_b),
            pl.BlockSpec((1, 1, d), per_b),
            pl.BlockSpec((1, d), const),
            pl.BlockSpec((1, d), const),
            pl.BlockSpec((d, 2 * LANES), const),
            pl.BlockSpec((1, LANES), const),
        ],
        out_specs=[
            pl.BlockSpec((1, ROW_TILE, d), tile),
            pl.BlockSpec((ROW_TILE * sub, LANES), lambda i, j: (i * tiles_per_batch + j, 0)),
            pl.BlockSpec((1, ROW_TILE, LANES), tile),
        ],
        out_shape=[
            jax.ShapeDtypeStruct((b, s, d), F32),
            jax.ShapeDtypeStruct((b * s * sub, LANES), F32),
            jax.ShapeDtypeStruct((b, s, LANES), F32),
        ],
        compiler_params=_cparams(("arbitrary", "arbitrary")),
        name="post_attn",
    )(a, wo_bf16, x, g1, sc2, sh2, ln_g, ln_b, wr, br)


def _expert_kernel(be_ref, nused_ref, tok_ref, h_hbm, wg_ref, wu_ref, wd_ref, y_ref,
                   xbuf, wgb, wub, wdb, sem):
    i = pl.program_id(0)
    tb = MOE_TB
    sub = xbuf.shape[0] // tb

    @pl.when(i < nused_ref[0])
    def _():
        def issue(j, carry):
            _tile_copy(h_hbm, xbuf, tok_ref[0, 0, j], j, sub, sem).start()
            return carry

        lax.fori_loop(0, tb, issue, 0, unroll=DMA_UNROLL)

        @pl.when((i == 0) | (be_ref[i] != be_ref[jnp.maximum(i - 1, 0)]))
        def _():
            wgb[...] = wg_ref[0].astype(BF16)
            wub[...] = wu_ref[0].astype(BF16)
            wdb[...] = wd_ref[0].astype(BF16)

        pltpu.make_async_copy(h_hbm.at[pl.ds(0, tb * sub)], xbuf, sem).wait()
        xb = _load_token_tiles(xbuf, tb, BF16)
        g = jnp.dot(xb, wgb[...], preferred_element_type=F32)
        u = jnp.dot(xb, wub[...], preferred_element_type=F32)
        act = (g * jax.nn.sigmoid(g) * u).astype(BF16)
        _store_token_tiles(y_ref, jnp.dot(act, wdb[...], preferred_element_type=F32))

    @pl.when(i >= nused_ref[0])
    def _():
        y_ref[...] = jnp.zeros_like(y_ref)


def _expert_mlps(block_expert, n_used, row_tok, h2_tiles, w_gate, w_up, w_down):
    n_blocks = block_expert.shape[0]
    _, d, de = w_gate.shape
    sub = d // LANES
    grid_spec = pltpu.PrefetchScalarGridSpec(
        num_scalar_prefetch=2,
        grid=(n_blocks,),
        in_specs=[
            pl.BlockSpec((1, 1, MOE_TB), lambda i, be, nu: (i, 0, 0), memory_space=pltpu.SMEM),
            pl.BlockSpec(memory_space=pl.ANY),
            pl.BlockSpec((1, d, de), lambda i, be, nu: (be[i], 0, 0)),
            pl.BlockSpec((1, d, de), lambda i, be, nu: (be[i], 0, 0)),
            pl.BlockSpec((1, de, d), lambda i, be, nu: (be[i], 0, 0)),
        ],
        out_specs=pl.BlockSpec((MOE_TB * sub, LANES), lambda i, be, nu: (i, 0)),
        scratch_shapes=[
            pltpu.VMEM((MOE_TB * sub, LANES), F32),
            pltpu.VMEM((d, de), BF16),
            pltpu.VMEM((d, de), BF16),
            pltpu.VMEM((de, d), BF16),
            pltpu.SemaphoreType.DMA,
        ],
    )
    return pl.pallas_call(
        _expert_kernel,
        grid_spec=grid_spec,
        out_shape=jax.ShapeDtypeStruct((n_blocks * MOE_TB * sub, LANES), F32),
        compiler_params=_cparams(("arbitrary",)),
        name="expert_mlps",
    )(block_expert, n_used, row_tok.reshape(n_blocks, 1, MOE_TB), h2_tiles, w_gate, w_up, w_down)


def _combine_kernel(pos_ref, ys_hbm, x_ref, route_ref, g2_ref, lng_ref, lnb_ref, xo_ref, buf0, buf1, sem,
                    *, alpha):
    tm = COMBINE_TM
    sub = buf0.shape[0] // tm

    def issue(j, carry):
        _tile_copy(ys_hbm, buf0, pos_ref[0, 0, 2 * j], j, sub, sem).start()
        _tile_copy(ys_hbm, buf1, pos_ref[0, 0, 2 * j + 1], j, sub, sem).start()
        return carry

    lax.fori_loop(0, tm, issue, 0, unroll=DMA_UNROLL)
    route = route_ref[...]
    lane = lax.broadcasted_iota(jnp.int32, route.shape, 1)
    gate0 = jnp.sum(jnp.where(lane == 0, route, 0.0), axis=-1, keepdims=True)
    gate1 = jnp.sum(jnp.where(lane == 1, route, 0.0), axis=-1, keepdims=True)
    pltpu.make_async_copy(ys_hbm.at[pl.ds(0, tm * sub)], buf0, sem).wait()
    pltpu.make_async_copy(ys_hbm.at[pl.ds(0, tm * sub)], buf1, sem).wait()
    y = _load_token_tiles(buf0, tm, F32) * gate0 + _load_token_tiles(buf1, tm, F32) * gate1
    z = alpha * x_ref[...] + (1.0 + g2_ref[0]) * y
    xo_ref[...] = _layer_norm_rows(z, lng_ref[...], lnb_ref[...])


def _combine(pos, ys, x2d, route2d, g2, ln_g, ln_b, seq, alpha):
    n_tok, d = x2d.shape
    tm = COMBINE_TM
    n_tiles = n_tok // tm
    tiles_per_batch = seq // tm
    const = lambda i: (0, 0)
    return pl.pallas_call(
        functools.partial(_combine_kernel, alpha=alpha),
        grid=(n_tiles,),
        in_specs=[
            pl.BlockSpec((1, 1, TOP_K * tm), lambda i: (i, 0, 0), memory_space=pltpu.SMEM),
            pl.BlockSpec(memory_space=pl.ANY),
            pl.BlockSpec((tm, d), lambda i: (i, 0)),
            pl.BlockSpec((tm, LANES), lambda i: (i, 0)),
            pl.BlockSpec((1, 1, d), lambda i: (i // tiles_per_batch, 0, 0)),
            pl.BlockSpec((1, d), const),
            pl.BlockSpec((1, d), const),
        ],
        out_specs=pl.BlockSpec((tm, d), lambda i: (i, 0)),
        out_shape=jax.ShapeDtypeStruct((n_tok, d), F32),
        scratch_shapes=[
            pltpu.VMEM((tm * (d // LANES), LANES), F32),
            pltpu.VMEM((tm * (d // LANES), LANES), F32),
            pltpu.SemaphoreType.DMA,
        ],
        compiler_params=_cparams(("arbitrary",)),
        name="moe_combine",
    )(pos.reshape(n_tiles, 1, TOP_K * tm), ys, x2d, route2d, g2, ln_g, ln_b)


def _route_pos_kernel(route_ref, pos_ref, cnt_ref, counts, base):
    p = pl.program_id(0)
    j = pl.program_id(1)
    route = route_ref[...]
    t = route.shape[0]
    lane = lax.broadcasted_iota(jnp.int32, route.shape, 1)
    lanef = lane.astype(F32)
    e1 = jnp.sum(jnp.where(lane == 2, route, 0.0), axis=-1, keepdims=True)
    e2 = jnp.sum(jnp.where(lane == 3, route, 0.0), axis=-1, keepdims=True)
    m1 = lanef == e1
    m2 = lanef == e2
    mem = jnp.where(m1 | m2, 1.0, 0.0)
    colsum = jnp.sum(mem, axis=0, keepdims=True)

    @pl.when((p == 0) & (j == 0))
    def _():
        counts[...] = jnp.zeros_like(counts)

    @pl.when(p == 0)
    def _():
        counts[...] = counts[...] + colsum

    @pl.when((p == 1) & (j == 0))
    def _():
        c = counts[...]
        cnt_ref[...] = c
        nblk = jnp.floor((c + (MOE_TB - 1)) * (1.0 / MOE_TB))
        r_i = lax.broadcasted_iota(jnp.int32, (LANES, LANES), 0)
        c_i = lax.broadcasted_iota(jnp.int32, (LANES, LANES), 1)
        incl = jnp.where(r_i <= c_i, 1.0, 0.0).astype(BF16)
        end = jnp.dot(nblk.astype(BF16), incl, preferred_element_type=F32)
        base[...] = (end - nblk) * MOE_TB

    @pl.when(p == 1)
    def _():
        r_i = lax.broadcasted_iota(jnp.int32, (t, t), 0)
        c_i = lax.broadcasted_iota(jnp.int32, (t, t), 1)
        before = jnp.where(c_i < r_i, 1.0, 0.0).astype(BF16)
        rank = base[0:1, :] + jnp.dot(before, mem.astype(BF16), preferred_element_type=F32)
        pos1 = jnp.sum(jnp.where(m1, rank, 0.0), axis=-1, keepdims=True)
        pos2 = jnp.sum(jnp.where(m2, rank, 0.0), axis=-1, keepdims=True)
        pos_ref[...] = jnp.where(lane == 0, pos1, jnp.where(lane == 1, pos2, 0.0))
        base[...] = base[...] + colsum


def _routing_tables(route2d, n_blocks):
    n_tok = route2d.shape[0]
    n_tiles = n_tok // ROUTE_T
    posf, cnt = pl.pallas_call(
        _route_pos_kernel,
        grid=(2, n_tiles),
        in_specs=[pl.BlockSpec((ROUTE_T, LANES), lambda p, j: (j, 0))],
        out_specs=[
            pl.BlockSpec((ROUTE_T, LANES), lambda p, j: (p * j, 0)),
            pl.BlockSpec((8, LANES), lambda p, j: (0, 0)),
        ],
        out_shape=[
            jax.ShapeDtypeStruct((n_tok, LANES), F32),
            jax.ShapeDtypeStruct((8, LANES), F32),
        ],
        scratch_shapes=[pltpu.VMEM((8, LANES), F32), pltpu.VMEM((8, LANES), F32)],
        compiler_params=_cparams(("arbitrary", "arbitrary")),
        name="route_positions",
    )(route2d)
    pos = posf[:, :TOP_K].astype(jnp.int32).reshape(n_tok * TOP_K)
    row_tok = jnp.zeros((n_blocks * MOE_TB,), jnp.int32).at[pos].set(
        jnp.arange(n_tok * TOP_K, dtype=jnp.int32) // TOP_K)
    counts = cnt[0, :N_EXPERTS].astype(jnp.int32)
    end_blk = jnp.cumsum((counts + MOE_TB - 1) // MOE_TB)
    n_used = end_blk[-1]
    blk = jnp.arange(n_blocks, dtype=jnp.int32)
    be = jnp.minimum(jnp.searchsorted(end_blk, blk, side="right"), N_EXPERTS - 1).astype(jnp.int32)
    be = jnp.where(blk < n_used, be, be[jnp.maximum(n_used - 1, 0)])
    return pos, row_tok, be, n_used.reshape(1).astype(jnp.int32)


def _rope_tables(seq, width):
    t = jnp.arange(seq, dtype=jnp.int32)
    pos = jnp.stack([t // GRID_W, t % GRID_W], axis=-1).astype(F32)
    inv_freq = ROPE_THETA ** (-jnp.arange(ROPE_PAIRS, dtype=F32) / ROPE_PAIRS)
    ang = pos[:, :, None] * inv_freq
    cos, sin = jnp.cos(ang), jnp.sin(ang)
    cosf = jnp.stack([cos, cos], axis=2).reshape(seq, HEAD_DIM)
    sinf = jnp.stack([-sin, sin], axis=2).reshape(seq, HEAD_DIM)
    reps = width // HEAD_DIM
    return jnp.tile(cosf, (1, reps)), jnp.tile(sinf, (1, reps))


def _dup_heads(w):
    d = w.shape[0]
    w4 = w.reshape(d, GQA_KV_HEADS, 1, HEAD_DIM)
    return jnp.concatenate([w4, w4], axis=2).reshape(d, GQA_KV_HEADS * HEAD_PAIR)


def kernel(x, c, ada_w, ada_b, ln_g, ln_b, na_w_qkv, na_rpb, na_w_o, gqa_w_qkv, gqa_q_norm, gqa_k_norm, gqa_w_o,
           moe_w_group, moe_b_group, moe_w_expert, moe_b_expert, moe_w_gate, moe_w_up, moe_w_down):
    b, s, d = x.shape
    depth = ada_w.shape[0]
    n_tok = b * s
    rows = s // GRID_W
    kvd = GQA_KV_HEADS * HEAD_DIM
    alpha = (2 * depth) ** 0.25
    n_blocks = (n_tok * TOP_K) // MOE_TB + N_EXPERTS

    mod = _ada_modulation(c, ada_w, ada_b).reshape(depth, b, 6, 1, d)

    cosf, sinf = _rope_tables(s, QKV_CHUNK)
    head_of_lane = jnp.arange(QKV_CHUNK, dtype=jnp.int32) // HEAD_DIM
    seg_t = (head_of_lane[None, :] == jnp.arange(LANES, dtype=jnp.int32)[:, None])
    seg = (seg_t.T.astype(F32) / HEAD_DIM).astype(BF16)
    seg_t = seg_t.astype(BF16)

    for i in range(depth):
        sh1, sc1, g1, sh2, sc2, g2 = (mod[i, :, k] for k in range(6))
        j = i // 2
        if i % 2 == 0:
            qkv = _qkv_na(x, sc1, sh1, na_w_qkv[j].astype(BF16))
            attn = _na_attention(qkv, _na_bias_table(na_rpb[j]), rows)
            w_o = na_w_o[j]
        else:
            w = gqa_w_qkv[j]
            w_all = jnp.concatenate([w[:, :d], _dup_heads(w[:, d:d + kvd]), _dup_heads(w[:, d + kvd:])], axis=1)
            reps = QKV_CHUNK // HEAD_DIM
            qkv = _qkv_gqa(x, sc1, sh1, w_all.astype(BF16),
                           jnp.tile(gqa_q_norm[j], reps)[None, :], jnp.tile(gqa_k_norm[j], reps)[None, :],
                           cosf, sinf, seg, seg_t)
            attn = _gqa_attention(qkv, d)
            w_o = gqa_w_o[j]

        w_r = jnp.concatenate([moe_w_group[i], moe_w_expert[i]], axis=1)
        w_r = jnp.pad(w_r, ((0, 0), (0, LANES - w_r.shape[1])))
        wr_hi, wr_lo = _split_bf16(w_r)
        b_r = jnp.concatenate([moe_b_group[i], moe_b_expert[i]])
        b_r = jnp.pad(b_r, (0, LANES - b_r.shape[0]))[None, :]
        x, h2, route = _post_attn(attn, w_o.astype(BF16), x, g1, sc2, sh2, ln_g[i, 0][None, :], ln_b[i, 0][None, :],
                                  jnp.concatenate([wr_hi, wr_lo], axis=1), b_r, alpha)

        route2d = route.reshape(n_tok, LANES)
        pos, row_tok, block_expert, n_used = _routing_tables(route2d, n_blocks)
        ys = _expert_mlps(block_expert, n_used, row_tok, h2, moe_w_gate[i], moe_w_up[i], moe_w_down[i])
        x = _combine(pos, ys, x.reshape(n_tok, d), route2d, g2, ln_g[i, 1][None, :], ln_b[i, 1][None, :],
                     s, alpha).reshape(b, s, d)
    return x
```

```python
import functools

import jax
import jax.numpy as jnp
from jax import lax
from jax.experimental import pallas as pl
from jax.experimental.pallas import tpu as pltpu

GRID_W = 64
N_HEADS = 16
HEAD_DIM = 64
NA_ROWS = 8
NA_COLS = 16
GQA_KV_HEADS = 4
ROPE_THETA = 10000.0
ROPE_PAIRS = HEAD_DIM // 4
N_GROUPS = 4
EXPERTS_PER_GROUP = 8
N_EXPERTS = N_GROUPS * EXPERTS_PER_GROUP
TOP_K = 2
LN_EPS = 1e-5
RMS_EPS = 1e-6

LANES = 128
HEAD_PAIR = 2 * HEAD_DIM
VMEM_LIMIT_BYTES = 52 * 1024 * 1024
ADA_TN = 1536
ROW_TILE = 512
QKV_CHUNK = 512
GQA_TQ = 256
NA_ROWS_PER_STEP = 2
MOE_TB = 256
DISPATCH_TM = 256
COMBINE_TM = 256
ROUTE_T = 512
DMA_UNROLL = 8
NEG_BIG = -1e30
LOG2_E = 1.4426950408889634
Q_SCALE = (HEAD_DIM ** -0.5) * LOG2_E

F32 = jnp.float32
BF16 = jnp.bfloat16


def _cparams(sem):
    return pltpu.CompilerParams(dimension_semantics=sem, vmem_limit_bytes=VMEM_LIMIT_BYTES)


def _split_bf16(a):
    hi = a.astype(BF16)
    lo = (a - hi.astype(F32)).astype(BF16)
    return hi, lo


def _ada_kernel(c_ref, w_ref, b_ref, o_ref):
    c = c_ref[...]
    ca = (c * jax.nn.sigmoid(c)).astype(BF16)
    o_ref[0] = jnp.dot(ca, w_ref[0].astype(BF16), preferred_element_type=F32) + b_ref[0]


def _ada_modulation(c, ada_w, ada_b):
    depth, d, n = ada_w.shape
    b = c.shape[0]
    return pl.pallas_call(
        _ada_kernel,
        grid=(depth, n // ADA_TN),
        in_specs=[
            pl.BlockSpec((b, d), lambda i, j: (0, 0)),
            pl.BlockSpec((1, d, ADA_TN), lambda i, j: (i, 0, j)),
            pl.BlockSpec((1, 1, ADA_TN), lambda i, j: (i, 0, j)),
        ],
        out_specs=pl.BlockSpec((1, b, ADA_TN), lambda i, j: (i, 0, j)),
        out_shape=jax.ShapeDtypeStruct((depth, b, n), F32),
        compiler_params=_cparams(("arbitrary", "arbitrary")),
        name="ada_modulation",
    )(c, ada_w, ada_b.reshape(depth, 1, n))


def _qkv_na_kernel(x_ref, sc_ref, sh_ref, w_ref, o_ref, *, d_model):
    h = (x_ref[0] * (1.0 + sc_ref[0]) + sh_ref[0]).astype(BF16)
    n = w_ref.shape[1]
    for n0 in range(0, n, QKV_CHUNK):
        acc = jnp.dot(h, w_ref[:, n0:n0 + QKV_CHUNK], preferred_element_type=F32)
        if n0 < d_model:
            acc = acc * Q_SCALE
        o_ref[0, :, n0:n0 + QKV_CHUNK] = acc.astype(BF16)


def _qkv_na(x, sc, sh, w_bf16):
    b, s, d = x.shape
    n = w_bf16.shape[1]
    return pl.pallas_call(
        functools.partial(_qkv_na_kernel, d_model=d),
        grid=(b, s // ROW_TILE),
        in_specs=[
            pl.BlockSpec((1, ROW_TILE, d), lambda i, j: (i, j, 0)),
            pl.BlockSpec((1, 1, d), lambda i, j: (i, 0, 0)),
            pl.BlockSpec((1, 1, d), lambda i, j: (i, 0, 0)),
            pl.BlockSpec((d, n), lambda i, j: (0, 0)),
        ],
        out_specs=pl.BlockSpec((1, ROW_TILE, n), lambda i, j: (i, j, 0)),
        out_shape=jax.ShapeDtypeStruct((b, s, n), BF16),
        compiler_params=_cparams(("arbitrary", "arbitrary")),
        name="qkv_na",
    )(x, sc, sh, w_bf16)


N_DR = 2 * NA_ROWS - 1
N_DC = 2 * NA_COLS - 1


def _na_bias_kernel(r_ref, m_ref, valid_ref, o_ref):
    r = r_ref[...]
    hi = r.astype(BF16)
    r1 = r - hi.astype(F32)
    mid = r1.astype(BF16)
    lo = (r1 - mid.astype(F32)).astype(BF16)
    m = m_ref[...]
    acc = (jnp.dot(hi, m, preferred_element_type=F32) + jnp.dot(mid, m, preferred_element_type=F32)
           + jnp.dot(lo, m, preferred_element_type=F32))
    o_ref[...] = jnp.where(valid_ref[...] > 0.0, acc * LOG2_E, NEG_BIG)


def _na_bias_table(rpb):
    qc = jnp.arange(GRID_W, dtype=jnp.int32)
    kc = jnp.arange(GRID_W, dtype=jnp.int32)
    win_start = jnp.clip(qc - NA_COLS // 2, 0, GRID_W - NA_COLS)
    valid = (kc[None, :] >= win_start[:, None]) & (kc[None, :] < win_start[:, None] + NA_COLS)
    dc = jnp.clip(kc[None, :] - qc[:, None] + NA_COLS - 1, 0, N_DC - 1)
    onehot = (jnp.arange(LANES, dtype=jnp.int32)[:, None] == dc.reshape(1, GRID_W * GRID_W)).astype(BF16)
    r2d = jnp.pad(rpb.reshape(N_HEADS * N_DR, N_DC).astype(F32), ((0, 0), (0, LANES - N_DC)))
    tiles = pl.pallas_call(
        _na_bias_kernel,
        out_shape=jax.ShapeDtypeStruct((N_HEADS * N_DR, GRID_W * GRID_W), F32),
        compiler_params=pltpu.CompilerParams(vmem_limit_bytes=VMEM_LIMIT_BYTES),
        name="na_bias_tiles",
    )(r2d, onehot, valid.reshape(1, GRID_W * GRID_W).astype(F32))
    tiles = tiles.reshape(N_HEADS, N_DR, GRID_W, GRID_W)
    pairs = jnp.concatenate([tiles[:, :N_DR - 1], tiles[:, 1:]], axis=-1)
    pairs = pairs.reshape(N_HEADS // 2, 2, N_DR - 1, GRID_W, HEAD_PAIR).transpose(0, 2, 1, 3, 4)
    return pairs.reshape(N_HEADS // 2, N_DR - 1, 2 * GRID_W, HEAD_PAIR)


def _na_kernel(q_ref, k_ref, v_ref, bias_ref, o_ref, *, rows):
    nk = NA_ROWS * GRID_W
    lane = lax.broadcasted_iota(jnp.int32, (GRID_W, HEAD_PAIR), 1)
    first = lane < HEAD_DIM
    for sub_row in range(NA_ROWS_PER_STEP):
        r = pl.program_id(1) * NA_ROWS_PER_STEP + sub_row
        row_start = jnp.clip(r - NA_ROWS // 2, 0, rows - NA_ROWS)
        start = pl.multiple_of(row_start * GRID_W, GRID_W)
        dr_base = row_start - r + (NA_ROWS - 1)
        q_rows = slice(sub_row * GRID_W, (sub_row + 1) * GRID_W)
        for hp in range(N_HEADS // 2):
            cols = slice(hp * HEAD_PAIR, (hp + 1) * HEAD_PAIR)
            q2 = q_ref[0, q_rows, cols]
            k2 = k_ref[0, pl.ds(start, nk), cols]
            v2 = v_ref[0, pl.ds(start, nk), cols]
            zero = jnp.zeros_like(q2)
            qs = jnp.concatenate([jnp.where(first, q2, zero), jnp.where(first, zero, q2)], axis=0)
            s = lax.dot_general(qs, k2, (((1,), (1,)), ((), ())), preferred_element_type=F32)
            s = s + jnp.concatenate([bias_ref[hp, dr_base + 2 * m] for m in range(NA_ROWS // 2)], axis=1)
            m = jnp.max(s, axis=-1, keepdims=True)
            p = jnp.exp2(s - m)
            l = jnp.sum(p, axis=-1, keepdims=True)
            o = jnp.dot(p.astype(BF16), v2, preferred_element_type=F32) / l
            o_ref[0, q_rows, cols] = jnp.where(first, o[:GRID_W], o[GRID_W:]).astype(BF16)


def _na_attention(qkv, bias, rows):
    b, s, n3 = qkv.shape
    d = n3 // 3
    assert rows >= NA_ROWS and NA_ROWS % 2 == 0 and rows % NA_ROWS_PER_STEP == 0
    q_tile = NA_ROWS_PER_STEP * GRID_W
    return pl.pallas_call(
        functools.partial(_na_kernel, rows=rows),
        grid=(b, rows // NA_ROWS_PER_STEP),
        in_specs=[
            pl.BlockSpec((1, q_tile, d), lambda i, r: (i, r, 0)),
            pl.BlockSpec((1, s, d), lambda i, r: (i, 0, 1)),
            pl.BlockSpec((1, s, d), lambda i, r: (i, 0, 2)),
            pl.BlockSpec(bias.shape, lambda i, r: (0, 0, 0, 0)),
        ],
        out_specs=pl.BlockSpec((1, q_tile, d), lambda i, r: (i, r, 0)),
        out_shape=jax.ShapeDtypeStruct((b, s, d), BF16),
        compiler_params=_cparams(("arbitrary", "arbitrary")),
        name="na_attention",
    )(qkv, qkv, qkv, bias)


def _rms_rope(a, gain, cosf, sinf, seg, seg_t):
    qq_hi, qq_lo = _split_bf16(a * a)
    ms = jnp.dot(qq_hi, seg, preferred_element_type=F32) + jnp.dot(qq_lo, seg, preferred_element_type=F32)
    r_hi, r_lo = _split_bf16(lax.rsqrt(ms + RMS_EPS))
    rf = jnp.dot(r_hi, seg_t, preferred_element_type=F32) + jnp.dot(r_lo, seg_t, preferred_element_type=F32)
    an = a * rf * gain
    n = a.shape[1]
    lane = lax.broadcasted_iota(jnp.int32, a.shape, 1)
    partner = jnp.where(lane % (2 * ROPE_PAIRS) < ROPE_PAIRS,
                        pltpu.roll(an, n - ROPE_PAIRS, 1), pltpu.roll(an, ROPE_PAIRS, 1))
    return an * cosf + partner * sinf


def _qkv_gqa_kernel(x_ref, sc_ref, sh_ref, w_ref, qg_ref, kg_ref, cos_ref, sin_ref, seg_ref, segt_ref, o_ref,
                    *, d_model):
    h = (x_ref[0] * (1.0 + sc_ref[0]) + sh_ref[0]).astype(BF16)
    n = w_ref.shape[1]
    kv_cols = (n - d_model) // 2
    cosf, sinf = cos_ref[...], sin_ref[...]
    seg, seg_t = seg_ref[...], segt_ref[...]
    for n0 in range(0, n, QKV_CHUNK):
        acc = jnp.dot(h, w_ref[:, n0:n0 + QKV_CHUNK], preferred_element_type=F32)
        if n0 < d_model:
            acc = _rms_rope(acc, qg_ref[...], cosf, sinf, seg, seg_t) * Q_SCALE
        elif n0 < d_model + kv_cols:
            acc = _rms_rope(acc, kg_ref[...], cosf, sinf, seg, seg_t)
        else:
            lane = lax.broadcasted_iota(jnp.int32, acc.shape, 1)
            acc = jnp.where(lane % HEAD_PAIR < HEAD_DIM, acc, 1.0)
        o_ref[0, :, n0:n0 + QKV_CHUNK] = acc.astype(BF16)


def _qkv_gqa(x, sc, sh, w_bf16, q_gain, k_gain, cosf, sinf, seg, seg_t):
    b, s, d = x.shape
    n = w_bf16.shape[1]
    const = lambda j, i: (0, 0)
    return pl.pallas_call(
        functools.partial(_qkv_gqa_kernel, d_model=d),
        grid=(s // ROW_TILE, b),
        in_specs=[
            pl.BlockSpec((1, ROW_TILE, d), lambda j, i: (i, j, 0)),
            pl.BlockSpec((1, 1, d), lambda j, i: (i, 0, 0)),
            pl.BlockSpec((1, 1, d), lambda j, i: (i, 0, 0)),
            pl.BlockSpec((d, n), const),
            pl.BlockSpec((1, QKV_CHUNK), const),
            pl.BlockSpec((1, QKV_CHUNK), const),
            pl.BlockSpec((ROW_TILE, QKV_CHUNK), lambda j, i: (j, 0)),
            pl.BlockSpec((ROW_TILE, QKV_CHUNK), lambda j, i: (j, 0)),
            pl.BlockSpec((QKV_CHUNK, LANES), const),
            pl.BlockSpec((LANES, QKV_CHUNK), const),
        ],
        out_specs=pl.BlockSpec((1, ROW_TILE, n), lambda j, i: (i, j, 0)),
        out_shape=jax.ShapeDtypeStruct((b, s, n), BF16),
        compiler_params=_cparams(("arbitrary", "arbitrary")),
        name="qkv_gqa",
    )(x, sc, sh, w_bf16, q_gain, k_gain, cosf, sinf, seg, seg_t)


def _gqa_kernel(q_ref, k_ref, v_ref, o_ref):
    k2 = k_ref[0]
    v2 = v_ref[0]
    tq = q_ref.shape[1]
    lane = lax.broadcasted_iota(jnp.int32, (tq, HEAD_PAIR), 1)
    first = lane < HEAD_DIM
    for pair in range(q_ref.shape[2] // HEAD_PAIR):
        cols = slice(pair * HEAD_PAIR, (pair + 1) * HEAD_PAIR)
        q2 = q_ref[0, :, cols]
        zero = jnp.zeros_like(q2)
        qs = jnp.concatenate([jnp.where(first, q2, zero), jnp.where(first, zero, q2)], axis=0)
        s = lax.dot_general(qs, k2, (((1,), (1,)), ((), ())), preferred_element_type=F32)
        m = jnp.max(s, axis=-1, keepdims=True)
        p = jnp.exp2(s - m).astype(BF16)
        o = jnp.dot(p, v2, preferred_element_type=F32)
        o_a, o_b = o[:tq], o[tq:]
        out = jnp.where(first, o_a / pltpu.roll(o_a, HEAD_DIM, 1), pltpu.roll(o_b, HEAD_DIM, 1) / o_b)
        o_ref[0, :, cols] = out.astype(BF16)


def _gqa_attention(qkv, d_model):
    b, s, n = qkv.shape
    group_cols = d_model // GQA_KV_HEADS
    k_block0 = d_model // HEAD_PAIR
    v_block0 = k_block0 + GQA_KV_HEADS
    return pl.pallas_call(
        _gqa_kernel,
        grid=(b, GQA_KV_HEADS, s // GQA_TQ),
        in_specs=[
            pl.BlockSpec((1, GQA_TQ, group_cols), lambda i, g, t: (i, t, g)),
            pl.BlockSpec((1, s, HEAD_PAIR), lambda i, g, t: (i, 0, k_block0 + g)),
            pl.BlockSpec((1, s, HEAD_PAIR), lambda i, g, t: (i, 0, v_block0 + g)),
        ],
        out_specs=pl.BlockSpec((1, GQA_TQ, group_cols), lambda i, g, t: (i, t, g)),
        out_shape=jax.ShapeDtypeStruct((b, s, d_model), BF16),
        compiler_params=_cparams(("arbitrary", "arbitrary", "arbitrary")),
        name="gqa_attention",
    )(qkv, qkv, qkv)


def _store_token_tiles(ref, val):
    t, d = val.shape
    sub = d // LANES
    for s in range(sub):
        ref[pl.ds(s, t, stride=sub), :] = val[:, s * LANES:(s + 1) * LANES]


def _load_token_tiles(ref, t, dtype):
    sub = ref.shape[0] // t
    return jnp.concatenate([ref[pl.ds(s, t, stride=sub), :].astype(dtype) for s in range(sub)], axis=1)


def _tile_copy(src_hbm, dst_vmem, src_tok, dst_tok, sub, sem):
    return pltpu.make_async_copy(src_hbm.at[pl.ds(pl.multiple_of(src_tok * sub, sub), sub)],
                                 dst_vmem.at[pl.ds(pl.multiple_of(dst_tok * sub, sub), sub)], sem)


def _layer_norm_rows(z, g, b):
    mu = jnp.mean(z, axis=-1, keepdims=True)
    zc = z - mu
    var = jnp.mean(zc * zc, axis=-1, keepdims=True)
    return zc * lax.rsqrt(var + LN_EPS) * g + b


def _router(lg):
    lane = lax.broadcasted_iota(jnp.int32, lg.shape, 1)
    lanef = lane.astype(F32)
    far = float(LANES)
    gmask = lane < N_GROUPS
    lgm = jnp.where(gmask, lg, NEG_BIG)
    mg = jnp.max(lgm, axis=-1, keepdims=True)
    sg = jnp.sum(jnp.exp(lgm - mg), axis=-1, keepdims=True)
    g_prob = 1.0 / sg
    g_idx = jnp.min(jnp.where(lgm == mg, lanef, far), axis=-1, keepdims=True)
    e_lo = N_GROUPS + g_idx * EXPERTS_PER_GROUP
    emask = (lanef >= e_lo) & (lanef < e_lo + EXPERTS_PER_GROUP)
    lem = jnp.where(emask, lg, NEG_BIG)
    me = jnp.max(lem, axis=-1, keepdims=True)
    ee = jnp.exp(lem - me)
    pe = jnp.where(emask, ee / jnp.sum(ee, axis=-1, keepdims=True), -1.0)
    p1 = jnp.max(pe, axis=-1, keepdims=True)
    i1 = jnp.min(jnp.where(pe == p1, lanef, far), axis=-1, keepdims=True)
    pe2 = jnp.where(lanef == i1, -1.0, pe)
    p2 = jnp.max(pe2, axis=-1, keepdims=True)
    i2 = jnp.min(jnp.where(pe2 == p2, lanef, far), axis=-1, keepdims=True)
    den = p1 + p2
    gate1 = g_prob * (p1 / den)
    gate2 = g_prob * (p2 / den)
    out = jnp.where(lane == 0, gate1, 0.0)
    out = jnp.where(lane == 1, gate2, out)
    out = jnp.where(lane == 2, i1 - N_GROUPS, out)
    out = jnp.where(lane == 3, i2 - N_GROUPS, out)
    return out


def _post_attn_kernel(a_ref, wo_ref, x_ref, g1_ref, sc2_ref, sh2_ref, lng_ref, lnb_ref, wr_ref, br_ref,
                      xo_ref, h2_ref, route_ref, *, alpha):
    y = jnp.dot(a_ref[0], wo_ref[...], preferred_element_type=F32)
    z = alpha * x_ref[0] + (1.0 + g1_ref[0]) * y
    xn = _layer_norm_rows(z, lng_ref[...], lnb_ref[...])
    xo_ref[0] = xn
    h2 = xn * (1.0 + sc2_ref[0]) + sh2_ref[0]
    _store_token_tiles(h2_ref, h2)
    h_hi, h_lo = _split_bf16(h2)
    t = jnp.dot(h_hi, wr_ref[...], preferred_element_type=F32)
    lg = (t[:, :LANES] + t[:, LANES:] + jnp.dot(h_lo, wr_ref[:, :LANES], preferred_element_type=F32)
          + br_ref[...])
    route_ref[0] = _router(lg)


def _post_attn(a, wo_bf16, x, g1, sc2, sh2, ln_g, ln_b, wr, br, alpha):
    b, s, d = x.shape
    sub = d // LANES
    tiles_per_batch = s // ROW_TILE
    const = lambda i, j: (0, 0)
    tile = lambda i, j: (i, j, 0)
    per_b = lambda i, j: (i, 0, 0)
    return pl.pallas_call(
        functools.partial(_post_attn_kernel, alpha=alpha),
        grid=(b, s // ROW_TILE),
        in_specs=[
            pl.BlockSpec((1, ROW_TILE, d), tile),
            pl.BlockSpec((d, d), const),
            pl.BlockSpec((1, ROW_TILE, d), tile),
            pl.BlockSpec((1, 1, d), per_b),
            pl.BlockSpec((1, 1, d), per_b),
            pl.BlockSpec((1, 1, d), per_b),
            pl.BlockSpec((1, d), const),
            pl.BlockSpec((1, d), const),
            pl.BlockSpec((d, 2 * LANES), const),
            pl.BlockSpec((1, LANES), const),
        ],
        out_specs=[
            pl.BlockSpec((1, ROW_TILE, d), tile),
            pl.BlockSpec((ROW_TILE * sub, LANES), lambda i, j: (i * tiles_per_batch + j, 0)),
            pl.BlockSpec((1, ROW_TILE, LANES), tile),
        ],
        out_shape=[
            jax.ShapeDtypeStruct((b, s, d), F32),
            jax.ShapeDtypeStruct((b * s * sub, LANES), F32),
            jax.ShapeDtypeStruct((b, s, LANES), F32),
        ],
        compiler_params=_cparams(("arbitrary", "arbitrary")),
        name="post_attn",
    )(a, wo_bf16, x, g1, sc2, sh2, ln_g, ln_b, wr, br)


def _dispatch_kernel(pos_ref, h_ref, xs_hbm, sem):
    tm = DISPATCH_TM
    sub = h_ref.shape[0] // tm

    def issue(jb, carry):
        for u in range(DMA_UNROLL):
            j = jb * DMA_UNROLL + u
            src = h_ref.at[pl.ds(pl.multiple_of(j * sub, sub), sub)]
            for k in range(TOP_K):
                row = pos_ref[0, 0, TOP_K * j + k]
                dst = xs_hbm.at[pl.ds(pl.multiple_of(row * sub, sub), sub)]
                pltpu.make_async_copy(src, dst, sem).start(priority=k % 2)
        return carry

    lax.fori_loop(0, tm // DMA_UNROLL, issue, 0)
    for _ in range(TOP_K):
        pltpu.make_async_copy(h_ref, xs_hbm.at[pl.ds(0, tm * sub)], sem).wait()


def _dispatch(pos, h2_tiles, n_tok):
    sub = h2_tiles.shape[0] // n_tok
    tm = DISPATCH_TM
    n_tiles = n_tok // tm
    return pl.pallas_call(
        _dispatch_kernel,
        grid=(n_tiles,),
        in_specs=[
            pl.BlockSpec((1, 1, TOP_K * tm), lambda i: (i, 0, 0), memory_space=pltpu.SMEM),
            pl.BlockSpec((tm * sub, LANES), lambda i: (i, 0)),
        ],
        out_specs=pl.BlockSpec(memory_space=pl.ANY),
        out_shape=jax.ShapeDtypeStruct((n_tok * TOP_K * sub, LANES), F32),
        scratch_shapes=[pltpu.SemaphoreType.DMA],
        compiler_params=_cparams(("arbitrary",)),
        name="moe_dispatch",
    )(pos.reshape(n_tiles, 1, TOP_K * tm), h2_tiles)


def _expert_kernel(blk_ref, exp_ref, lo_ref, hi_ref, x_ref, wg_ref, wu_ref, wd_ref, y_ref,
                   acc, wgb, wub, wdb, cast_expert):
    i = pl.program_id(0)
    n_items = pl.num_programs(0)
    tb = MOE_TB
    prev = jnp.maximum(i - 1, 0)
    nxt = jnp.minimum(i + 1, n_items - 1)
    first = (i == 0) | (blk_ref[i] != blk_ref[prev])
    last = (i == n_items - 1) | (blk_ref[i] != blk_ref[nxt])
    lo, hi = lo_ref[i], hi_ref[i]

    @pl.when(i == 0)
    def _():
        cast_expert[0] = -1

    @pl.when(first)
    def _():
        acc[...] = jnp.zeros_like(acc)

    @pl.when(hi > lo)
    def _():
        @pl.when(cast_expert[0] != exp_ref[i])
        def _():
            wgb[...] = wg_ref[0, 0].astype(BF16)
            wub[...] = wu_ref[0, 0].astype(BF16)
            wdb[...] = wd_ref[0, 0].astype(BF16)
            cast_expert[0] = exp_ref[i]

        xb = _load_token_tiles(x_ref, tb, BF16)
        g = jnp.dot(xb, wgb[...], preferred_element_type=F32)
        u = jnp.dot(xb, wub[...], preferred_element_type=F32)
        act = (g * jax.nn.sigmoid(g) * u).astype(BF16)
        y = jnp.dot(act, wdb[...], preferred_element_type=F32)
        row = lax.broadcasted_iota(jnp.int32, (tb, 1), 0)
        acc[...] += jnp.where((row >= lo) & (row < hi), y, 0.0)

    @pl.when(last)
    def _():
        _store_token_tiles(y_ref, acc[...])


def _expert_mlps(items, xs_tiles, w_gate, w_up, w_down, layer):
    item_blk, item_exp, item_lo, item_hi = items
    n_items = item_blk.shape[0]
    _, _, d, de = w_gate.shape
    sub = d // LANES
    w_index = lambda i, blk, ex, lo, hi: (layer, ex[i], 0, 0)
    grid_spec = pltpu.PrefetchScalarGridSpec(
        num_scalar_prefetch=4,
        grid=(n_items,),
        in_specs=[
            pl.BlockSpec((MOE_TB * sub, LANES), lambda i, blk, ex, lo, hi: (blk[i], 0)),
            pl.BlockSpec((1, 1, d, de), w_index),
            pl.BlockSpec((1, 1, d, de), w_index),
            pl.BlockSpec((1, 1, de, d), w_index),
        ],
        out_specs=pl.BlockSpec((MOE_TB * sub, LANES), lambda i, blk, ex, lo, hi: (blk[i], 0)),
        scratch_shapes=[
            pltpu.VMEM((MOE_TB, d), F32),
            pltpu.VMEM((d, de), BF16),
            pltpu.VMEM((d, de), BF16),
            pltpu.VMEM((de, d), BF16),
            pltpu.SMEM((1,), jnp.int32),
        ],
    )
    return pl.pallas_call(
        _expert_kernel,
        grid_spec=grid_spec,
        out_shape=jax.ShapeDtypeStruct(xs_tiles.shape, F32),
        compiler_params=_cparams(("arbitrary",)),
        name="expert_mlps",
    )(item_blk, item_exp, item_lo, item_hi, xs_tiles, w_gate, w_up, w_down)


def _combine_kernel(pos_ref, ys_hbm, x_ref, route_ref, g2_ref, lng_ref, lnb_ref, xo_ref, buf0, buf1, sem,
                    *, alpha):
    tm = COMBINE_TM
    sub = buf0.shape[0] // tm

    def issue(jb, carry):
        for u in range(DMA_UNROLL):
            j = jb * DMA_UNROLL + u
            _tile_copy(ys_hbm, buf0, pos_ref[0, 0, TOP_K * j], j, sub, sem).start(priority=0)
            _tile_copy(ys_hbm, buf1, pos_ref[0, 0, TOP_K * j + 1], j, sub, sem).start(priority=1)
        return carry

    lax.fori_loop(0, tm // DMA_UNROLL, issue, 0)
    route = route_ref[...]
    lane = lax.broadcasted_iota(jnp.int32, route.shape, 1)
    gate0 = jnp.sum(jnp.where(lane == 0, route, 0.0), axis=-1, keepdims=True)
    gate1 = jnp.sum(jnp.where(lane == 1, route, 0.0), axis=-1, keepdims=True)
    pltpu.make_async_copy(ys_hbm.at[pl.ds(0, tm * sub)], buf0, sem).wait()
    pltpu.make_async_copy(ys_hbm.at[pl.ds(0, tm * sub)], buf1, sem).wait()
    y = _load_token_tiles(buf0, tm, F32) * gate0 + _load_token_tiles(buf1, tm, F32) * gate1
    z = alpha * x_ref[...] + (1.0 + g2_ref[0]) * y
    xo_ref[...] = _layer_norm_rows(z, lng_ref[...], lnb_ref[...])


def _combine(pos, ys, x2d, route2d, g2, ln_g, ln_b, seq, alpha):
    n_tok, d = x2d.shape
    tm = COMBINE_TM
    n_tiles = n_tok // tm
    tiles_per_batch = seq // tm
    const = lambda i: (0, 0)
    return pl.pallas_call(
        functools.partial(_combine_kernel, alpha=alpha),
        grid=(n_tiles,),
        in_specs=[
            pl.BlockSpec((1, 1, TOP_K * tm), lambda i: (i, 0, 0), memory_space=pltpu.SMEM),
            pl.BlockSpec(memory_space=pl.ANY),
            pl.BlockSpec((tm, d), lambda i: (i, 0)),
            pl.BlockSpec((tm, LANES), lambda i: (i, 0)),
            pl.BlockSpec((1, 1, d), lambda i: (i // tiles_per_batch, 0, 0)),
            pl.BlockSpec((1, d), const),
            pl.BlockSpec((1, d), const),
        ],
        out_specs=pl.BlockSpec((tm, d), lambda i: (i, 0)),
        out_shape=jax.ShapeDtypeStruct((n_tok, d), F32),
        scratch_shapes=[
            pltpu.VMEM((tm * (d // LANES), LANES), F32),
            pltpu.VMEM((tm * (d // LANES), LANES), F32),
            pltpu.SemaphoreType.DMA,
        ],
        compiler_params=_cparams(("arbitrary",)),
        name="moe_combine",
    )(pos.reshape(n_tiles, 1, TOP_K * tm), ys, x2d, route2d, g2, ln_g, ln_b)


def _route_pos_kernel(route_ref, pos_ref, cnt_ref, counts, base):
    p = pl.program_id(0)
    j = pl.program_id(1)
    route = route_ref[...]
    t = route.shape[0]
    lane = lax.broadcasted_iota(jnp.int32, route.shape, 1)
    lanef = lane.astype(F32)
    e1 = jnp.sum(jnp.where(lane == 2, route, 0.0), axis=-1, keepdims=True)
    e2 = jnp.sum(jnp.where(lane == 3, route, 0.0), axis=-1, keepdims=True)
    m1 = lanef == e1
    m2 = lanef == e2
    mem = jnp.where(m1 | m2, 1.0, 0.0)
    colsum = jnp.sum(mem, axis=0, keepdims=True)

    @pl.when((p == 0) & (j == 0))
    def _():
        counts[...] = jnp.zeros_like(counts)

    @pl.when(p == 0)
    def _():
        counts[...] = counts[...] + colsum

    @pl.when((p == 1) & (j == 0))
    def _():
        c = counts[...]
        cnt_ref[...] = c
        c_hi = jnp.floor(c * (1.0 / 256.0))
        c_lo = c - c_hi * 256.0
        r_i = lax.broadcasted_iota(jnp.int32, (LANES, LANES), 0)
        c_i = lax.broadcasted_iota(jnp.int32, (LANES, LANES), 1)
        excl = jnp.where(r_i < c_i, 1.0, 0.0).astype(BF16)
        base[...] = (jnp.dot(c_hi.astype(BF16), excl, preferred_element_type=F32) * 256.0
                     + jnp.dot(c_lo.astype(BF16), excl, preferred_element_type=F32))

    @pl.when(p == 1)
    def _():
        r_i = lax.broadcasted_iota(jnp.int32, (t, t), 0)
        c_i = lax.broadcasted_iota(jnp.int32, (t, t), 1)
        before = jnp.where(c_i < r_i, 1.0, 0.0).astype(BF16)
        rank = base[0:1, :] + jnp.dot(before, mem.astype(BF16), preferred_element_type=F32)
        pos1 = jnp.sum(jnp.where(m1, rank, 0.0), axis=-1, keepdims=True)
        pos2 = jnp.sum(jnp.where(m2, rank, 0.0), axis=-1, keepdims=True)
        pos_ref[...] = jnp.where(lane == 0, pos1, jnp.where(lane == 1, pos2, 0.0))
        base[...] = base[...] + colsum


def _routing_tables(route2d):
    n_tok = route2d.shape[0]
    n_tiles = n_tok // ROUTE_T
    posf, cnt = pl.pallas_call(
        _route_pos_kernel,
        grid=(2, n_tiles),
        in_specs=[pl.BlockSpec((ROUTE_T, LANES), lambda p, j: (j, 0))],
        out_specs=[
            pl.BlockSpec((ROUTE_T, LANES), lambda p, j: (p * j, 0)),
            pl.BlockSpec((8, LANES), lambda p, j: (0, 0)),
        ],
        out_shape=[
            jax.ShapeDtypeStruct((n_tok, LANES), F32),
            jax.ShapeDtypeStruct((8, LANES), F32),
        ],
        scratch_shapes=[pltpu.VMEM((8, LANES), F32), pltpu.VMEM((8, LANES), F32)],
        compiler_params=_cparams(("arbitrary", "arbitrary")),
        name="route_positions",
    )(route2d)
    n_rows = n_tok * TOP_K
    pos = posf[:, :TOP_K].astype(jnp.int32).reshape(n_rows)
    counts = cnt[0, :N_EXPERTS].astype(jnp.int32)
    seg_end = jnp.cumsum(counts)
    cuts = jnp.concatenate([jnp.arange(n_rows // MOE_TB, dtype=jnp.int32) * MOE_TB, seg_end - counts])
    n_items = cuts.shape[0]
    idx = jnp.arange(n_items, dtype=jnp.int32)
    rank = (jnp.sum(cuts[None, :] < cuts[:, None], axis=1)
            + jnp.sum((cuts[None, :] == cuts[:, None]) & (idx[None, :] < idx[:, None]), axis=1))
    lo_abs = jnp.sum(jnp.where(rank[None, :] == idx[:, None], cuts[None, :], 0), axis=1)
    hi_abs = jnp.concatenate([lo_abs[1:], jnp.full((1,), n_rows, jnp.int32)])
    item_blk = jnp.minimum(lo_abs // MOE_TB, n_rows // MOE_TB - 1)
    item_exp = jnp.minimum(jnp.sum(seg_end[None, :] <= lo_abs[:, None], axis=1), N_EXPERTS - 1)
    item_lo = lo_abs - item_blk * MOE_TB
    item_hi = hi_abs - item_blk * MOE_TB
    items = tuple(a.astype(jnp.int32) for a in (item_blk, item_exp, item_lo, item_hi))
    return pos, items


def _rope_tables(seq, width):
    t = jnp.arange(seq, dtype=jnp.int32)
    pos = jnp.stack([t // GRID_W, t % GRID_W], axis=-1).astype(F32)
    inv_freq = ROPE_THETA ** (-jnp.arange(ROPE_PAIRS, dtype=F32) / ROPE_PAIRS)
    ang = pos[:, :, None] * inv_freq
    cos, sin = jnp.cos(ang), jnp.sin(ang)
    cosf = jnp.stack([cos, cos], axis=2).reshape(seq, HEAD_DIM)
    sinf = jnp.stack([-sin, sin], axis=2).reshape(seq, HEAD_DIM)
    reps = width // HEAD_DIM
    return jnp.tile(cosf, (1, reps)), jnp.tile(sinf, (1, reps))


def _dup_heads(w):
    d = w.shape[0]
    w4 = w.reshape(d, GQA_KV_HEADS, 1, HEAD_DIM)
    return jnp.concatenate([w4, w4], axis=2).reshape(d, GQA_KV_HEADS * HEAD_PAIR)


def kernel(x, c, ada_w, ada_b, ln_g, ln_b, na_w_qkv, na_rpb, na_w_o, gqa_w_qkv, gqa_q_norm, gqa_k_norm, gqa_w_o,
           moe_w_group, moe_b_group, moe_w_expert, moe_b_expert, moe_w_gate, moe_w_up, moe_w_down):
    b, s, d = x.shape
    depth = ada_w.shape[0]
    n_tok = b * s
    rows = s // GRID_W
    kvd = GQA_KV_HEADS * HEAD_DIM
    alpha = (2 * depth) ** 0.25

    mod = _ada_modulation(c, ada_w, ada_b).reshape(depth, b, 6, 1, d)

    cosf, sinf = _rope_tables(s, QKV_CHUNK)
    head_of_lane = jnp.arange(QKV_CHUNK, dtype=jnp.int32) // HEAD_DIM
    seg_t = (head_of_lane[None, :] == jnp.arange(LANES, dtype=jnp.int32)[:, None])
    seg = (seg_t.T.astype(F32) / HEAD_DIM).astype(BF16)
    seg_t = seg_t.astype(BF16)

    for i in range(depth):
        sh1, sc1, g1, sh2, sc2, g2 = (mod[i, :, k] for k in range(6))
        j = i // 2
        if i % 2 == 0:
            qkv = _qkv_na(x, sc1, sh1, na_w_qkv[j].astype(BF16))
            attn = _na_attention(qkv, _na_bias_table(na_rpb[j]), rows)
            w_o = na_w_o[j]
        else:
            w = gqa_w_qkv[j]
            w_all = jnp.concatenate([w[:, :d], _dup_heads(w[:, d:d + kvd]), _dup_heads(w[:, d + kvd:])], axis=1)
            reps = QKV_CHUNK // HEAD_DIM
            qkv = _qkv_gqa(x, sc1, sh1, w_all.astype(BF16),
                           jnp.tile(gqa_q_norm[j], reps)[None, :], jnp.tile(gqa_k_norm[j], reps)[None, :],
                           cosf, sinf, seg, seg_t)
            attn = _gqa_attention(qkv, d)
            w_o = gqa_w_o[j]

        w_r = jnp.concatenate([moe_w_group[i], moe_w_expert[i]], axis=1)
        w_r = jnp.pad(w_r, ((0, 0), (0, LANES - w_r.shape[1])))
        wr_hi, wr_lo = _split_bf16(w_r)
        b_r = jnp.concatenate([moe_b_group[i], moe_b_expert[i]])
        b_r = jnp.pad(b_r, (0, LANES - b_r.shape[0]))[None, :]
        x, h2, route = _post_attn(attn, w_o.astype(BF16), x, g1, sc2, sh2, ln_g[i, 0][None, :], ln_b[i, 0][None, :],
                                  jnp.concatenate([wr_hi, wr_lo], axis=1), b_r, alpha)

        route2d = route.reshape(n_tok, LANES)
        pos, items = _routing_tables(route2d)
        xs = _dispatch(pos, h2, n_tok)
        ys = _expert_mlps(items, xs, moe_w_gate, moe_w_up, moe_w_down, i)
        x = _combine(pos, ys, x.reshape(n_tok, d), route2d, g2, ln_g[i, 1][None, :], ln_b[i, 1][None, :],
                     s, alpha).reshape(b, s, d)
    return x
```

```python
import functools

import jax
import jax.numpy as jnp
from jax import lax
from jax.experimental import pallas as pl
from jax.experimental.pallas import tpu as pltpu

GRID_W = 64
N_HEADS = 16
HEAD_DIM = 64
NA_ROWS = 8
NA_COLS = 16
GQA_KV_HEADS = 4
ROPE_THETA = 10000.0
ROPE_PAIRS = HEAD_DIM // 4
N_GROUPS = 4
EXPERTS_PER_GROUP = 8
N_EXPERTS = N_GROUPS * EXPERTS_PER_GROUP
TOP_K = 2
LN_EPS = 1e-5
RMS_EPS = 1e-6

LANES = 128
HEAD_PAIR = 2 * HEAD_DIM
VMEM_LIMIT_BYTES = 52 * 1024 * 1024
ADA_TN = 1536
ROW_TILE = 512
QKV_CHUNK = 512
GQA_TQ = 256
NA_ROWS_PER_STEP = 2
NA_STAGE_PAIRS = 4
MOE_TB = 256
DISPATCH_TM = 256
COMBINE_TM = 256
ROUTE_T = 512
DMA_UNROLL = 8
NEG_BIG = -1e30
LOG2_E = 1.4426950408889634
Q_SCALE = (HEAD_DIM ** -0.5) * LOG2_E
GQA_BOUND_MARGIN = 1.02
GQA_BOUND_MAX = 50.0

F32 = jnp.float32
BF16 = jnp.bfloat16


def _cparams(sem):
    return pltpu.CompilerParams(dimension_semantics=sem, vmem_limit_bytes=VMEM_LIMIT_BYTES)


def _split_bf16(a):
    hi = a.astype(BF16)
    lo = (a - hi.astype(F32)).astype(BF16)
    return hi, lo


def _ada_kernel(c_ref, w_ref, b_ref, o_ref):
    c = c_ref[...]
    ca = (c * jax.nn.sigmoid(c)).astype(BF16)
    o_ref[0] = jnp.dot(ca, w_ref[0].astype(BF16), preferred_element_type=F32) + b_ref[0]


def _ada_modulation(c, ada_w, ada_b):
    depth, d, n = ada_w.shape
    b = c.shape[0]
    return pl.pallas_call(
        _ada_kernel,
        grid=(depth, n // ADA_TN),
        in_specs=[
            pl.BlockSpec((b, d), lambda i, j: (0, 0)),
            pl.BlockSpec((1, d, ADA_TN), lambda i, j: (i, 0, j)),
            pl.BlockSpec((1, 1, ADA_TN), lambda i, j: (i, 0, j)),
        ],
        out_specs=pl.BlockSpec((1, b, ADA_TN), lambda i, j: (i, 0, j)),
        out_shape=jax.ShapeDtypeStruct((depth, b, n), F32),
        compiler_params=_cparams(("arbitrary", "arbitrary")),
        name="ada_modulation",
    )(c, ada_w, ada_b.reshape(depth, 1, n))


def _qkv_na_kernel(x_ref, sc_ref, sh_ref, w_ref, o_ref, *, d_model):
    h = (x_ref[0] * (1.0 + sc_ref[0]) + sh_ref[0]).astype(BF16)
    n = w_ref.shape[1]
    for n0 in range(0, n, QKV_CHUNK):
        acc = jnp.dot(h, w_ref[:, n0:n0 + QKV_CHUNK], preferred_element_type=F32)
        if n0 < d_model:
            acc = acc * Q_SCALE
        o_ref[0, :, n0:n0 + QKV_CHUNK] = acc.astype(BF16)


def _qkv_na(x, sc, sh, w_bf16):
    b, s, d = x.shape
    n = w_bf16.shape[1]
    return pl.pallas_call(
        functools.partial(_qkv_na_kernel, d_model=d),
        grid=(b, s // ROW_TILE),
        in_specs=[
            pl.BlockSpec((1, ROW_TILE, d), lambda i, j: (i, j, 0)),
            pl.BlockSpec((1, 1, d), lambda i, j: (i, 0, 0)),
            pl.BlockSpec((1, 1, d), lambda i, j: (i, 0, 0)),
            pl.BlockSpec((d, n), lambda i, j: (0, 0)),
        ],
        out_specs=pl.BlockSpec((1, ROW_TILE, n), lambda i, j: (i, j, 0)),
        out_shape=jax.ShapeDtypeStruct((b, s, n), BF16),
        compiler_params=_cparams(("arbitrary", "arbitrary")),
        name="qkv_na",
    )(x, sc, sh, w_bf16)


N_DR = 2 * NA_ROWS - 1
N_DC = 2 * NA_COLS - 1


def _na_bias_kernel(r_ref, m_ref, valid_ref, o_ref):
    r = r_ref[...]
    hi = r.astype(BF16)
    r1 = r - hi.astype(F32)
    mid = r1.astype(BF16)
    lo = (r1 - mid.astype(F32)).astype(BF16)
    m = m_ref[...]
    acc = (jnp.dot(hi, m, preferred_element_type=F32) + jnp.dot(mid, m, preferred_element_type=F32)
           + jnp.dot(lo, m, preferred_element_type=F32))
    o_ref[...] = jnp.where(valid_ref[...] > 0.0, acc * LOG2_E, NEG_BIG)


def _na_bias_table(rpb):
    qc = jnp.arange(GRID_W, dtype=jnp.int32)
    kc = jnp.arange(GRID_W, dtype=jnp.int32)
    win_start = jnp.clip(qc - NA_COLS // 2, 0, GRID_W - NA_COLS)
    valid = (kc[None, :] >= win_start[:, None]) & (kc[None, :] < win_start[:, None] + NA_COLS)
    dc = jnp.clip(kc[None, :] - qc[:, None] + NA_COLS - 1, 0, N_DC - 1)
    onehot = (jnp.arange(LANES, dtype=jnp.int32)[:, None] == dc.reshape(1, GRID_W * GRID_W)).astype(BF16)
    r2d = jnp.pad(rpb.reshape(N_HEADS * N_DR, N_DC).astype(F32), ((0, 0), (0, LANES - N_DC)))
    tiles = pl.pallas_call(
        _na_bias_kernel,
        out_shape=jax.ShapeDtypeStruct((N_HEADS * N_DR, GRID_W * GRID_W), F32),
        compiler_params=pltpu.CompilerParams(vmem_limit_bytes=VMEM_LIMIT_BYTES),
        name="na_bias_tiles",
    )(r2d, onehot, valid.reshape(1, GRID_W * GRID_W).astype(F32))
    tiles = tiles.reshape(N_HEADS, N_DR, GRID_W, GRID_W)
    pairs = jnp.concatenate([tiles[:, :N_DR - 1], tiles[:, 1:]], axis=-1)
    pairs = pairs.reshape(N_HEADS // 2, 2, N_DR - 1, GRID_W, HEAD_PAIR).transpose(0, 2, 1, 3, 4)
    return pairs.reshape(N_HEADS // 2, N_DR - 1, 2 * GRID_W, HEAD_PAIR)


def _na_kernel(q_ref, k_ref, v_ref, bias_ref, o_ref, *, rows):
    nk = NA_ROWS * GRID_W
    lane = lax.broadcasted_iota(jnp.int32, (GRID_W, HEAD_PAIR), 1)
    first = lane < HEAD_DIM
    for sub_row in range(NA_ROWS_PER_STEP):
        r = pl.program_id(1) * NA_ROWS_PER_STEP + sub_row
        row_start = jnp.clip(r - NA_ROWS // 2, 0, rows - NA_ROWS)
        start = pl.multiple_of(row_start * GRID_W, GRID_W)
        dr_base = row_start - r + (NA_ROWS - 1)
        q_rows = slice(sub_row * GRID_W, (sub_row + 1) * GRID_W)
        for hp0 in range(0, N_HEADS // 2, NA_STAGE_PAIRS):
            pairs = range(hp0, hp0 + NA_STAGE_PAIRS)
            cols = {hp: slice(hp * HEAD_PAIR, (hp + 1) * HEAD_PAIR) for hp in pairs}
            s_all = {}
            for hp in pairs:
                q2 = q_ref[0, q_rows, cols[hp]]
                k2 = k_ref[0, pl.ds(start, nk), cols[hp]]
                zero = jnp.zeros_like(q2)
                qs = jnp.concatenate([jnp.where(first, q2, zero), jnp.where(first, zero, q2)], axis=0)
                s = lax.dot_general(qs, k2, (((1,), (1,)), ((), ())), preferred_element_type=F32)
                s_all[hp] = s + jnp.concatenate([bias_ref[hp, dr_base + 2 * m] for m in range(NA_ROWS // 2)],
                                                axis=1)
            m_all = {hp: jnp.max(s_all[hp], axis=-1, keepdims=True) for hp in pairs}
            p_all = {hp: jnp.exp2(s_all[hp] - m_all[hp]) for hp in pairs}
            l_all = {hp: jnp.sum(p_all[hp], axis=-1, keepdims=True) for hp in pairs}
            o_all = {hp: jnp.dot(p_all[hp].astype(BF16), v_ref[0, pl.ds(start, nk), cols[hp]],
                                 preferred_element_type=F32) for hp in pairs}
            for hp in pairs:
                o = o_all[hp] / l_all[hp]
                o_ref[0, q_rows, cols[hp]] = jnp.where(first, o[:GRID_W], o[GRID_W:]).astype(BF16)


def _na_attention(qkv, bias, rows):
    b, s, n3 = qkv.shape
    d = n3 // 3
    assert rows >= NA_ROWS and NA_ROWS % 2 == 0 and rows % NA_ROWS_PER_STEP == 0
    q_tile = NA_ROWS_PER_STEP * GRID_W
    return pl.pallas_call(
        functools.partial(_na_kernel, rows=rows),
        grid=(b, rows // NA_ROWS_PER_STEP),
        in_specs=[
            pl.BlockSpec((1, q_tile, d), lambda i, r: (i, r, 0)),
            pl.BlockSpec((1, s, d), lambda i, r: (i, 0, 1)),
            pl.BlockSpec((1, s, d), lambda i, r: (i, 0, 2)),
            pl.BlockSpec(bias.shape, lambda i, r: (0, 0, 0, 0)),
        ],
        out_specs=pl.BlockSpec((1, q_tile, d), lambda i, r: (i, r, 0)),
        out_shape=jax.ShapeDtypeStruct((b, s, d), BF16),
        compiler_params=_cparams(("arbitrary", "arbitrary")),
        name="na_attention",
    )(qkv, qkv, qkv, bias)


def _rms_rope(a, gain, cosf, sinf, seg, seg_t):
    qq_hi, qq_lo = _split_bf16(a * a)
    ms = jnp.dot(qq_hi, seg, preferred_element_type=F32) + jnp.dot(qq_lo, seg, preferred_element_type=F32)
    r_hi, r_lo = _split_bf16(lax.rsqrt(ms + RMS_EPS))
    rf = jnp.dot(r_hi, seg_t, preferred_element_type=F32) + jnp.dot(r_lo, seg_t, preferred_element_type=F32)
    an = a * rf * gain
    n = a.shape[1]
    lane = lax.broadcasted_iota(jnp.int32, a.shape, 1)
    partner = jnp.where(lane % (2 * ROPE_PAIRS) < ROPE_PAIRS,
                        pltpu.roll(an, n - ROPE_PAIRS, 1), pltpu.roll(an, ROPE_PAIRS, 1))
    return an * cosf + partner * sinf


def _qkv_gqa_kernel(x_ref, sc_ref, sh_ref, w_ref, qg_ref, kg_ref, cos_ref, sin_ref, seg_ref, segt_ref, o_ref,
                    *, d_model):
    h = (x_ref[0] * (1.0 + sc_ref[0]) + sh_ref[0]).astype(BF16)
    n = w_ref.shape[1]
    kv_cols = (n - d_model) // 2
    cosf, sinf = cos_ref[...], sin_ref[...]
    seg, seg_t = seg_ref[...], segt_ref[...]
    for n0 in range(0, n, QKV_CHUNK):
        acc = jnp.dot(h, w_ref[:, n0:n0 + QKV_CHUNK], preferred_element_type=F32)
        if n0 < d_model:
            acc = _rms_rope(acc, qg_ref[...], cosf, sinf, seg, seg_t) * Q_SCALE
        elif n0 < d_model + kv_cols:
            acc = _rms_rope(acc, kg_ref[...], cosf, sinf, seg, seg_t)
        else:
            lane = lax.broadcasted_iota(jnp.int32, acc.shape, 1)
            acc = jnp.where(lane % HEAD_PAIR < HEAD_DIM, acc, 1.0)
        o_ref[0, :, n0:n0 + QKV_CHUNK] = acc.astype(BF16)


def _qkv_gqa(x, sc, sh, w_bf16, q_gain, k_gain, cosf, sinf, seg, seg_t):
    b, s, d = x.shape
    n = w_bf16.shape[1]
    const = lambda j, i: (0, 0)
    return pl.pallas_call(
        functools.partial(_qkv_gqa_kernel, d_model=d),
        grid=(s // ROW_TILE, b),
        in_specs=[
            pl.BlockSpec((1, ROW_TILE, d), lambda j, i: (i, j, 0)),
            pl.BlockSpec((1, 1, d), lambda j, i: (i, 0, 0)),
            pl.BlockSpec((1, 1, d), lambda j, i: (i, 0, 0)),
            pl.BlockSpec((d, n), const),
            pl.BlockSpec((1, QKV_CHUNK), const),
            pl.BlockSpec((1, QKV_CHUNK), const),
            pl.BlockSpec((ROW_TILE, QKV_CHUNK), lambda j, i: (j, 0)),
            pl.BlockSpec((ROW_TILE, QKV_CHUNK), lambda j, i: (j, 0)),
            pl.BlockSpec((QKV_CHUNK, LANES), const),
            pl.BlockSpec((LANES, QKV_CHUNK), const),
        ],
        out_specs=pl.BlockSpec((1, ROW_TILE, n), lambda j, i: (i, j, 0)),
        out_shape=jax.ShapeDtypeStruct((b, s, n), BF16),
        compiler_params=_cparams(("arbitrary", "arbitrary")),
        name="qkv_gqa",
    )(x, sc, sh, w_bf16, q_gain, k_gain, cosf, sinf, seg, seg_t)


def _gqa_kernel(bound_ref, q_ref, k_ref, v_ref, o_ref, *, use_bound):
    k2 = k_ref[0]
    v2 = v_ref[0]
    tq = q_ref.shape[1]
    lane = lax.broadcasted_iota(jnp.int32, (tq, HEAD_PAIR), 1)
    first = lane < HEAD_DIM
    for pair in range(q_ref.shape[2] // HEAD_PAIR):
        cols = slice(pair * HEAD_PAIR, (pair + 1) * HEAD_PAIR)
        q2 = q_ref[0, :, cols]
        zero = jnp.zeros_like(q2)
        qs = jnp.concatenate([jnp.where(first, q2, zero), jnp.where(first, zero, q2)], axis=0)
        s = lax.dot_general(qs, k2, (((1,), (1,)), ((), ())), preferred_element_type=F32)
        m = bound_ref[0] if use_bound else jnp.max(s, axis=-1, keepdims=True)
        p = jnp.exp2(s - m).astype(BF16)
        o = jnp.dot(p, v2, preferred_element_type=F32)
        o_a, o_b = o[:tq], o[tq:]
        out = jnp.where(first, o_a / pltpu.roll(o_a, HEAD_DIM, 1), pltpu.roll(o_b, HEAD_DIM, 1) / o_b)
        o_ref[0, :, cols] = out.astype(BF16)


def _gqa_logit_bound(q_gain, k_gain):
    return (GQA_BOUND_MARGIN * LOG2_E * (HEAD_DIM ** 0.5)
            * jnp.max(jnp.abs(q_gain)) * jnp.max(jnp.abs(k_gain))).astype(F32).reshape(1)


def _gqa_attention(qkv, d_model, q_gain, k_gain):
    bound = _gqa_logit_bound(q_gain, k_gain)
    return lax.cond(bound[0] <= GQA_BOUND_MAX,
                    functools.partial(_gqa_attention_call, d_model=d_model, use_bound=True),
                    functools.partial(_gqa_attention_call, d_model=d_model, use_bound=False),
                    bound, qkv)


def _gqa_attention_call(bound, qkv, *, d_model, use_bound):
    b, s, n = qkv.shape
    group_cols = d_model // GQA_KV_HEADS
    k_block0 = d_model // HEAD_PAIR
    v_block0 = k_block0 + GQA_KV_HEADS
    return pl.pallas_call(
        functools.partial(_gqa_kernel, use_bound=use_bound),
        grid=(b, GQA_KV_HEADS, s // GQA_TQ),
        in_specs=[
            pl.BlockSpec(memory_space=pltpu.SMEM),
            pl.BlockSpec((1, GQA_TQ, group_cols), lambda i, g, t: (i, t, g)),
            pl.BlockSpec((1, s, HEAD_PAIR), lambda i, g, t: (i, 0, k_block0 + g)),
            pl.BlockSpec((1, s, HEAD_PAIR), lambda i, g, t: (i, 0, v_block0 + g)),
        ],
        out_specs=pl.BlockSpec((1, GQA_TQ, group_cols), lambda i, g, t: (i, t, g)),
        out_shape=jax.ShapeDtypeStruct((b, s, d_model), BF16),
        compiler_params=_cparams(("arbitrary", "arbitrary", "arbitrary")),
        name="gqa_attention_bounded" if use_bound else "gqa_attention",
    )(bound, qkv, qkv, qkv)


def _store_token_tiles(ref, val):
    t, d = val.shape
    sub = d // LANES
    for s in range(sub):
        ref[pl.ds(s, t, stride=sub), :] = val[:, s * LANES:(s + 1) * LANES]


def _load_token_tiles(ref, t, dtype):
    sub = ref.shape[0] // t
    return jnp.concatenate([ref[pl.ds(s, t, stride=sub), :].astype(dtype) for s in range(sub)], axis=1)


def _tile_copy(src_hbm, dst_vmem, src_tok, dst_tok, sub, sem):
    return pltpu.make_async_copy(src_hbm.at[pl.ds(pl.multiple_of(src_tok * sub, sub), sub)],
                                 dst_vmem.at[pl.ds(pl.multiple_of(dst_tok * sub, sub), sub)], sem)


def _layer_norm_rows(z, g, b):
    mu = jnp.mean(z, axis=-1, keepdims=True)
    zc = z - mu
    var = jnp.mean(zc * zc, axis=-1, keepdims=True)
    return zc * lax.rsqrt(var + LN_EPS) * g + b


def _router(lg):
    lane = lax.broadcasted_iota(jnp.int32, lg.shape, 1)
    lanef = lane.astype(F32)
    far = float(LANES)
    gmask = lane < N_GROUPS
    lgm = jnp.where(gmask, lg, NEG_BIG)
    mg = jnp.max(lgm, axis=-1, keepdims=True)
    sg = jnp.sum(jnp.exp(lgm - mg), axis=-1, keepdims=True)
    g_prob = 1.0 / sg
    g_idx = jnp.min(jnp.where(lgm == mg, lanef, far), axis=-1, keepdims=True)
    e_lo = N_GROUPS + g_idx * EXPERTS_PER_GROUP
    emask = (lanef >= e_lo) & (lanef < e_lo + EXPERTS_PER_GROUP)
    lem = jnp.where(emask, lg, NEG_BIG)
    me = jnp.max(lem, axis=-1, keepdims=True)
    ee = jnp.exp(lem - me)
    pe = jnp.where(emask, ee / jnp.sum(ee, axis=-1, keepdims=True), -1.0)
    p1 = jnp.max(pe, axis=-1, keepdims=True)
    i1 = jnp.min(jnp.where(pe == p1, lanef, far), axis=-1, keepdims=True)
    pe2 = jnp.where(lanef == i1, -1.0, pe)
    p2 = jnp.max(pe2, axis=-1, keepdims=True)
    i2 = jnp.min(jnp.where(pe2 == p2, lanef, far), axis=-1, keepdims=True)
    den = p1 + p2
    gate1 = g_prob * (p1 / den)
    gate2 = g_prob * (p2 / den)
    out = jnp.where(lane == 0, gate1, 0.0)
    out = jnp.where(lane == 1, gate2, out)
    out = jnp.where(lane == 2, i1 - N_GROUPS, out)
    out = jnp.where(lane == 3, i2 - N_GROUPS, out)
    return out


def _post_attn_kernel(a_ref, wo_ref, x_ref, g1_ref, sc2_ref, sh2_ref, lng_ref, lnb_ref, wr_ref, br_ref,
                      xo_ref, h2_ref, route_ref, *, alpha):
    y = jnp.dot(a_ref[0], wo_ref[...], preferred_element_type=F32)
    z = alpha * x_ref[0] + (1.0 + g1_ref[0]) * y
    xn = _layer_norm_rows(z, lng_ref[...], lnb_ref[...])
    xo_ref[0] = xn
    h2 = xn * (1.0 + sc2_ref[0]) + sh2_ref[0]
    _store_token_tiles(h2_ref, h2)
    h_hi, h_lo = _split_bf16(h2)
    t = jnp.dot(h_hi, wr_ref[...], preferred_element_type=F32)
    lg = (t[:, :LANES] + t[:, LANES:] + jnp.dot(h_lo, wr_ref[:, :LANES], preferred_element_type=F32)
          + br_ref[...])
    route_ref[0] = _router(lg)


def _post_attn(a, wo_bf16, x, g1, sc2, sh2, ln_g, ln_b, wr, br, alpha):
    b, s, d = x.shape
    sub = d // LANES
    tiles_per_batch = s // ROW_TILE
    const = lambda i, j: (0, 0)
    tile = lambda i, j: (i, j, 0)
    per_b = lambda i, j: (i, 0, 0)
    return pl.pallas_call(
        functools.partial(_post_attn_kernel, alpha=alpha),
        grid=(b, s // ROW_TILE),
        in_specs=[
            pl.BlockSpec((1, ROW_TILE, d), tile),
            pl.BlockSpec((d, d), const),
            pl.BlockSpec((1, ROW_TILE, d), tile),
            pl.BlockSpec((1, 1, d), per_b),
            pl.BlockSpec((1, 1, d), per_b),
            pl.BlockSpec((1, 1, d), per_b),
            pl.BlockSpec((1, d), const),
            pl.BlockSpec((1, d), const),
            pl.BlockSpec((d, 2 * LANES), const),
            pl.BlockSpec((1, LANES), const),
        ],
        out_specs=[
            pl.BlockSpec((1, ROW_TILE, d), tile),
            pl.BlockSpec((ROW_TILE * sub, LANES), lambda i, j: (i * tiles_per_batch + j, 0)),
            pl.BlockSpec((1, ROW_TILE, LANES), tile),
        ],
        out_shape=[
            jax.ShapeDtypeStruct((b, s, d), F32),
            jax.ShapeDtypeStruct((b * s * sub, LANES), F32),
            jax.ShapeDtypeStruct((b, s, LANES), F32),
        ],
        compiler_params=_cparams(("arbitrary", "arbitrary")),
        name="post_attn",
    )(a, wo_bf16, x, g1, sc2, sh2, ln_g, ln_b, wr, br)


def _dispatch_kernel(pos_ref, h_ref, xs_hbm, sem):
    tm = DISPATCH_TM
    sub = h_ref.shape[0] // tm

    def issue(jb, carry):
        for u in range(DMA_UNROLL):
            j = jb * DMA_UNROLL + u
            src = h_ref.at[pl.ds(pl.multiple_of(j * sub, sub), sub)]
            for k in range(TOP_K):
                row = pos_ref[0, 0, TOP_K * j + k]
                dst = xs_hbm.at[pl.ds(pl.multiple_of(row * sub, sub), sub)]
                pltpu.make_async_copy(src, dst, sem).start(priority=k % 2)
        return carry

    lax.fori_loop(0, tm // DMA_UNROLL, issue, 0)
    for _ in range(TOP_K):
        pltpu.make_async_copy(h_ref, xs_hbm.at[pl.ds(0, tm * sub)], sem).wait()


def _dispatch(pos, h2_tiles, n_tok):
    sub = h2_tiles.shape[0] // n_tok
    tm = DISPATCH_TM
    n_tiles = n_tok // tm
    return pl.pallas_call(
        _dispatch_kernel,
        grid=(n_tiles,),
        in_specs=[
            pl.BlockSpec((1, 1, TOP_K * tm), lambda i: (i, 0, 0), memory_space=pltpu.SMEM),
            pl.BlockSpec((tm * sub, LANES), lambda i: (i, 0)),
        ],
        out_specs=pl.BlockSpec(memory_space=pl.ANY),
        out_shape=jax.ShapeDtypeStruct((n_tok * TOP_K * sub, LANES), F32),
        scratch_shapes=[pltpu.SemaphoreType.DMA],
        compiler_params=_cparams(("arbitrary",)),
        name="moe_dispatch",
    )(pos.reshape(n_tiles, 1, TOP_K * tm), h2_tiles)


def _expert_kernel(blk_ref, exp_ref, lo_ref, hi_ref, x_ref, wg_ref, wu_ref, wd_ref, y_ref,
                   acc, wgb, wub, wdb, cast_expert, state):
    i = pl.program_id(0)
    n_items = pl.num_programs(0)
    tb = MOE_TB
    prev = jnp.maximum(i - 1, 0)
    nxt = jnp.minimum(i + 1, n_items - 1)
    first = (i == 0) | (blk_ref[i] != blk_ref[prev])
    last = (i == n_items - 1) | (blk_ref[i] != blk_ref[nxt])
    lo, hi = lo_ref[i], hi_ref[i]
    whole = (lo == 0) & (hi == tb)

    @pl.when(i == 0)
    def _():
        cast_expert[0] = -1

    @pl.when(first)
    def _():
        state[0] = 0
        state[1] = 0

    @pl.when(hi > lo)
    def _():
        @pl.when(cast_expert[0] != exp_ref[i])
        def _():
            wgb[...] = wg_ref[0, 0].astype(BF16)
            wub[...] = wu_ref[0, 0].astype(BF16)
            wdb[...] = wd_ref[0, 0].astype(BF16)
            cast_expert[0] = exp_ref[i]

        xb = _load_token_tiles(x_ref, tb, BF16)
        g = jnp.dot(xb, wgb[...], preferred_element_type=F32)
        u = jnp.dot(xb, wub[...], preferred_element_type=F32)
        act = (g * jax.nn.sigmoid(g) * u).astype(BF16)
        y = jnp.dot(act, wdb[...], preferred_element_type=F32)

        @pl.when(whole)
        def _():
            _store_token_tiles(y_ref, y)
            state[0] = 1

        @pl.when(jnp.logical_not(whole))
        def _():
            row = lax.broadcasted_iota(jnp.int32, (tb, 1), 0)
            part = jnp.where((row >= lo) & (row < hi), y, 0.0)

            @pl.when(state[1] == 0)
            def _():
                acc[...] = part

            @pl.when(state[1] != 0)
            def _():
                acc[...] += part

            state[1] = 1

    @pl.when(last & (state[0] == 0))
    def _():
        _store_token_tiles(y_ref, acc[...])


def _expert_mlps(items, xs_tiles, w_gate, w_up, w_down, layer):
    item_blk, item_exp, item_lo, item_hi = items
    n_items = item_blk.shape[0]
    _, _, d, de = w_gate.shape
    sub = d // LANES
    w_index = lambda i, blk, ex, lo, hi: (layer, ex[i], 0, 0)
    grid_spec = pltpu.PrefetchScalarGridSpec(
        num_scalar_prefetch=4,
        grid=(n_items,),
        in_specs=[
            pl.BlockSpec((MOE_TB * sub, LANES), lambda i, blk, ex, lo, hi: (blk[i], 0)),
            pl.BlockSpec((1, 1, d, de), w_index),
            pl.BlockSpec((1, 1, d, de), w_index),
            pl.BlockSpec((1, 1, de, d), w_index),
        ],
        out_specs=pl.BlockSpec((MOE_TB * sub, LANES), lambda i, blk, ex, lo, hi: (blk[i], 0)),
        scratch_shapes=[
            pltpu.VMEM((MOE_TB, d), F32),
            pltpu.VMEM((d, de), BF16),
            pltpu.VMEM((d, de), BF16),
            pltpu.VMEM((de, d), BF16),
            pltpu.SMEM((1,), jnp.int32),
            pltpu.SMEM((2,), jnp.int32),
        ],
    )
    return pl.pallas_call(
        _expert_kernel,
        grid_spec=grid_spec,
        out_shape=jax.ShapeDtypeStruct(xs_tiles.shape, F32),
        compiler_params=_cparams(("arbitrary",)),
        name="expert_mlps",
    )(item_blk, item_exp, item_lo, item_hi, xs_tiles, w_gate, w_up, w_down)


def _combine_kernel(pos_ref, ys_hbm, x_ref, route_ref, g2_ref, lng_ref, lnb_ref, xo_ref, buf0, buf1, sem,
                    *, alpha):
    tm = COMBINE_TM
    sub = buf0.shape[0] // tm

    def issue(jb, carry):
        for u in range(DMA_UNROLL):
            j = jb * DMA_UNROLL + u
            _tile_copy(ys_hbm, buf0, pos_ref[0, 0, TOP_K * j], j, sub, sem).start(priority=0)
            _tile_copy(ys_hbm, buf1, pos_ref[0, 0, TOP_K * j + 1], j, sub, sem).start(priority=1)
        return carry

    lax.fori_loop(0, tm // DMA_UNROLL, issue, 0)
    route = route_ref[...]
    lane = lax.broadcasted_iota(jnp.int32, route.shape, 1)
    gate0 = jnp.sum(jnp.where(lane == 0, route, 0.0), axis=-1, keepdims=True)
    gate1 = jnp.sum(jnp.where(lane == 1, route, 0.0), axis=-1, keepdims=True)
    pltpu.make_async_copy(ys_hbm.at[pl.ds(0, tm * sub)], buf0, sem).wait()
    pltpu.make_async_copy(ys_hbm.at[pl.ds(0, tm * sub)], buf1, sem).wait()
    y = _load_token_tiles(buf0, tm, F32) * gate0 + _load_token_tiles(buf1, tm, F32) * gate1
    z = alpha * x_ref[...] + (1.0 + g2_ref[0]) * y
    xo_ref[...] = _layer_norm_rows(z, lng_ref[...], lnb_ref[...])


def _combine(pos, ys, x2d, route2d, g2, ln_g, ln_b, seq, alpha):
    n_tok, d = x2d.shape
    tm = COMBINE_TM
    n_tiles = n_tok // tm
    tiles_per_batch = seq // tm
    const = lambda i: (0, 0)
    return pl.pallas_call(
        functools.partial(_combine_kernel, alpha=alpha),
        grid=(n_tiles,),
        in_specs=[
            pl.BlockSpec((1, 1, TOP_K * tm), lambda i: (i, 0, 0), memory_space=pltpu.SMEM),
            pl.BlockSpec(memory_space=pl.ANY),
            pl.BlockSpec((tm, d), lambda i: (i, 0)),
            pl.BlockSpec((tm, LANES), lambda i: (i, 0)),
            pl.BlockSpec((1, 1, d), lambda i: (i // tiles_per_batch, 0, 0)),
            pl.BlockSpec((1, d), const),
            pl.BlockSpec((1, d), const),
        ],
        out_specs=pl.BlockSpec((tm, d), lambda i: (i, 0)),
        out_shape=jax.ShapeDtypeStruct((n_tok, d), F32),
        scratch_shapes=[
            pltpu.VMEM((tm * (d // LANES), LANES), F32),
            pltpu.VMEM((tm * (d // LANES), LANES), F32),
            pltpu.SemaphoreType.DMA,
        ],
        compiler_params=_cparams(("arbitrary",)),
        name="moe_combine",
    )(pos.reshape(n_tiles, 1, TOP_K * tm), ys, x2d, route2d, g2, ln_g, ln_b)


def _route_pos_kernel(route_ref, pos_ref, cnt_ref, counts, base):
    p = pl.program_id(0)
    j = pl.program_id(1)
    route = route_ref[...]
    t = route.shape[0]
    lane = lax.broadcasted_iota(jnp.int32, route.shape, 1)
    lanef = lane.astype(F32)
    e1 = jnp.sum(jnp.where(lane == 2, route, 0.0), axis=-1, keepdims=True)
    e2 = jnp.sum(jnp.where(lane == 3, route, 0.0), axis=-1, keepdims=True)
    m1 = lanef == e1
    m2 = lanef == e2
    mem = jnp.where(m1 | m2, 1.0, 0.0)
    colsum = jnp.sum(mem, axis=0, keepdims=True)

    @pl.when((p == 0) & (j == 0))
    def _():
        counts[...] = jnp.zeros_like(counts)

    @pl.when(p == 0)
    def _():
        counts[...] = counts[...] + colsum

    @pl.when((p == 1) & (j == 0))
    def _():
        c = counts[...]
        cnt_ref[...] = c
        c_hi = jnp.floor(c * (1.0 / 256.0))
        c_lo = c - c_hi * 256.0
        r_i = lax.broadcasted_iota(jnp.int32, (LANES, LANES), 0)
        c_i = lax.broadcasted_iota(jnp.int32, (LANES, LANES), 1)
        excl = jnp.where(r_i < c_i, 1.0, 0.0).astype(BF16)
        base[...] = (jnp.dot(c_hi.astype(BF16), excl, preferred_element_type=F32) * 256.0
                     + jnp.dot(c_lo.astype(BF16), excl, preferred_element_type=F32))

    @pl.when(p == 1)
    def _():
        r_i = lax.broadcasted_iota(jnp.int32, (t, t), 0)
        c_i = lax.broadcasted_iota(jnp.int32, (t, t), 1)
        before = jnp.where(c_i < r_i, 1.0, 0.0).astype(BF16)
        rank = base[0:1, :] + jnp.dot(before, mem.astype(BF16), preferred_element_type=F32)
        pos1 = jnp.sum(jnp.where(m1, rank, 0.0), axis=-1, keepdims=True)
        pos2 = jnp.sum(jnp.where(m2, rank, 0.0), axis=-1, keepdims=True)
        pos_ref[...] = jnp.where(lane == 0, pos1, jnp.where(lane == 1, pos2, 0.0))
        base[...] = base[...] + colsum


def _routing_tables(route2d):
    n_tok = route2d.shape[0]
    n_tiles = n_tok // ROUTE_T
    posf, cnt = pl.pallas_call(
        _route_pos_kernel,
        grid=(2, n_tiles),
        in_specs=[pl.BlockSpec((ROUTE_T, LANES), lambda p, j: (j, 0))],
        out_specs=[
            pl.BlockSpec((ROUTE_T, LANES), lambda p, j: (p * j, 0)),
            pl.BlockSpec((8, LANES), lambda p, j: (0, 0)),
        ],
        out_shape=[
            jax.ShapeDtypeStruct((n_tok, LANES), F32),
            jax.ShapeDtypeStruct((8, LANES), F32),
        ],
        scratch_shapes=[pltpu.VMEM((8, LANES), F32), pltpu.VMEM((8, LANES), F32)],
        compiler_params=_cparams(("arbitrary", "arbitrary")),
        name="route_positions",
    )(route2d)
    n_rows = n_tok * TOP_K
    pos = posf[:, :TOP_K].astype(jnp.int32).reshape(n_rows)
    counts = cnt[0, :N_EXPERTS].astype(jnp.int32)
    seg_end = jnp.cumsum(counts)
    cuts = jnp.concatenate([jnp.arange(n_rows // MOE_TB, dtype=jnp.int32) * MOE_TB, seg_end - counts])
    n_items = cuts.shape[0]
    idx = jnp.arange(n_items, dtype=jnp.int32)
    rank = (jnp.sum(cuts[None, :] < cuts[:, None], axis=1)
            + jnp.sum((cuts[None, :] == cuts[:, None]) & (idx[None, :] < idx[:, None]), axis=1))
    lo_abs = jnp.sum(jnp.where(rank[None, :] == idx[:, None], cuts[None, :], 0), axis=1)
    hi_abs = jnp.concatenate([lo_abs[1:], jnp.full((1,), n_rows, jnp.int32)])
    item_blk = jnp.minimum(lo_abs // MOE_TB, n_rows // MOE_TB - 1)
    item_exp = jnp.minimum(jnp.sum(seg_end[None, :] <= lo_abs[:, None], axis=1), N_EXPERTS - 1)
    item_lo = lo_abs - item_blk * MOE_TB
    item_hi = hi_abs - item_blk * MOE_TB
    items = tuple(a.astype(jnp.int32) for a in (item_blk, item_exp, item_lo, item_hi))
    return pos, items


def _rope_tables(seq, width):
    t = jnp.arange(seq, dtype=jnp.int32)
    pos = jnp.stack([t // GRID_W, t % GRID_W], axis=-1).astype(F32)
    inv_freq = ROPE_THETA ** (-jnp.arange(ROPE_PAIRS, dtype=F32) / ROPE_PAIRS)
    ang = pos[:, :, None] * inv_freq
    cos, sin = jnp.cos(ang), jnp.sin(ang)
    cosf = jnp.stack([cos, cos], axis=2).reshape(seq, HEAD_DIM)
    sinf = jnp.stack([-sin, sin], axis=2).reshape(seq, HEAD_DIM)
    reps = width // HEAD_DIM
    return jnp.tile(cosf, (1, reps)), jnp.tile(sinf, (1, reps))


def _dup_heads(w):
    d = w.shape[0]
    w4 = w.reshape(d, GQA_KV_HEADS, 1, HEAD_DIM)
    return jnp.concatenate([w4, w4], axis=2).reshape(d, GQA_KV_HEADS * HEAD_PAIR)


def kernel(x, c, ada_w, ada_b, ln_g, ln_b, na_w_qkv, na_rpb, na_w_o, gqa_w_qkv, gqa_q_norm, gqa_k_norm, gqa_w_o,
           moe_w_group, moe_b_group, moe_w_expert, moe_b_expert, moe_w_gate, moe_w_up, moe_w_down):
    b, s, d = x.shape
    depth = ada_w.shape[0]
    n_tok = b * s
    rows = s // GRID_W
    kvd = GQA_KV_HEADS * HEAD_DIM
    alpha = (2 * depth) ** 0.25

    mod = _ada_modulation(c, ada_w, ada_b).reshape(depth, b, 6, 1, d)

    cosf, sinf = _rope_tables(s, QKV_CHUNK)
    head_of_lane = jnp.arange(QKV_CHUNK, dtype=jnp.int32) // HEAD_DIM
    seg_t = (head_of_lane[None, :] == jnp.arange(LANES, dtype=jnp.int32)[:, None])
    seg = (seg_t.T.astype(F32) / HEAD_DIM).astype(BF16)
    seg_t = seg_t.astype(BF16)

    for i in range(depth):
        sh1, sc1, g1, sh2, sc2, g2 = (mod[i, :, k] for k in range(6))
        j = i // 2
        if i % 2 == 0:
            qkv = _qkv_na(x, sc1, sh1, na_w_qkv[j].astype(BF16))
            attn = _na_attention(qkv, _na_bias_table(na_rpb[j]), rows)
            w_o = na_w_o[j]
        else:
            w = gqa_w_qkv[j]
            w_all = jnp.concatenate([w[:, :d], _dup_heads(w[:, d:d + kvd]), _dup_heads(w[:, d + kvd:])], axis=1)
            reps = QKV_CHUNK // HEAD_DIM
            qkv = _qkv_gqa(x, sc1, sh1, w_all.astype(BF16),
                           jnp.tile(gqa_q_norm[j], reps)[None, :], jnp.tile(gqa_k_norm[j], reps)[None, :],
                           cosf, sinf, seg, seg_t)
            attn = _gqa_attention(qkv, d, gqa_q_norm[j], gqa_k_norm[j])
            w_o = gqa_w_o[j]

        w_r = jnp.concatenate([moe_w_group[i], moe_w_expert[i]], axis=1)
        w_r = jnp.pad(w_r, ((0, 0), (0, LANES - w_r.shape[1])))
        wr_hi, wr_lo = _split_bf16(w_r)
        b_r = jnp.concatenate([moe_b_group[i], moe_b_expert[i]])
        b_r = jnp.pad(b_r, (0, LANES - b_r.shape[0]))[None, :]
        x, h2, route = _post_attn(attn, w_o.astype(BF16), x, g1, sc2, sh2, ln_g[i, 0][None, :], ln_b[i, 0][None, :],
                                  jnp.concatenate([wr_hi, wr_lo], axis=1), b_r, alpha)

        route2d = route.reshape(n_tok, LANES)
        pos, items = _routing_tables(route2d)
        xs = _dispatch(pos, h2, n_tok)
        ys = _expert_mlps(items, xs, moe_w_gate, moe_w_up, moe_w_down, i)
        x = _combine(pos, ys, x.reshape(n_tok, d), route2d, g2, ln_g[i, 1][None, :], ln_b[i, 1][None, :],
                     s, alpha).reshape(b, s, d)
    return x
```

```python
import functools

import jax
import jax.numpy as jnp
from jax import lax
from jax.experimental import pallas as pl
from jax.experimental.pallas import tpu as pltpu

GRID_W = 64
N_HEADS = 16
HEAD_DIM = 64
NA_ROWS = 8
NA_COLS = 16
GQA_KV_HEADS = 4
ROPE_THETA = 10000.0
ROPE_PAIRS = HEAD_DIM // 4
N_GROUPS = 4
EXPERTS_PER_GROUP = 8
N_EXPERTS = N_GROUPS * EXPERTS_PER_GROUP
TOP_K = 2
LN_EPS = 1e-5
RMS_EPS = 1e-6

LANES = 128
HEAD_PAIR = 2 * HEAD_DIM
VMEM_LIMIT_BYTES = 52 * 1024 * 1024
ADA_TN = 1536
ROW_TILE = 512
QKV_CHUNK = 512
GQA_TQ = 512
NA_ROWS_PER_STEP = 2
NA_STAGE_PAIRS = 4
MOE_TB = 256
DISPATCH_TM = 1024
COMBINE_TM = 1024
ROUTE_T = 512
DMA_UNROLL = 8
NEG_BIG = -1e30
LOG2_E = 1.4426950408889634
Q_SCALE = (HEAD_DIM ** -0.5) * LOG2_E
GQA_BOUND_MARGIN = 1.02
GQA_BOUND_MAX = 50.0

F32 = jnp.float32
BF16 = jnp.bfloat16


def _cparams(sem):
    return pltpu.CompilerParams(dimension_semantics=sem, vmem_limit_bytes=VMEM_LIMIT_BYTES)


def _split_bf16(a):
    hi = a.astype(BF16)
    lo = (a - hi.astype(F32)).astype(BF16)
    return hi, lo


def _ada_kernel(c_ref, w_ref, b_ref, o_ref):
    c = c_ref[...]
    ca = (c * jax.nn.sigmoid(c)).astype(BF16)
    o_ref[0] = jnp.dot(ca, w_ref[0].astype(BF16), preferred_element_type=F32) + b_ref[0]


def _ada_modulation(c, ada_w, ada_b):
    depth, d, n = ada_w.shape
    b = c.shape[0]
    return pl.pallas_call(
        _ada_kernel,
        grid=(depth, n // ADA_TN),
        in_specs=[
            pl.BlockSpec((b, d), lambda i, j: (0, 0)),
            pl.BlockSpec((1, d, ADA_TN), lambda i, j: (i, 0, j)),
            pl.BlockSpec((1, 1, ADA_TN), lambda i, j: (i, 0, j)),
        ],
        out_specs=pl.BlockSpec((1, b, ADA_TN), lambda i, j: (i, 0, j)),
        out_shape=jax.ShapeDtypeStruct((depth, b, n), F32),
        compiler_params=_cparams(("arbitrary", "arbitrary")),
        name="ada_modulation",
    )(c, ada_w, ada_b.reshape(depth, 1, n))


def _qkv_na_kernel(x_ref, sc_ref, sh_ref, w_ref, o_ref, *, d_model):
    h = (x_ref[0] * (1.0 + sc_ref[0]) + sh_ref[0]).astype(BF16)
    n = w_ref.shape[1]
    for n0 in range(0, n, QKV_CHUNK):
        acc = jnp.dot(h, w_ref[:, n0:n0 + QKV_CHUNK], preferred_element_type=F32)
        if n0 < d_model:
            acc = acc * Q_SCALE
        o_ref[0, :, n0:n0 + QKV_CHUNK] = acc.astype(BF16)


def _qkv_na(x, sc, sh, w_bf16):
    b, s, d = x.shape
    n = w_bf16.shape[1]
    return pl.pallas_call(
        functools.partial(_qkv_na_kernel, d_model=d),
        grid=(b, s // ROW_TILE),
        in_specs=[
            pl.BlockSpec((1, ROW_TILE, d), lambda i, j: (i, j, 0)),
            pl.BlockSpec((1, 1, d), lambda i, j: (i, 0, 0)),
            pl.BlockSpec((1, 1, d), lambda i, j: (i, 0, 0)),
            pl.BlockSpec((d, n), lambda i, j: (0, 0)),
        ],
        out_specs=pl.BlockSpec((1, ROW_TILE, n), lambda i, j: (i, j, 0)),
        out_shape=jax.ShapeDtypeStruct((b, s, n), BF16),
        compiler_params=_cparams(("arbitrary", "arbitrary")),
        name="qkv_na",
    )(x, sc, sh, w_bf16)


N_DR = 2 * NA_ROWS - 1
N_DC = 2 * NA_COLS - 1


def _na_bias_kernel(r_ref, m_ref, valid_ref, o_ref):
    r = r_ref[...]
    hi = r.astype(BF16)
    r1 = r - hi.astype(F32)
    mid = r1.astype(BF16)
    lo = (r1 - mid.astype(F32)).astype(BF16)
    m = m_ref[...]
    acc = (jnp.dot(hi, m, preferred_element_type=F32) + jnp.dot(mid, m, preferred_element_type=F32)
           + jnp.dot(lo, m, preferred_element_type=F32))
    o_ref[...] = jnp.where(valid_ref[...] > 0.0, acc * LOG2_E, NEG_BIG)


def _na_bias_table(rpb):
    qc = jnp.arange(GRID_W, dtype=jnp.int32)
    kc = jnp.arange(GRID_W, dtype=jnp.int32)
    win_start = jnp.clip(qc - NA_COLS // 2, 0, GRID_W - NA_COLS)
    valid = (kc[None, :] >= win_start[:, None]) & (kc[None, :] < win_start[:, None] + NA_COLS)
    dc = jnp.clip(kc[None, :] - qc[:, None] + NA_COLS - 1, 0, N_DC - 1)
    onehot = (jnp.arange(LANES, dtype=jnp.int32)[:, None] == dc.reshape(1, GRID_W * GRID_W)).astype(BF16)
    r2d = jnp.pad(rpb.reshape(N_HEADS * N_DR, N_DC).astype(F32), ((0, 0), (0, LANES - N_DC)))
    tiles = pl.pallas_call(
        _na_bias_kernel,
        out_shape=jax.ShapeDtypeStruct((N_HEADS * N_DR, GRID_W * GRID_W), F32),
        compiler_params=pltpu.CompilerParams(vmem_limit_bytes=VMEM_LIMIT_BYTES),
        name="na_bias_tiles",
    )(r2d, onehot, valid.reshape(1, GRID_W * GRID_W).astype(F32))
    tiles = tiles.reshape(N_HEADS, N_DR, GRID_W, GRID_W)
    pairs = jnp.concatenate([tiles[:, :N_DR - 1], tiles[:, 1:]], axis=-1)
    pairs = pairs.reshape(N_HEADS // 2, 2, N_DR - 1, GRID_W, HEAD_PAIR).transpose(0, 2, 1, 3, 4)
    return pairs.reshape(N_HEADS // 2, N_DR - 1, 2 * GRID_W, HEAD_PAIR)


def _na_kernel(q_ref, k_ref, v_ref, bias_ref, o_ref, *, rows):
    nk = NA_ROWS * GRID_W
    lane = lax.broadcasted_iota(jnp.int32, (GRID_W, HEAD_PAIR), 1)
    first = lane < HEAD_DIM
    for sub_row in range(NA_ROWS_PER_STEP):
        r = pl.program_id(1) * NA_ROWS_PER_STEP + sub_row
        row_start = jnp.clip(r - NA_ROWS // 2, 0, rows - NA_ROWS)
        start = pl.multiple_of(row_start * GRID_W, GRID_W)
        dr_base = row_start - r + (NA_ROWS - 1)
        q_rows = slice(sub_row * GRID_W, (sub_row + 1) * GRID_W)
        for hp0 in range(0, N_HEADS // 2, NA_STAGE_PAIRS):
            pairs = range(hp0, hp0 + NA_STAGE_PAIRS)
            cols = {hp: slice(hp * HEAD_PAIR, (hp + 1) * HEAD_PAIR) for hp in pairs}
            s_all = {}
            for hp in pairs:
                q2 = q_ref[0, q_rows, cols[hp]]
                k2 = k_ref[0, pl.ds(start, nk), cols[hp]]
                zero = jnp.zeros_like(q2)
                qs = jnp.concatenate([jnp.where(first, q2, zero), jnp.where(first, zero, q2)], axis=0)
                s = lax.dot_general(qs, k2, (((1,), (1,)), ((), ())), preferred_element_type=F32)
                s_all[hp] = s + jnp.concatenate([bias_ref[hp, dr_base + 2 * m] for m in range(NA_ROWS // 2)],
                                                axis=1)
            m_all = {hp: jnp.max(s_all[hp], axis=-1, keepdims=True) for hp in pairs}
            p_all = {hp: jnp.exp2(s_all[hp] - m_all[hp]) for hp in pairs}
            l_all = {hp: jnp.sum(p_all[hp], axis=-1, keepdims=True) for hp in pairs}
            o_all = {hp: jnp.dot(p_all[hp].astype(BF16), v_ref[0, pl.ds(start, nk), cols[hp]],
                                 preferred_element_type=F32) for hp in pairs}
            for hp in pairs:
                o = o_all[hp] / l_all[hp]
                o_ref[0, q_rows, cols[hp]] = jnp.where(first, o[:GRID_W], o[GRID_W:]).astype(BF16)


def _na_attention(qkv, bias, rows):
    b, s, n3 = qkv.shape
    d = n3 // 3
    assert rows >= NA_ROWS and NA_ROWS % 2 == 0 and rows % NA_ROWS_PER_STEP == 0
    q_tile = NA_ROWS_PER_STEP * GRID_W
    return pl.pallas_call(
        functools.partial(_na_kernel, rows=rows),
        grid=(b, rows // NA_ROWS_PER_STEP),
        in_specs=[
            pl.BlockSpec((1, q_tile, d), lambda i, r: (i, r, 0)),
            pl.BlockSpec((1, s, d), lambda i, r: (i, 0, 1)),
            pl.BlockSpec((1, s, d), lambda i, r: (i, 0, 2)),
            pl.BlockSpec(bias.shape, lambda i, r: (0, 0, 0, 0)),
        ],
        out_specs=pl.BlockSpec((1, q_tile, d), lambda i, r: (i, r, 0)),
        out_shape=jax.ShapeDtypeStruct((b, s, d), BF16),
        compiler_params=_cparams(("arbitrary", "arbitrary")),
        name="na_attention",
    )(qkv, qkv, qkv, bias)


def _rms_rope(a, gain, cosf, sinf, seg, seg_t):
    qq_hi, qq_lo = _split_bf16(a * a)
    ms = jnp.dot(qq_hi, seg, preferred_element_type=F32) + jnp.dot(qq_lo, seg, preferred_element_type=F32)
    r_hi, r_lo = _split_bf16(lax.rsqrt(ms + RMS_EPS))
    rf = jnp.dot(r_hi, seg_t, preferred_element_type=F32) + jnp.dot(r_lo, seg_t, preferred_element_type=F32)
    an = a * rf * gain
    n = a.shape[1]
    lane = lax.broadcasted_iota(jnp.int32, a.shape, 1)
    partner = jnp.where(lane % (2 * ROPE_PAIRS) < ROPE_PAIRS,
                        pltpu.roll(an, n - ROPE_PAIRS, 1), pltpu.roll(an, ROPE_PAIRS, 1))
    return an * cosf + partner * sinf


def _qkv_gqa_kernel(x_ref, sc_ref, sh_ref, w_ref, qg_ref, kg_ref, cos_ref, sin_ref, seg_ref, segt_ref, o_ref,
                    *, d_model):
    h = (x_ref[0] * (1.0 + sc_ref[0]) + sh_ref[0]).astype(BF16)
    n = w_ref.shape[1]
    kv_cols = (n - d_model) // 2
    cosf, sinf = cos_ref[...], sin_ref[...]
    seg, seg_t = seg_ref[...], segt_ref[...]
    for n0 in range(0, n, QKV_CHUNK):
        acc = jnp.dot(h, w_ref[:, n0:n0 + QKV_CHUNK], preferred_element_type=F32)
        if n0 < d_model:
            acc = _rms_rope(acc, qg_ref[...], cosf, sinf, seg, seg_t) * Q_SCALE
        elif n0 < d_model + kv_cols:
            acc = _rms_rope(acc, kg_ref[...], cosf, sinf, seg, seg_t)
        else:
            lane = lax.broadcasted_iota(jnp.int32, acc.shape, 1)
            acc = jnp.where(lane % HEAD_PAIR < HEAD_DIM, acc, 1.0)
        o_ref[0, :, n0:n0 + QKV_CHUNK] = acc.astype(BF16)


def _qkv_gqa(x, sc, sh, w_bf16, q_gain, k_gain, cosf, sinf, seg, seg_t):
    b, s, d = x.shape
    n = w_bf16.shape[1]
    const = lambda j, i: (0, 0)
    return pl.pallas_call(
        functools.partial(_qkv_gqa_kernel, d_model=d),
        grid=(s // ROW_TILE, b),
        in_specs=[
            pl.BlockSpec((1, ROW_TILE, d), lambda j, i: (i, j, 0)),
            pl.BlockSpec((1, 1, d), lambda j, i: (i, 0, 0)),
            pl.BlockSpec((1, 1, d), lambda j, i: (i, 0, 0)),
            pl.BlockSpec((d, n), const),
            pl.BlockSpec((1, QKV_CHUNK), const),
            pl.BlockSpec((1, QKV_CHUNK), const),
            pl.BlockSpec((ROW_TILE, QKV_CHUNK), lambda j, i: (j, 0)),
            pl.BlockSpec((ROW_TILE, QKV_CHUNK), lambda j, i: (j, 0)),
            pl.BlockSpec((QKV_CHUNK, LANES), const),
            pl.BlockSpec((LANES, QKV_CHUNK), const),
        ],
        out_specs=pl.BlockSpec((1, ROW_TILE, n), lambda j, i: (i, j, 0)),
        out_shape=jax.ShapeDtypeStruct((b, s, n), BF16),
        compiler_params=_cparams(("arbitrary", "arbitrary")),
        name="qkv_gqa",
    )(x, sc, sh, w_bf16, q_gain, k_gain, cosf, sinf, seg, seg_t)


def _gqa_kernel(bound_ref, q_ref, k_ref, v_ref, o_ref, *, use_bound):
    k2 = k_ref[0]
    v2 = v_ref[0]
    tq = q_ref.shape[1]
    lane = lax.broadcasted_iota(jnp.int32, (tq, HEAD_PAIR), 1)
    first = lane < HEAD_DIM
    for pair in range(q_ref.shape[2] // HEAD_PAIR):
        cols = slice(pair * HEAD_PAIR, (pair + 1) * HEAD_PAIR)
        q2 = q_ref[0, :, cols]
        zero = jnp.zeros_like(q2)
        qs = jnp.concatenate([jnp.where(first, q2, zero), jnp.where(first, zero, q2)], axis=0)
        s = lax.dot_general(qs, k2, (((1,), (1,)), ((), ())), preferred_element_type=F32)
        m = bound_ref[0] if use_bound else jnp.max(s, axis=-1, keepdims=True)
        p = jnp.exp2(s - m).astype(BF16)
        o = jnp.dot(p, v2, preferred_element_type=F32)
        o_a, o_b = o[:tq], o[tq:]
        out = jnp.where(first, o_a / pltpu.roll(o_a, HEAD_DIM, 1), pltpu.roll(o_b, HEAD_DIM, 1) / o_b)
        o_ref[0, :, cols] = out.astype(BF16)


def _gqa_logit_bound(q_gain, k_gain):
    return (GQA_BOUND_MARGIN * LOG2_E * (HEAD_DIM ** 0.5)
            * jnp.max(jnp.abs(q_gain)) * jnp.max(jnp.abs(k_gain))).astype(F32).reshape(1)


def _gqa_attention(qkv, d_model, q_gain, k_gain):
    bound = _gqa_logit_bound(q_gain, k_gain)
    return lax.cond(bound[0] <= GQA_BOUND_MAX,
                    functools.partial(_gqa_attention_call, d_model=d_model, use_bound=True),
                    functools.partial(_gqa_attention_call, d_model=d_model, use_bound=False),
                    bound, qkv)


def _gqa_attention_call(bound, qkv, *, d_model, use_bound):
    b, s, n = qkv.shape
    group_cols = d_model // GQA_KV_HEADS
    k_block0 = d_model // HEAD_PAIR
    v_block0 = k_block0 + GQA_KV_HEADS
    return pl.pallas_call(
        functools.partial(_gqa_kernel, use_bound=use_bound),
        grid=(b, GQA_KV_HEADS, s // GQA_TQ),
        in_specs=[
            pl.BlockSpec(memory_space=pltpu.SMEM),
            pl.BlockSpec((1, GQA_TQ, group_cols), lambda i, g, t: (i, t, g)),
            pl.BlockSpec((1, s, HEAD_PAIR), lambda i, g, t: (i, 0, k_block0 + g)),
            pl.BlockSpec((1, s, HEAD_PAIR), lambda i, g, t: (i, 0, v_block0 + g)),
        ],
        out_specs=pl.BlockSpec((1, GQA_TQ, group_cols), lambda i, g, t: (i, t, g)),
        out_shape=jax.ShapeDtypeStruct((b, s, d_model), BF16),
        compiler_params=_cparams(("arbitrary", "arbitrary", "arbitrary")),
        name="gqa_attention_bounded" if use_bound else "gqa_attention",
    )(bound, qkv, qkv, qkv)


def _store_token_tiles(ref, val, first_token=0):
    t, d = val.shape
    sub = d // LANES
    for s in range(sub):
        ref[pl.ds(first_token * sub + s, t, stride=sub), :] = val[:, s * LANES:(s + 1) * LANES]


def _load_token_tiles(ref, t, dtype):
    sub = ref.shape[0] // t
    return jnp.concatenate([ref[pl.ds(s, t, stride=sub), :].astype(dtype) for s in range(sub)], axis=1)


def _tile_copy(src_hbm, dst_vmem, src_tok, dst_tok, sub, sem):
    return pltpu.make_async_copy(src_hbm.at[pl.ds(pl.multiple_of(src_tok * sub, sub), sub)],
                                 dst_vmem.at[pl.ds(pl.multiple_of(dst_tok * sub, sub), sub)], sem)


def _layer_norm_rows(z, g, b):
    mu = jnp.mean(z, axis=-1, keepdims=True)
    zc = z - mu
    var = jnp.mean(zc * zc, axis=-1, keepdims=True)
    return zc * lax.rsqrt(var + LN_EPS) * g + b


def _router(lg):
    lane = lax.broadcasted_iota(jnp.int32, lg.shape, 1)
    lanef = lane.astype(F32)
    far = float(LANES)
    gmask = lane < N_GROUPS
    lgm = jnp.where(gmask, lg, NEG_BIG)
    mg = jnp.max(lgm, axis=-1, keepdims=True)
    sg = jnp.sum(jnp.exp(lgm - mg), axis=-1, keepdims=True)
    g_prob = 1.0 / sg
    g_idx = jnp.min(jnp.where(lgm == mg, lanef, far), axis=-1, keepdims=True)
    e_lo = N_GROUPS + g_idx * EXPERTS_PER_GROUP
    emask = (lanef >= e_lo) & (lanef < e_lo + EXPERTS_PER_GROUP)
    lem = jnp.where(emask, lg, NEG_BIG)
    me = jnp.max(lem, axis=-1, keepdims=True)
    ee = jnp.exp(lem - me)
    pe = jnp.where(emask, ee / jnp.sum(ee, axis=-1, keepdims=True), -1.0)
    p1 = jnp.max(pe, axis=-1, keepdims=True)
    i1 = jnp.min(jnp.where(pe == p1, lanef, far), axis=-1, keepdims=True)
    pe2 = jnp.where(lanef == i1, -1.0, pe)
    p2 = jnp.max(pe2, axis=-1, keepdims=True)
    i2 = jnp.min(jnp.where(pe2 == p2, lanef, far), axis=-1, keepdims=True)
    den = p1 + p2
    gate1 = g_prob * (p1 / den)
    gate2 = g_prob * (p2 / den)
    out = jnp.where(lane == 0, gate1, 0.0)
    out = jnp.where(lane == 1, gate2, out)
    out = jnp.where(lane == 2, i1 - N_GROUPS, out)
    out = jnp.where(lane == 3, i2 - N_GROUPS, out)
    return out


def _post_attn_kernel(a_ref, wo_ref, x_ref, g1_ref, sc2_ref, sh2_ref, lng_ref, lnb_ref, wr_ref, br_ref,
                      xo_ref, h2_ref, route_ref, *, alpha):
    y = jnp.dot(a_ref[0], wo_ref[...], preferred_element_type=F32)
    z = alpha * x_ref[0] + (1.0 + g1_ref[0]) * y
    xn = _layer_norm_rows(z, lng_ref[...], lnb_ref[...])
    xo_ref[0] = xn
    h2 = xn * (1.0 + sc2_ref[0]) + sh2_ref[0]
    _store_token_tiles(h2_ref, h2)
    h_hi, h_lo = _split_bf16(h2)
    t = jnp.dot(h_hi, wr_ref[...], preferred_element_type=F32)
    lg = (t[:, :LANES] + t[:, LANES:] + jnp.dot(h_lo, wr_ref[:, :LANES], preferred_element_type=F32)
          + br_ref[...])
    route_ref[0] = _router(lg)


def _post_attn(a, wo_bf16, x, g1, sc2, sh2, ln_g, ln_b, wr, br, alpha):
    b, s, d = x.shape
    sub = d // LANES
    tiles_per_batch = s // ROW_TILE
    const = lambda i, j: (0, 0)
    tile = lambda i, j: (i, j, 0)
    per_b = lambda i, j: (i, 0, 0)
    return pl.pallas_call(
        functools.partial(_post_attn_kernel, alpha=alpha),
        grid=(b, s // ROW_TILE),
        in_specs=[
            pl.BlockSpec((1, ROW_TILE, d), tile),
            pl.BlockSpec((d, d), const),
            pl.BlockSpec((1, ROW_TILE, d), tile),
            pl.BlockSpec((1, 1, d), per_b),
            pl.BlockSpec((1, 1, d), per_b),
            pl.BlockSpec((1, 1, d), per_b),
            pl.BlockSpec((1, d), const),
            pl.BlockSpec((1, d), const),
            pl.BlockSpec((d, 2 * LANES), const),
            pl.BlockSpec((1, LANES), const),
        ],
        out_specs=[
            pl.BlockSpec((1, ROW_TILE, d), tile),
            pl.BlockSpec((ROW_TILE * sub, LANES), lambda i, j: (i * tiles_per_batch + j, 0)),
            pl.BlockSpec((1, ROW_TILE, LANES), tile),
        ],
        out_shape=[
            jax.ShapeDtypeStruct((b, s, d), F32),
            jax.ShapeDtypeStruct((b * s * sub, LANES), F32),
            jax.ShapeDtypeStruct((b, s, LANES), F32),
        ],
        compiler_params=_cparams(("arbitrary", "arbitrary")),
        name="post_attn",
    )(a, wo_bf16, x, g1, sc2, sh2, ln_g, ln_b, wr, br)


def _dispatch_kernel(pos_ref, h_ref, xs_hbm, sem):
    tm = DISPATCH_TM
    sub = h_ref.shape[0] // tm

    def issue(jb, carry):
        for u in range(DMA_UNROLL):
            j = jb * DMA_UNROLL + u
            src = h_ref.at[pl.ds(pl.multiple_of(j * sub, sub), sub)]
            for k in range(TOP_K):
                row = pos_ref[0, 0, TOP_K * j + k]
                dst = xs_hbm.at[pl.ds(pl.multiple_of(row * sub, sub), sub)]
                pltpu.make_async_copy(src, dst, sem).start(priority=k % 2)
        return carry

    lax.fori_loop(0, tm // DMA_UNROLL, issue, 0)
    for _ in range(TOP_K):
        pltpu.make_async_copy(h_ref, xs_hbm.at[pl.ds(0, tm * sub)], sem).wait()


def _dispatch(pos, h2_tiles, n_tok):
    sub = h2_tiles.shape[0] // n_tok
    tm = DISPATCH_TM
    n_tiles = n_tok // tm
    return pl.pallas_call(
        _dispatch_kernel,
        grid=(n_tiles,),
        in_specs=[
            pl.BlockSpec((1, 1, TOP_K * tm), lambda i: (i, 0, 0), memory_space=pltpu.SMEM),
            pl.BlockSpec((tm * sub, LANES), lambda i: (i, 0)),
        ],
        out_specs=pl.BlockSpec(memory_space=pl.ANY),
        out_shape=jax.ShapeDtypeStruct((n_tok * TOP_K * sub, LANES), F32),
        scratch_shapes=[pltpu.SemaphoreType.DMA],
        compiler_params=_cparams(("arbitrary",)),
        name="moe_dispatch",
    )(pos.reshape(n_tiles, 1, TOP_K * tm), h2_tiles)


def _expert_kernel(blk_ref, exp_ref, lo_ref, hi_ref, x_ref, wg_ref, wu_ref, wd_ref, y_ref,
                   acc, wgb, wub, wdb, cast_expert, state):
    i = pl.program_id(0)
    n_items = pl.num_programs(0)
    tb = MOE_TB
    prev = jnp.maximum(i - 1, 0)
    nxt = jnp.minimum(i + 1, n_items - 1)
    first = (i == 0) | (blk_ref[i] != blk_ref[prev])
    last = (i == n_items - 1) | (blk_ref[i] != blk_ref[nxt])
    lo, hi = lo_ref[i], hi_ref[i]
    whole = (lo == 0) & (hi == tb)

    @pl.when(i == 0)
    def _():
        cast_expert[0] = -1

    @pl.when(first)
    def _():
        state[0] = 0
        state[1] = 0

    @pl.when(hi > lo)
    def _():
        @pl.when(cast_expert[0] != exp_ref[i])
        def _():
            wgb[...] = wg_ref[0, 0].astype(BF16)
            wub[...] = wu_ref[0, 0].astype(BF16)
            wdb[...] = wd_ref[0, 0].astype(BF16)
            cast_expert[0] = exp_ref[i]

        xb = _load_token_tiles(x_ref, tb, BF16)
        g = jnp.dot(xb, wgb[...], preferred_element_type=F32)
        u = jnp.dot(xb, wub[...], preferred_element_type=F32)
        act = (g * jax.nn.sigmoid(g) * u).astype(BF16)
        y = jnp.dot(act, wdb[...], preferred_element_type=F32)

        @pl.when(whole)
        def _():
            _store_token_tiles(y_ref, y)
            state[0] = 1

        @pl.when(jnp.logical_not(whole))
        def _():
            row = lax.broadcasted_iota(jnp.int32, (tb, 1), 0)
            part = jnp.where((row >= lo) & (row < hi), y, 0.0)

            @pl.when(state[1] == 0)
            def _():
                acc[...] = part

            @pl.when(state[1] != 0)
            def _():
                acc[...] += part

            state[1] = 1

    @pl.when(last & (state[0] == 0))
    def _():
        _store_token_tiles(y_ref, acc[...])


def _expert_mlps(items, xs_tiles, w_gate, w_up, w_down, layer):
    item_blk, item_exp, item_lo, item_hi = items
    n_items = item_blk.shape[0]
    _, _, d, de = w_gate.shape
    sub = d // LANES
    w_index = lambda i, blk, ex, lo, hi: (layer, ex[i], 0, 0)
    grid_spec = pltpu.PrefetchScalarGridSpec(
        num_scalar_prefetch=4,
        grid=(n_items,),
        in_specs=[
            pl.BlockSpec((MOE_TB * sub, LANES), lambda i, blk, ex, lo, hi: (blk[i], 0)),
            pl.BlockSpec((1, 1, d, de), w_index),
            pl.BlockSpec((1, 1, d, de), w_index),
            pl.BlockSpec((1, 1, de, d), w_index),
        ],
        out_specs=pl.BlockSpec((MOE_TB * sub, LANES), lambda i, blk, ex, lo, hi: (blk[i], 0)),
        scratch_shapes=[
            pltpu.VMEM((MOE_TB, d), F32),
            pltpu.VMEM((d, de), BF16),
            pltpu.VMEM((d, de), BF16),
            pltpu.VMEM((de, d), BF16),
            pltpu.SMEM((1,), jnp.int32),
            pltpu.SMEM((2,), jnp.int32),
        ],
    )
    return pl.pallas_call(
        _expert_kernel,
        grid_spec=grid_spec,
        out_shape=jax.ShapeDtypeStruct(xs_tiles.shape, F32),
        compiler_params=_cparams(("arbitrary",)),
        name="expert_mlps",
    )(item_blk, item_exp, item_lo, item_hi, xs_tiles, w_gate, w_up, w_down)


def _combine_kernel(pos_ref, pos_next_ref, ys_hbm, x_ref, route_ref, g2_ref, lng_ref, lnb_ref, xo_ref, buf, sems,
                    *, alpha):
    i = pl.program_id(0)
    n = pl.num_programs(0)
    tm = COMBINE_TM
    sub = buf.shape[2] // tm
    slot = i % 2

    def gather(p_ref, dst_slot):
        def issue(jb, carry):
            for u in range(DMA_UNROLL):
                j = jb * DMA_UNROLL + u
                for k in range(TOP_K):
                    _tile_copy(ys_hbm, buf.at[dst_slot, k], p_ref[0, 0, TOP_K * j + k], j, sub,
                               sems.at[dst_slot]).start(priority=k % 2)
            return carry

        lax.fori_loop(0, tm // DMA_UNROLL, issue, 0)

    @pl.when(i == 0)
    def _():
        gather(pos_ref, slot)

    @pl.when(i + 1 < n)
    def _():
        gather(pos_next_ref, 1 - slot)

    route = route_ref[...]
    lane = lax.broadcasted_iota(jnp.int32, route.shape, 1)
    gates = [jnp.sum(jnp.where(lane == k, route, 0.0), axis=-1, keepdims=True) for k in range(TOP_K)]
    for k in range(TOP_K):
        pltpu.make_async_copy(ys_hbm.at[pl.ds(0, tm * sub)], buf.at[slot, k], sems.at[slot]).wait()
    y = _load_token_tiles(buf.at[slot, 0], tm, F32) * gates[0]
    for k in range(1, TOP_K):
        y = y + _load_token_tiles(buf.at[slot, k], tm, F32) * gates[k]
    z = alpha * x_ref[...] + (1.0 + g2_ref[0]) * y
    xo_ref[...] = _layer_norm_rows(z, lng_ref[...], lnb_ref[...])


def _combine(pos, ys, x2d, route2d, g2, ln_g, ln_b, seq, alpha):
    n_tok, d = x2d.shape
    tm = COMBINE_TM
    n_tiles = n_tok // tm
    tiles_per_batch = seq // tm
    const = lambda i: (0, 0)
    pos3 = pos.reshape(n_tiles, 1, TOP_K * tm)
    return pl.pallas_call(
        functools.partial(_combine_kernel, alpha=alpha),
        grid=(n_tiles,),
        in_specs=[
            pl.BlockSpec((1, 1, TOP_K * tm), lambda i: (i, 0, 0), memory_space=pltpu.SMEM),
            pl.BlockSpec((1, 1, TOP_K * tm), lambda i: (jnp.minimum(i + 1, n_tiles - 1), 0, 0),
                         memory_space=pltpu.SMEM),
            pl.BlockSpec(memory_space=pl.ANY),
            pl.BlockSpec((tm, d), lambda i: (i, 0)),
            pl.BlockSpec((tm, LANES), lambda i: (i, 0)),
            pl.BlockSpec((1, 1, d), lambda i: (i // tiles_per_batch, 0, 0)),
            pl.BlockSpec((1, d), const),
            pl.BlockSpec((1, d), const),
        ],
        out_specs=pl.BlockSpec((tm, d), lambda i: (i, 0)),
        out_shape=jax.ShapeDtypeStruct((n_tok, d), F32),
        scratch_shapes=[
            pltpu.VMEM((2, TOP_K, tm * (d // LANES), LANES), F32),
            pltpu.SemaphoreType.DMA((2,)),
        ],
        compiler_params=_cparams(("arbitrary",)),
        name="moe_combine",
    )(pos3, pos3, ys, x2d, route2d, g2, ln_g, ln_b)


def _route_pos_kernel(route_ref, pos_ref, cnt_ref, counts, base):
    p = pl.program_id(0)
    j = pl.program_id(1)
    route = route_ref[...]
    t = route.shape[0]
    lane = lax.broadcasted_iota(jnp.int32, route.shape, 1)
    lanef = lane.astype(F32)
    e1 = jnp.sum(jnp.where(lane == 2, route, 0.0), axis=-1, keepdims=True)
    e2 = jnp.sum(jnp.where(lane == 3, route, 0.0), axis=-1, keepdims=True)
    m1 = lanef == e1
    m2 = lanef == e2
    mem = jnp.where(m1 | m2, 1.0, 0.0)
    colsum = jnp.sum(mem, axis=0, keepdims=True)

    @pl.when((p == 0) & (j == 0))
    def _():
        counts[...] = jnp.zeros_like(counts)

    @pl.when(p == 0)
    def _():
        counts[...] = counts[...] + colsum

    @pl.when((p == 1) & (j == 0))
    def _():
        c = counts[...]
        cnt_ref[...] = c
        c_hi = jnp.floor(c * (1.0 / 256.0))
        c_lo = c - c_hi * 256.0
        r_i = lax.broadcasted_iota(jnp.int32, (LANES, LANES), 0)
        c_i = lax.broadcasted_iota(jnp.int32, (LANES, LANES), 1)
        excl = jnp.where(r_i < c_i, 1.0, 0.0).astype(BF16)
        base[...] = (jnp.dot(c_hi.astype(BF16), excl, preferred_element_type=F32) * 256.0
                     + jnp.dot(c_lo.astype(BF16), excl, preferred_element_type=F32))

    @pl.when(p == 1)
    def _():
        r_i = lax.broadcasted_iota(jnp.int32, (t, t), 0)
        c_i = lax.broadcasted_iota(jnp.int32, (t, t), 1)
        before = jnp.where(c_i < r_i, 1.0, 0.0).astype(BF16)
        rank = base[0:1, :] + jnp.dot(before, mem.astype(BF16), preferred_element_type=F32)
        pos1 = jnp.sum(jnp.where(m1, rank, 0.0), axis=-1, keepdims=True)
        pos2 = jnp.sum(jnp.where(m2, rank, 0.0), axis=-1, keepdims=True)
        pos_ref[...] = jnp.where(lane == 0, pos1, jnp.where(lane == 1, pos2, 0.0))
        base[...] = base[...] + colsum


def _routing_tables(route2d):
    n_tok = route2d.shape[0]
    n_tiles = n_tok // ROUTE_T
    posf, cnt = pl.pallas_call(
        _route_pos_kernel,
        grid=(2, n_tiles),
        in_specs=[pl.BlockSpec((ROUTE_T, LANES), lambda p, j: (j, 0))],
        out_specs=[
            pl.BlockSpec((ROUTE_T, LANES), lambda p, j: (p * j, 0)),
            pl.BlockSpec((8, LANES), lambda p, j: (0, 0)),
        ],
        out_shape=[
            jax.ShapeDtypeStruct((n_tok, LANES), F32),
            jax.ShapeDtypeStruct((8, LANES), F32),
        ],
        scratch_shapes=[pltpu.VMEM((8, LANES), F32), pltpu.VMEM((8, LANES), F32)],
        compiler_params=_cparams(("arbitrary", "arbitrary")),
        name="route_positions",
    )(route2d)
    n_rows = n_tok * TOP_K
    pos = posf[:, :TOP_K].astype(jnp.int32).reshape(n_rows)
    counts = cnt[0, :N_EXPERTS].astype(jnp.int32)
    seg_end = jnp.cumsum(counts)
    cuts = jnp.concatenate([jnp.arange(n_rows // MOE_TB, dtype=jnp.int32) * MOE_TB, seg_end - counts])
    n_items = cuts.shape[0]
    idx = jnp.arange(n_items, dtype=jnp.int32)
    rank = (jnp.sum(cuts[None, :] < cuts[:, None], axis=1)
            + jnp.sum((cuts[None, :] == cuts[:, None]) & (idx[None, :] < idx[:, None]), axis=1))
    lo_abs = jnp.sum(jnp.where(rank[None, :] == idx[:, None], cuts[None, :], 0), axis=1)
    hi_abs = jnp.concatenate([lo_abs[1:], jnp.full((1,), n_rows, jnp.int32)])
    item_blk = jnp.minimum(lo_abs // MOE_TB, n_rows // MOE_TB - 1)
    item_exp = jnp.minimum(jnp.sum(seg_end[None, :] <= lo_abs[:, None], axis=1), N_EXPERTS - 1)
    item_lo = lo_abs - item_blk * MOE_TB
    item_hi = hi_abs - item_blk * MOE_TB
    items = tuple(a.astype(jnp.int32) for a in (item_blk, item_exp, item_lo, item_hi))
    return pos, items


def _rope_tables(seq, width):
    t = jnp.arange(seq, dtype=jnp.int32)
    pos = jnp.stack([t // GRID_W, t % GRID_W], axis=-1).astype(F32)
    inv_freq = ROPE_THETA ** (-jnp.arange(ROPE_PAIRS, dtype=F32) / ROPE_PAIRS)
    ang = pos[:, :, None] * inv_freq
    cos, sin = jnp.cos(ang), jnp.sin(ang)
    cosf = jnp.stack([cos, cos], axis=2).reshape(seq, HEAD_DIM)
    sinf = jnp.stack([-sin, sin], axis=2).reshape(seq, HEAD_DIM)
    reps = width // HEAD_DIM
    return jnp.tile(cosf, (1, reps)), jnp.tile(sinf, (1, reps))


def _dup_heads(w):
    d = w.shape[0]
    w4 = w.reshape(d, GQA_KV_HEADS, 1, HEAD_DIM)
    return jnp.concatenate([w4, w4], axis=2).reshape(d, GQA_KV_HEADS * HEAD_PAIR)


def kernel(x, c, ada_w, ada_b, ln_g, ln_b, na_w_qkv, na_rpb, na_w_o, gqa_w_qkv, gqa_q_norm, gqa_k_norm, gqa_w_o,
           moe_w_group, moe_b_group, moe_w_expert, moe_b_expert, moe_w_gate, moe_w_up, moe_w_down):
    b, s, d = x.shape
    depth = ada_w.shape[0]
    n_tok = b * s
    rows = s // GRID_W
    kvd = GQA_KV_HEADS * HEAD_DIM
    alpha = (2 * depth) ** 0.25

    mod = _ada_modulation(c, ada_w, ada_b).reshape(depth, b, 6, 1, d)

    cosf, sinf = _rope_tables(s, QKV_CHUNK)
    head_of_lane = jnp.arange(QKV_CHUNK, dtype=jnp.int32) // HEAD_DIM
    seg_t = (head_of_lane[None, :] == jnp.arange(LANES, dtype=jnp.int32)[:, None])
    seg = (seg_t.T.astype(F32) / HEAD_DIM).astype(BF16)
    seg_t = seg_t.astype(BF16)

    for i in range(depth):
        sh1, sc1, g1, sh2, sc2, g2 = (mod[i, :, k] for k in range(6))
        j = i // 2
        if i % 2 == 0:
            qkv = _qkv_na(x, sc1, sh1, na_w_qkv[j].astype(BF16))
            attn = _na_attention(qkv, _na_bias_table(na_rpb[j]), rows)
            w_o = na_w_o[j]
        else:
            w = gqa_w_qkv[j]
            w_all = jnp.concatenate([w[:, :d], _dup_heads(w[:, d:d + kvd]), _dup_heads(w[:, d + kvd:])], axis=1)
            reps = QKV_CHUNK // HEAD_DIM
            qkv = _qkv_gqa(x, sc1, sh1, w_all.astype(BF16),
                           jnp.tile(gqa_q_norm[j], reps)[None, :], jnp.tile(gqa_k_norm[j], reps)[None, :],
                           cosf, sinf, seg, seg_t)
            attn = _gqa_attention(qkv, d, gqa_q_norm[j], gqa_k_norm[j])
            w_o = gqa_w_o[j]

        w_r = jnp.concatenate([moe_w_group[i], moe_w_expert[i]], axis=1)
        w_r = jnp.pad(w_r, ((0, 0), (0, LANES - w_r.shape[1])))
        wr_hi, wr_lo = _split_bf16(w_r)
        b_r = jnp.concatenate([moe_b_group[i], moe_b_expert[i]])
        b_r = jnp.pad(b_r, (0, LANES - b_r.shape[0]))[None, :]
        x, h2, route = _post_attn(attn, w_o.astype(BF16), x, g1, sc2, sh2, ln_g[i, 0][None, :], ln_b[i, 0][None, :],
                                  jnp.concatenate([wr_hi, wr_lo], axis=1), b_r, alpha)

        route2d = route.reshape(n_tok, LANES)
        pos, items = _routing_tables(route2d)
        xs = _dispatch(pos, h2, n_tok)
        ys = _expert_mlps(items, xs, moe_w_gate, moe_w_up, moe_w_down, i)
        x = _combine(pos, ys, x.reshape(n_tok, d), route2d, g2, ln_g[i, 1][None, :], ln_b[i, 1][None, :],
                     s, alpha).reshape(b, s, d)
    return x
```

```python
import functools

import jax
import jax.numpy as jnp
from jax import lax
from jax.experimental import pallas as pl
from jax.experimental.pallas import tpu as pltpu

GRID_W = 64
N_HEADS = 16
HEAD_DIM = 64
NA_ROWS = 8
NA_COLS = 16
GQA_KV_HEADS = 4
ROPE_THETA = 10000.0
ROPE_PAIRS = HEAD_DIM // 4
N_GROUPS = 4
EXPERTS_PER_GROUP = 8
N_EXPERTS = N_GROUPS * EXPERTS_PER_GROUP
TOP_K = 2
LN_EPS = 1e-5
RMS_EPS = 1e-6

LANES = 128
HEAD_PAIR = 2 * HEAD_DIM
VMEM_LIMIT_BYTES = 52 * 1024 * 1024
ADA_TN = 1536
ROW_TILE = 512
QKV_CHUNK = 512
GQA_TQ = 512
NA_ROWS_PER_STEP = 2
NA_STAGE_PAIRS = 4
MOE_TB = 256
DISPATCH_TM = 1024
COMBINE_TM = 1024
ROUTE_T = 512
DMA_UNROLL = 8
NEG_BIG = -1e30
LOG2_E = 1.4426950408889634
Q_SCALE = (HEAD_DIM ** -0.5) * LOG2_E
GQA_BOUND_MARGIN = 1.02
GQA_BOUND_MAX = 50.0

F32 = jnp.float32
BF16 = jnp.bfloat16


def _cparams(sem):
    return pltpu.CompilerParams(dimension_semantics=sem, vmem_limit_bytes=VMEM_LIMIT_BYTES)


def _split_bf16(a):
    hi = a.astype(BF16)
    lo = (a - hi.astype(F32)).astype(BF16)
    return hi, lo


def _ada_kernel(c_ref, w_ref, b_ref, o_ref):
    c = c_ref[...]
    ca = (c * jax.nn.sigmoid(c)).astype(BF16)
    o_ref[0] = jnp.dot(ca, w_ref[0].astype(BF16), preferred_element_type=F32) + b_ref[0]


def _ada_modulation(c, ada_w, ada_b):
    depth, d, n = ada_w.shape
    b = c.shape[0]
    return pl.pallas_call(
        _ada_kernel,
        grid=(depth, n // ADA_TN),
        in_specs=[
            pl.BlockSpec((b, d), lambda i, j: (0, 0)),
            pl.BlockSpec((1, d, ADA_TN), lambda i, j: (i, 0, j)),
            pl.BlockSpec((1, 1, ADA_TN), lambda i, j: (i, 0, j)),
        ],
        out_specs=pl.BlockSpec((1, b, ADA_TN), lambda i, j: (i, 0, j)),
        out_shape=jax.ShapeDtypeStruct((depth, b, n), F32),
        compiler_params=_cparams(("arbitrary", "arbitrary")),
        name="ada_modulation",
    )(c, ada_w, ada_b.reshape(depth, 1, n))


def _project_na(h, w_ref, o_ref, d_model):
    n = w_ref.shape[1]
    for n0 in range(0, n, QKV_CHUNK):
        acc = jnp.dot(h, w_ref[:, n0:n0 + QKV_CHUNK], preferred_element_type=F32)
        if n0 < d_model:
            acc = acc * Q_SCALE
        o_ref[:, n0:n0 + QKV_CHUNK] = acc.astype(BF16)


def _qkv_na_kernel(x_ref, sc_ref, sh_ref, w_ref, o_ref, *, d_model):
    h = (x_ref[0] * (1.0 + sc_ref[0]) + sh_ref[0]).astype(BF16)
    _project_na(h, w_ref, o_ref.at[0], d_model)


def _qkv_na(x, sc, sh, w_bf16):
    b, s, d = x.shape
    n = w_bf16.shape[1]
    return pl.pallas_call(
        functools.partial(_qkv_na_kernel, d_model=d),
        grid=(b, s // ROW_TILE),
        in_specs=[
            pl.BlockSpec((1, ROW_TILE, d), lambda i, j: (i, j, 0)),
            pl.BlockSpec((1, 1, d), lambda i, j: (i, 0, 0)),
            pl.BlockSpec((1, 1, d), lambda i, j: (i, 0, 0)),
            pl.BlockSpec((d, n), lambda i, j: (0, 0)),
        ],
        out_specs=pl.BlockSpec((1, ROW_TILE, n), lambda i, j: (i, j, 0)),
        out_shape=jax.ShapeDtypeStruct((b, s, n), BF16),
        compiler_params=_cparams(("arbitrary", "arbitrary")),
        name="qkv_na",
    )(x, sc, sh, w_bf16)


N_DR = 2 * NA_ROWS - 1
N_DC = 2 * NA_COLS - 1


def _na_bias_kernel(r_ref, m_ref, valid_ref, o_ref):
    r = r_ref[...]
    hi = r.astype(BF16)
    r1 = r - hi.astype(F32)
    mid = r1.astype(BF16)
    lo = (r1 - mid.astype(F32)).astype(BF16)
    m = m_ref[...]
    acc = (jnp.dot(hi, m, preferred_element_type=F32) + jnp.dot(mid, m, preferred_element_type=F32)
           + jnp.dot(lo, m, preferred_element_type=F32))
    o_ref[...] = jnp.where(valid_ref[...] > 0.0, acc * LOG2_E, NEG_BIG)


def _na_bias_table(rpb):
    qc = jnp.arange(GRID_W, dtype=jnp.int32)
    kc = jnp.arange(GRID_W, dtype=jnp.int32)
    win_start = jnp.clip(qc - NA_COLS // 2, 0, GRID_W - NA_COLS)
    valid = (kc[None, :] >= win_start[:, None]) & (kc[None, :] < win_start[:, None] + NA_COLS)
    dc = jnp.clip(kc[None, :] - qc[:, None] + NA_COLS - 1, 0, N_DC - 1)
    onehot = (jnp.arange(LANES, dtype=jnp.int32)[:, None] == dc.reshape(1, GRID_W * GRID_W)).astype(BF16)
    r2d = jnp.pad(rpb.reshape(N_HEADS * N_DR, N_DC).astype(F32), ((0, 0), (0, LANES - N_DC)))
    tiles = pl.pallas_call(
        _na_bias_kernel,
        out_shape=jax.ShapeDtypeStruct((N_HEADS * N_DR, GRID_W * GRID_W), F32),
        compiler_params=pltpu.CompilerParams(vmem_limit_bytes=VMEM_LIMIT_BYTES),
        name="na_bias_tiles",
    )(r2d, onehot, valid.reshape(1, GRID_W * GRID_W).astype(F32))
    tiles = tiles.reshape(N_HEADS, N_DR, GRID_W, GRID_W)
    pairs = jnp.concatenate([tiles[:, :N_DR - 1], tiles[:, 1:]], axis=-1)
    pairs = pairs.reshape(N_HEADS // 2, 2, N_DR - 1, GRID_W, HEAD_PAIR).transpose(0, 2, 1, 3, 4)
    return pairs.reshape(N_HEADS // 2, N_DR - 1, 2 * GRID_W, HEAD_PAIR)


def _na_kernel(q_ref, k_ref, v_ref, bias_ref, o_ref, *, rows):
    nk = NA_ROWS * GRID_W
    lane = lax.broadcasted_iota(jnp.int32, (GRID_W, HEAD_PAIR), 1)
    first = lane < HEAD_DIM
    for sub_row in range(NA_ROWS_PER_STEP):
        r = pl.program_id(1) * NA_ROWS_PER_STEP + sub_row
        row_start = jnp.clip(r - NA_ROWS // 2, 0, rows - NA_ROWS)
        start = pl.multiple_of(row_start * GRID_W, GRID_W)
        dr_base = row_start - r + (NA_ROWS - 1)
        q_rows = slice(sub_row * GRID_W, (sub_row + 1) * GRID_W)
        for hp0 in range(0, N_HEADS // 2, NA_STAGE_PAIRS):
            pairs = range(hp0, hp0 + NA_STAGE_PAIRS)
            cols = {hp: slice(hp * HEAD_PAIR, (hp + 1) * HEAD_PAIR) for hp in pairs}
            s_all = {}
            for hp in pairs:
                q2 = q_ref[0, q_rows, cols[hp]]
                k2 = k_ref[0, pl.ds(start, nk), cols[hp]]
                zero = jnp.zeros_like(q2)
                qs = jnp.concatenate([jnp.where(first, q2, zero), jnp.where(first, zero, q2)], axis=0)
                s = lax.dot_general(qs, k2, (((1,), (1,)), ((), ())), preferred_element_type=F32)
                s_all[hp] = s + jnp.concatenate([bias_ref[hp, dr_base + 2 * m] for m in range(NA_ROWS // 2)],
                                                axis=1)
            m_all = {hp: jnp.max(s_all[hp], axis=-1, keepdims=True) for hp in pairs}
            p_all = {hp: jnp.exp2(s_all[hp] - m_all[hp]) for hp in pairs}
            l_all = {hp: jnp.sum(p_all[hp], axis=-1, keepdims=True) for hp in pairs}
            o_all = {hp: jnp.dot(p_all[hp].astype(BF16), v_ref[0, pl.ds(start, nk), cols[hp]],
                                 preferred_element_type=F32) for hp in pairs}
            for hp in pairs:
                o = o_all[hp] / l_all[hp]
                o_ref[0, q_rows, cols[hp]] = jnp.where(first, o[:GRID_W], o[GRID_W:]).astype(BF16)


def _na_attention(qkv, bias, rows):
    b, s, n3 = qkv.shape
    d = n3 // 3
    assert rows >= NA_ROWS and NA_ROWS % 2 == 0 and rows % NA_ROWS_PER_STEP == 0
    q_tile = NA_ROWS_PER_STEP * GRID_W
    return pl.pallas_call(
        functools.partial(_na_kernel, rows=rows),
        grid=(b, rows // NA_ROWS_PER_STEP),
        in_specs=[
            pl.BlockSpec((1, q_tile, d), lambda i, r: (i, r, 0)),
            pl.BlockSpec((1, s, d), lambda i, r: (i, 0, 1)),
            pl.BlockSpec((1, s, d), lambda i, r: (i, 0, 2)),
            pl.BlockSpec(bias.shape, lambda i, r: (0, 0, 0, 0)),
        ],
        out_specs=pl.BlockSpec((1, q_tile, d), lambda i, r: (i, r, 0)),
        out_shape=jax.ShapeDtypeStruct((b, s, d), BF16),
        compiler_params=_cparams(("arbitrary", "arbitrary")),
        name="na_attention",
    )(qkv, qkv, qkv, bias)


def _rms_rope(a, gain, cosf, sinf, seg, seg_t):
    qq_hi, qq_lo = _split_bf16(a * a)
    ms = jnp.dot(qq_hi, seg, preferred_element_type=F32) + jnp.dot(qq_lo, seg, preferred_element_type=F32)
    r_hi, r_lo = _split_bf16(lax.rsqrt(ms + RMS_EPS))
    rf = jnp.dot(r_hi, seg_t, preferred_element_type=F32) + jnp.dot(r_lo, seg_t, preferred_element_type=F32)
    an = a * rf * gain
    n = a.shape[1]
    lane = lax.broadcasted_iota(jnp.int32, a.shape, 1)
    partner = jnp.where(lane % (2 * ROPE_PAIRS) < ROPE_PAIRS,
                        pltpu.roll(an, n - ROPE_PAIRS, 1), pltpu.roll(an, ROPE_PAIRS, 1))
    return an * cosf + partner * sinf


def _project_gqa(h, w_ref, qg_ref, kg_ref, cos_ref, sin_ref, seg_ref, segt_ref, o_ref, d_model):
    n = w_ref.shape[1]
    kv_cols = (n - d_model) // 2
    cosf, sinf = cos_ref[...], sin_ref[...]
    seg, seg_t = seg_ref[...], segt_ref[...]
    for n0 in range(0, n, QKV_CHUNK):
        acc = jnp.dot(h, w_ref[:, n0:n0 + QKV_CHUNK], preferred_element_type=F32)
        if n0 < d_model:
            acc = _rms_rope(acc, qg_ref[...], cosf, sinf, seg, seg_t) * Q_SCALE
        elif n0 < d_model + kv_cols:
            acc = _rms_rope(acc, kg_ref[...], cosf, sinf, seg, seg_t)
        else:
            lane = lax.broadcasted_iota(jnp.int32, acc.shape, 1)
            acc = jnp.where(lane % HEAD_PAIR < HEAD_DIM, acc, 1.0)
        o_ref[:, n0:n0 + QKV_CHUNK] = acc.astype(BF16)


def _qkv_gqa_kernel(x_ref, sc_ref, sh_ref, w_ref, qg_ref, kg_ref, cos_ref, sin_ref, seg_ref, segt_ref, o_ref,
                    *, d_model):
    h = (x_ref[0] * (1.0 + sc_ref[0]) + sh_ref[0]).astype(BF16)
    _project_gqa(h, w_ref, qg_ref, kg_ref, cos_ref, sin_ref, seg_ref, segt_ref, o_ref.at[0], d_model)


def _qkv_gqa(x, sc, sh, w_bf16, q_gain, k_gain, cosf, sinf, seg, seg_t):
    b, s, d = x.shape
    n = w_bf16.shape[1]
    const = lambda j, i: (0, 0)
    return pl.pallas_call(
        functools.partial(_qkv_gqa_kernel, d_model=d),
        grid=(s // ROW_TILE, b),
        in_specs=[
            pl.BlockSpec((1, ROW_TILE, d), lambda j, i: (i, j, 0)),
            pl.BlockSpec((1, 1, d), lambda j, i: (i, 0, 0)),
            pl.BlockSpec((1, 1, d), lambda j, i: (i, 0, 0)),
            pl.BlockSpec((d, n), const),
            pl.BlockSpec((1, QKV_CHUNK), const),
            pl.BlockSpec((1, QKV_CHUNK), const),
            pl.BlockSpec((ROW_TILE, QKV_CHUNK), lambda j, i: (j, 0)),
            pl.BlockSpec((ROW_TILE, QKV_CHUNK), lambda j, i: (j, 0)),
            pl.BlockSpec((QKV_CHUNK, LANES), const),
            pl.BlockSpec((LANES, QKV_CHUNK), const),
        ],
        out_specs=pl.BlockSpec((1, ROW_TILE, n), lambda j, i: (i, j, 0)),
        out_shape=jax.ShapeDtypeStruct((b, s, n), BF16),
        compiler_params=_cparams(("arbitrary", "arbitrary")),
        name="qkv_gqa",
    )(x, sc, sh, w_bf16, q_gain, k_gain, cosf, sinf, seg, seg_t)


def _gqa_kernel(bound_ref, q_ref, k_ref, v_ref, o_ref, *, use_bound):
    k2 = k_ref[0]
    v2 = v_ref[0]
    tq = q_ref.shape[1]
    lane = lax.broadcasted_iota(jnp.int32, (tq, HEAD_PAIR), 1)
    first = lane < HEAD_DIM
    for pair in range(q_ref.shape[2] // HEAD_PAIR):
        cols = slice(pair * HEAD_PAIR, (pair + 1) * HEAD_PAIR)
        q2 = q_ref[0, :, cols]
        zero = jnp.zeros_like(q2)
        qs = jnp.concatenate([jnp.where(first, q2, zero), jnp.where(first, zero, q2)], axis=0)
        s = lax.dot_general(qs, k2, (((1,), (1,)), ((), ())), preferred_element_type=F32)
        m = bound_ref[0] if use_bound else jnp.max(s, axis=-1, keepdims=True)
        p = jnp.exp2(s - m).astype(BF16)
        o = jnp.dot(p, v2, preferred_element_type=F32)
        o_a, o_b = o[:tq], o[tq:]
        out = jnp.where(first, o_a / pltpu.roll(o_a, HEAD_DIM, 1), pltpu.roll(o_b, HEAD_DIM, 1) / o_b)
        o_ref[0, :, cols] = out.astype(BF16)


def _gqa_logit_bound(q_gain, k_gain):
    return (GQA_BOUND_MARGIN * LOG2_E * (HEAD_DIM ** 0.5)
            * jnp.max(jnp.abs(q_gain)) * jnp.max(jnp.abs(k_gain))).astype(F32).reshape(1)


def _gqa_attention(qkv, d_model, q_gain, k_gain):
    bound = _gqa_logit_bound(q_gain, k_gain)
    return lax.cond(bound[0] <= GQA_BOUND_MAX,
                    functools.partial(_gqa_attention_call, d_model=d_model, use_bound=True),
                    functools.partial(_gqa_attention_call, d_model=d_model, use_bound=False),
                    bound, qkv)


def _gqa_attention_call(bound, qkv, *, d_model, use_bound):
    b, s, n = qkv.shape
    group_cols = d_model // GQA_KV_HEADS
    k_block0 = d_model // HEAD_PAIR
    v_block0 = k_block0 + GQA_KV_HEADS
    return pl.pallas_call(
        functools.partial(_gqa_kernel, use_bound=use_bound),
        grid=(b, GQA_KV_HEADS, s // GQA_TQ),
        in_specs=[
            pl.BlockSpec(memory_space=pltpu.SMEM),
            pl.BlockSpec((1, GQA_TQ, group_cols), lambda i, g, t: (i, t, g)),
            pl.BlockSpec((1, s, HEAD_PAIR), lambda i, g, t: (i, 0, k_block0 + g)),
            pl.BlockSpec((1, s, HEAD_PAIR), lambda i, g, t: (i, 0, v_block0 + g)),
        ],
        out_specs=pl.BlockSpec((1, GQA_TQ, group_cols), lambda i, g, t: (i, t, g)),
        out_shape=jax.ShapeDtypeStruct((b, s, d_model), BF16),
        compiler_params=_cparams(("arbitrary", "arbitrary", "arbitrary")),
        name="gqa_attention_bounded" if use_bound else "gqa_attention",
    )(bound, qkv, qkv, qkv)


def _store_token_tiles(ref, val, first_token=0):
    t, d = val.shape
    sub = d // LANES
    for s in range(sub):
        ref[pl.ds(first_token * sub + s, t, stride=sub), :] = val[:, s * LANES:(s + 1) * LANES]


def _load_token_tiles(ref, t, dtype):
    sub = ref.shape[0] // t
    return jnp.concatenate([ref[pl.ds(s, t, stride=sub), :].astype(dtype) for s in range(sub)], axis=1)


def _tile_copy(src_hbm, dst_vmem, src_tok, dst_tok, sub, sem):
    return pltpu.make_async_copy(src_hbm.at[pl.ds(pl.multiple_of(src_tok * sub, sub), sub)],
                                 dst_vmem.at[pl.ds(pl.multiple_of(dst_tok * sub, sub), sub)], sem)


def _layer_norm_rows(z, g, b):
    mu = jnp.mean(z, axis=-1, keepdims=True)
    zc = z - mu
    var = jnp.mean(zc * zc, axis=-1, keepdims=True)
    return zc * lax.rsqrt(var + LN_EPS) * g + b


def _router(lg):
    lane = lax.broadcasted_iota(jnp.int32, lg.shape, 1)
    lanef = lane.astype(F32)
    far = float(LANES)
    gmask = lane < N_GROUPS
    lgm = jnp.where(gmask, lg, NEG_BIG)
    mg = jnp.max(lgm, axis=-1, keepdims=True)
    sg = jnp.sum(jnp.exp(lgm - mg), axis=-1, keepdims=True)
    g_prob = 1.0 / sg
    g_idx = jnp.min(jnp.where(lgm == mg, lanef, far), axis=-1, keepdims=True)
    e_lo = N_GROUPS + g_idx * EXPERTS_PER_GROUP
    emask = (lanef >= e_lo) & (lanef < e_lo + EXPERTS_PER_GROUP)
    lem = jnp.where(emask, lg, NEG_BIG)
    me = jnp.max(lem, axis=-1, keepdims=True)
    ee = jnp.exp(lem - me)
    pe = jnp.where(emask, ee / jnp.sum(ee, axis=-1, keepdims=True), -1.0)
    p1 = jnp.max(pe, axis=-1, keepdims=True)
    i1 = jnp.min(jnp.where(pe == p1, lanef, far), axis=-1, keepdims=True)
    pe2 = jnp.where(lanef == i1, -1.0, pe)
    p2 = jnp.max(pe2, axis=-1, keepdims=True)
    i2 = jnp.min(jnp.where(pe2 == p2, lanef, far), axis=-1, keepdims=True)
    den = p1 + p2
    gate1 = g_prob * (p1 / den)
    gate2 = g_prob * (p2 / den)
    out = jnp.where(lane == 0, gate1, 0.0)
    out = jnp.where(lane == 1, gate2, out)
    out = jnp.where(lane == 2, i1 - N_GROUPS, out)
    out = jnp.where(lane == 3, i2 - N_GROUPS, out)
    return out


def _post_attn_kernel(a_ref, wo_ref, x_ref, g1_ref, sc2_ref, sh2_ref, lng_ref, lnb_ref, wr_ref, br_ref,
                      xo_ref, h2_ref, route_ref, *, alpha):
    y = jnp.dot(a_ref[0], wo_ref[...], preferred_element_type=F32)
    z = alpha * x_ref[0] + (1.0 + g1_ref[0]) * y
    xn = _layer_norm_rows(z, lng_ref[...], lnb_ref[...])
    xo_ref[0] = xn
    h2 = xn * (1.0 + sc2_ref[0]) + sh2_ref[0]
    _store_token_tiles(h2_ref, h2)
    h_hi, h_lo = _split_bf16(h2)
    t = jnp.dot(h_hi, wr_ref[...], preferred_element_type=F32)
    lg = (t[:, :LANES] + t[:, LANES:] + jnp.dot(h_lo, wr_ref[:, :LANES], preferred_element_type=F32)
          + br_ref[...])
    route_ref[0] = _router(lg)


def _post_attn(a, wo_bf16, x, g1, sc2, sh2, ln_g, ln_b, wr, br, alpha):
    b, s, d = x.shape
    sub = d // LANES
    tiles_per_batch = s // ROW_TILE
    const = lambda i, j: (0, 0)
    tile = lambda i, j: (i, j, 0)
    per_b = lambda i, j: (i, 0, 0)
    return pl.pallas_call(
        functools.partial(_post_attn_kernel, alpha=alpha),
        grid=(b, s // ROW_TILE),
        in_specs=[
            pl.BlockSpec((1, ROW_TILE, d), tile),
            pl.BlockSpec((d, d), const),
            pl.BlockSpec((1, ROW_TILE, d), tile),
            pl.BlockSpec((1, 1, d), per_b),
            pl.BlockSpec((1, 1, d), per_b),
            pl.BlockSpec((1, 1, d), per_b),
            pl.BlockSpec((1, d), const),
            pl.BlockSpec((1, d), const),
            pl.BlockSpec((d, 2 * LANES), const),
            pl.BlockSpec((1, LANES), const),
        ],
        out_specs=[
            pl.BlockSpec((1, ROW_TILE, d), tile),
            pl.BlockSpec((ROW_TILE * sub, LANES), lambda i, j: (i * tiles_per_batch + j, 0)),
            pl.BlockSpec((1, ROW_TILE, LANES), tile),
        ],
        out_shape=[
            jax.ShapeDtypeStruct((b, s, d), F32),
            jax.ShapeDtypeStruct((b * s * sub, LANES), F32),
            jax.ShapeDtypeStruct((b, s, LANES), F32),
        ],
        compiler_params=_cparams(("arbitrary", "arbitrary")),
        name="post_attn",
    )(a, wo_bf16, x, g1, sc2, sh2, ln_g, ln_b, wr, br)


def _dispatch_kernel(pos_ref, h_ref, xs_hbm, sem):
    tm = DISPATCH_TM
    sub = h_ref.shape[0] // tm

    def issue(jb, carry):
        for u in range(DMA_UNROLL):
            j = jb * DMA_UNROLL + u
            src = h_ref.at[pl.ds(pl.multiple_of(j * sub, sub), sub)]
            for k in range(TOP_K):
                row = pos_ref[0, 0, TOP_K * j + k]
                dst = xs_hbm.at[pl.ds(pl.multiple_of(row * sub, sub), sub)]
                pltpu.make_async_copy(src, dst, sem).start(priority=k % 2)
        return carry

    lax.fori_loop(0, tm // DMA_UNROLL, issue, 0)
    for _ in range(TOP_K):
        pltpu.make_async_copy(h_ref, xs_hbm.at[pl.ds(0, tm * sub)], sem).wait()


def _dispatch(pos, h2_tiles, n_tok):
    sub = h2_tiles.shape[0] // n_tok
    tm = DISPATCH_TM
    n_tiles = n_tok // tm
    return pl.pallas_call(
        _dispatch_kernel,
        grid=(n_tiles,),
        in_specs=[
            pl.BlockSpec((1, 1, TOP_K * tm), lambda i: (i, 0, 0), memory_space=pltpu.SMEM),
            pl.BlockSpec((tm * sub, LANES), lambda i: (i, 0)),
        ],
        out_specs=pl.BlockSpec(memory_space=pl.ANY),
        out_shape=jax.ShapeDtypeStruct((n_tok * TOP_K * sub, LANES), F32),
        scratch_shapes=[pltpu.SemaphoreType.DMA],
        compiler_params=_cparams(("arbitrary",)),
        name="moe_dispatch",
    )(pos.reshape(n_tiles, 1, TOP_K * tm), h2_tiles)


def _expert_kernel(blk_ref, exp_ref, lo_ref, hi_ref, x_ref, wg_ref, wu_ref, wd_ref, y_ref,
                   acc, wgb, wub, wdb, cast_expert, state):
    i = pl.program_id(0)
    n_items = pl.num_programs(0)
    tb = MOE_TB
    prev = jnp.maximum(i - 1, 0)
    nxt = jnp.minimum(i + 1, n_items - 1)
    first = (i == 0) | (blk_ref[i] != blk_ref[prev])
    last = (i == n_items - 1) | (blk_ref[i] != blk_ref[nxt])
    lo, hi = lo_ref[i], hi_ref[i]
    whole = (lo == 0) & (hi == tb)

    @pl.when(i == 0)
    def _():
        cast_expert[0] = -1

    @pl.when(first)
    def _():
        state[0] = 0
        state[1] = 0

    @pl.when(hi > lo)
    def _():
        @pl.when(cast_expert[0] != exp_ref[i])
        def _():
            wgb[...] = wg_ref[0, 0].astype(BF16)
            wub[...] = wu_ref[0, 0].astype(BF16)
            wdb[...] = wd_ref[0, 0].astype(BF16)
            cast_expert[0] = exp_ref[i]

        xb = _load_token_tiles(x_ref, tb, BF16)
        g = jnp.dot(xb, wgb[...], preferred_element_type=F32)
        u = jnp.dot(xb, wub[...], preferred_element_type=F32)
        act = (g * jax.nn.sigmoid(g) * u).astype(BF16)
        y = jnp.dot(act, wdb[...], preferred_element_type=F32)

        @pl.when(whole)
        def _():
            _store_token_tiles(y_ref, y)
            state[0] = 1

        @pl.when(jnp.logical_not(whole))
        def _():
            row = lax.broadcasted_iota(jnp.int32, (tb, 1), 0)
            part = jnp.where((row >= lo) & (row < hi), y, 0.0)

            @pl.when(state[1] == 0)
            def _():
                acc[...] = part

            @pl.when(state[1] != 0)
            def _():
                acc[...] += part

            state[1] = 1

    @pl.when(last & (state[0] == 0))
    def _():
        _store_token_tiles(y_ref, acc[...])


def _expert_mlps(items, xs_tiles, w_gate, w_up, w_down, layer):
    item_blk, item_exp, item_lo, item_hi = items
    n_items = item_blk.shape[0]
    _, _, d, de = w_gate.shape
    sub = d // LANES
    w_index = lambda i, blk, ex, lo, hi: (layer, ex[i], 0, 0)
    grid_spec = pltpu.PrefetchScalarGridSpec(
        num_scalar_prefetch=4,
        grid=(n_items,),
        in_specs=[
            pl.BlockSpec((MOE_TB * sub, LANES), lambda i, blk, ex, lo, hi: (blk[i], 0)),
            pl.BlockSpec((1, 1, d, de), w_index),
            pl.BlockSpec((1, 1, d, de), w_index),
            pl.BlockSpec((1, 1, de, d), w_index),
        ],
        out_specs=pl.BlockSpec((MOE_TB * sub, LANES), lambda i, blk, ex, lo, hi: (blk[i], 0)),
        scratch_shapes=[
            pltpu.VMEM((MOE_TB, d), F32),
            pltpu.VMEM((d, de), BF16),
            pltpu.VMEM((d, de), BF16),
            pltpu.VMEM((de, d), BF16),
            pltpu.SMEM((1,), jnp.int32),
            pltpu.SMEM((2,), jnp.int32),
        ],
    )
    return pl.pallas_call(
        _expert_kernel,
        grid_spec=grid_spec,
        out_shape=jax.ShapeDtypeStruct(xs_tiles.shape, F32),
        compiler_params=_cparams(("arbitrary",)),
        name="expert_mlps",
    )(item_blk, item_exp, item_lo, item_hi, xs_tiles, w_gate, w_up, w_down)


def _combine_kernel(pos_ref, pos_next_ref, ys_hbm, x_ref, route_ref, g2_ref, lng_ref, lnb_ref, xo_ref, buf, sems,
                    *, alpha):
    i = pl.program_id(0)
    n = pl.num_programs(0)
    tm = COMBINE_TM
    sub = buf.shape[2] // tm
    slot = i % 2

    def gather(p_ref, dst_slot):
        def issue(jb, carry):
            for u in range(DMA_UNROLL):
                j = jb * DMA_UNROLL + u
                for k in range(TOP_K):
                    _tile_copy(ys_hbm, buf.at[dst_slot, k], p_ref[0, 0, TOP_K * j + k], j, sub,
                               sems.at[dst_slot]).start(priority=k % 2)
            return carry

        lax.fori_loop(0, tm // DMA_UNROLL, issue, 0)

    @pl.when(i == 0)
    def _():
        gather(pos_ref, slot)

    @pl.when(i + 1 < n)
    def _():
        gather(pos_next_ref, 1 - slot)

    route = route_ref[...]
    lane = lax.broadcasted_iota(jnp.int32, route.shape, 1)
    gates = [jnp.sum(jnp.where(lane == k, route, 0.0), axis=-1, keepdims=True) for k in range(TOP_K)]
    for k in range(TOP_K):
        pltpu.make_async_copy(ys_hbm.at[pl.ds(0, tm * sub)], buf.at[slot, k], sems.at[slot]).wait()
    y = _load_token_tiles(buf.at[slot, 0], tm, F32) * gates[0]
    for k in range(1, TOP_K):
        y = y + _load_token_tiles(buf.at[slot, k], tm, F32) * gates[k]
    z = alpha * x_ref[...] + (1.0 + g2_ref[0]) * y
    xo_ref[...] = _layer_norm_rows(z, lng_ref[...], lnb_ref[...])


def _combine(pos, ys, x2d, route2d, g2, ln_g, ln_b, seq, alpha):
    n_tok, d = x2d.shape
    tm = COMBINE_TM
    n_tiles = n_tok // tm
    tiles_per_batch = seq // tm
    const = lambda i: (0, 0)
    pos3 = pos.reshape(n_tiles, 1, TOP_K * tm)
    return pl.pallas_call(
        functools.partial(_combine_kernel, alpha=alpha),
        grid=(n_tiles,),
        in_specs=[
            pl.BlockSpec((1, 1, TOP_K * tm), lambda i: (i, 0, 0), memory_space=pltpu.SMEM),
            pl.BlockSpec((1, 1, TOP_K * tm), lambda i: (jnp.minimum(i + 1, n_tiles - 1), 0, 0),
                         memory_space=pltpu.SMEM),
            pl.BlockSpec(memory_space=pl.ANY),
            pl.BlockSpec((tm, d), lambda i: (i, 0)),
            pl.BlockSpec((tm, LANES), lambda i: (i, 0)),
            pl.BlockSpec((1, 1, d), lambda i: (i // tiles_per_batch, 0, 0)),
            pl.BlockSpec((1, d), const),
            pl.BlockSpec((1, d), const),
        ],
        out_specs=pl.BlockSpec((tm, d), lambda i: (i, 0)),
        out_shape=jax.ShapeDtypeStruct((n_tok, d), F32),
        scratch_shapes=[
            pltpu.VMEM((2, TOP_K, tm * (d // LANES), LANES), F32),
            pltpu.SemaphoreType.DMA((2,)),
        ],
        compiler_params=_cparams(("arbitrary",)),
        name="moe_combine",
    )(pos3, pos3, ys, x2d, route2d, g2, ln_g, ln_b)


def _combine_project_kernel(pos_ref, pos_next_ref, ys_hbm, x_ref, route_ref, g2_ref, lng_ref, lnb_ref,
                            sc_ref, sh_ref, w_ref, *rest, alpha, d_model, gqa):
    xo_ref, qkv_ref, buf, sems = rest[-4:]
    i = pl.program_id(0)
    n = pl.num_programs(0)
    tm = ROW_TILE
    sub = buf.shape[2] // tm
    slot = i % 2

    def tile_gather(p_ref, dst_slot, j, k):
        return _tile_copy(ys_hbm, buf.at[dst_slot, k], p_ref[0, 0, TOP_K * j + k], j, sub, sems.at[dst_slot])

    @pl.when(i == 0)
    def _():
        def issue(jb, carry):
            for u in range(DMA_UNROLL):
                for k in range(TOP_K):
                    tile_gather(pos_ref, slot, jb * DMA_UNROLL + u, k).start(priority=k % 2)
            return carry

        lax.fori_loop(0, tm // DMA_UNROLL, issue, 0)

    route = route_ref[...]
    lane = lax.broadcasted_iota(jnp.int32, route.shape, 1)
    gates = [jnp.sum(jnp.where(lane == k, route, 0.0), axis=-1, keepdims=True) for k in range(TOP_K)]
    for k in range(TOP_K):
        pltpu.make_async_copy(ys_hbm.at[pl.ds(0, tm * sub)], buf.at[slot, k], sems.at[slot]).wait()
    y = _load_token_tiles(buf.at[slot, 0], tm, F32) * gates[0]
    for k in range(1, TOP_K):
        y = y + _load_token_tiles(buf.at[slot, k], tm, F32) * gates[k]
    z = alpha * x_ref[...] + (1.0 + g2_ref[0]) * y
    xn = _layer_norm_rows(z, lng_ref[...], lnb_ref[...])
    xo_ref[...] = xn
    h = (xn * (1.0 + sc_ref[0]) + sh_ref[0]).astype(BF16)

    for j in range(tm):
        for k in range(TOP_K):
            tile_gather(pos_next_ref, 1 - slot, j, k).start(priority=k % 2)

    if gqa:
        _project_gqa(h, w_ref, *rest[:6], qkv_ref, d_model)
    else:
        _project_na(h, w_ref, qkv_ref, d_model)

    @pl.when(i == n - 1)
    def _():
        for k in range(TOP_K):
            pltpu.make_async_copy(ys_hbm.at[pl.ds(0, tm * sub)], buf.at[1 - slot, k], sems.at[1 - slot]).wait()


def _combine_project(pos, ys, x2d, route2d, g2, ln_g, ln_b, sc, sh, w_bf16, gqa_extras, seq, alpha):
    n_tok, d = x2d.shape
    n = w_bf16.shape[1]
    tm = ROW_TILE
    n_tiles = n_tok // tm
    tiles_per_batch = seq // tm
    const = lambda i: (0, 0)
    per_b = lambda i: (i // tiles_per_batch, 0, 0)
    pos3 = pos.reshape(n_tiles, 1, TOP_K * tm)
    extra_specs = []
    if gqa_extras:
        extra_specs = [
            pl.BlockSpec((1, QKV_CHUNK), const),
            pl.BlockSpec((1, QKV_CHUNK), const),
            pl.BlockSpec((tm, QKV_CHUNK), lambda i: (i % tiles_per_batch, 0)),
            pl.BlockSpec((tm, QKV_CHUNK), lambda i: (i % tiles_per_batch, 0)),
            pl.BlockSpec((QKV_CHUNK, LANES), const),
            pl.BlockSpec((LANES, QKV_CHUNK), const),
        ]
    return pl.pallas_call(
        functools.partial(_combine_project_kernel, alpha=alpha, d_model=d, gqa=bool(gqa_extras)),
        grid=(n_tiles,),
        in_specs=[
            pl.BlockSpec((1, 1, TOP_K * tm), lambda i: (i, 0, 0), memory_space=pltpu.SMEM),
            pl.BlockSpec((1, 1, TOP_K * tm), lambda i: (jnp.minimum(i + 1, n_tiles - 1), 0, 0),
                         memory_space=pltpu.SMEM),
            pl.BlockSpec(memory_space=pl.ANY),
            pl.BlockSpec((tm, d), lambda i: (i, 0)),
            pl.BlockSpec((tm, LANES), lambda i: (i, 0)),
            pl.BlockSpec((1, 1, d), per_b),
            pl.BlockSpec((1, d), const),
            pl.BlockSpec((1, d), const),
            pl.BlockSpec((1, 1, d), per_b),
            pl.BlockSpec((1, 1, d), per_b),
            pl.BlockSpec((d, n), const),
        ] + extra_specs,
        out_specs=[
            pl.BlockSpec((tm, d), lambda i: (i, 0)),
            pl.BlockSpec((tm, n), lambda i: (i, 0)),
        ],
        out_shape=[
            jax.ShapeDtypeStruct((n_tok, d), F32),
            jax.ShapeDtypeStruct((n_tok, n), BF16),
        ],
        scratch_shapes=[
            pltpu.VMEM((2, TOP_K, tm * (d // LANES), LANES), F32),
            pltpu.SemaphoreType.DMA((2,)),
        ],
        compiler_params=_cparams(("arbitrary",)),
        name="combine_qkv_gqa" if gqa_extras else "combine_qkv_na",
    )(pos3, pos3, ys, x2d, route2d, g2, ln_g, ln_b, sc, sh, w_bf16, *gqa_extras)


def _route_pos_kernel(route_ref, pos_ref, cnt_ref, counts, base):
    p = pl.program_id(0)
    j = pl.program_id(1)
    route = route_ref[...]
    t = route.shape[0]
    lane = lax.broadcasted_iota(jnp.int32, route.shape, 1)
    lanef = lane.astype(F32)
    e1 = jnp.sum(jnp.where(lane == 2, route, 0.0), axis=-1, keepdims=True)
    e2 = jnp.sum(jnp.where(lane == 3, route, 0.0), axis=-1, keepdims=True)
    m1 = lanef == e1
    m2 = lanef == e2
    mem = jnp.where(m1 | m2, 1.0, 0.0)
    colsum = jnp.sum(mem, axis=0, keepdims=True)

    @pl.when((p == 0) & (j == 0))
    def _():
        counts[...] = jnp.zeros_like(counts)

    @pl.when(p == 0)
    def _():
        counts[...] = counts[...] + colsum

    @pl.when((p == 1) & (j == 0))
    def _():
        c = counts[...]
        cnt_ref[...] = c
        c_hi = jnp.floor(c * (1.0 / 256.0))
        c_lo = c - c_hi * 256.0
        r_i = lax.broadcasted_iota(jnp.int32, (LANES, LANES), 0)
        c_i = lax.broadcasted_iota(jnp.int32, (LANES, LANES), 1)
        excl = jnp.where(r_i < c_i, 1.0, 0.0).astype(BF16)
        base[...] = (jnp.dot(c_hi.astype(BF16), excl, preferred_element_type=F32) * 256.0
                     + jnp.dot(c_lo.astype(BF16), excl, preferred_element_type=F32))

    @pl.when(p == 1)
    def _():
        r_i = lax.broadcasted_iota(jnp.int32, (t, t), 0)
        c_i = lax.broadcasted_iota(jnp.int32, (t, t), 1)
        before = jnp.where(c_i < r_i, 1.0, 0.0).astype(BF16)
        rank = base[0:1, :] + jnp.dot(before, mem.astype(BF16), preferred_element_type=F32)
        pos1 = jnp.sum(jnp.where(m1, rank, 0.0), axis=-1, keepdims=True)
        pos2 = jnp.sum(jnp.where(m2, rank, 0.0), axis=-1, keepdims=True)
        pos_ref[...] = jnp.where(lane == 0, pos1, jnp.where(lane == 1, pos2, 0.0))
        base[...] = base[...] + colsum


def _routing_tables(route2d):
    n_tok = route2d.shape[0]
    n_tiles = n_tok // ROUTE_T
    posf, cnt = pl.pallas_call(
        _route_pos_kernel,
        grid=(2, n_tiles),
        in_specs=[pl.BlockSpec((ROUTE_T, LANES), lambda p, j: (j, 0))],
        out_specs=[
            pl.BlockSpec((ROUTE_T, LANES), lambda p, j: (p * j, 0)),
            pl.BlockSpec((8, LANES), lambda p, j: (0, 0)),
        ],
        out_shape=[
            jax.ShapeDtypeStruct((n_tok, LANES), F32),
            jax.ShapeDtypeStruct((8, LANES), F32),
        ],
        scratch_shapes=[pltpu.VMEM((8, LANES), F32), pltpu.VMEM((8, LANES), F32)],
        compiler_params=_cparams(("arbitrary", "arbitrary")),
        name="route_positions",
    )(route2d)
    n_rows = n_tok * TOP_K
    pos = posf[:, :TOP_K].astype(jnp.int32).reshape(n_rows)
    counts = cnt[0, :N_EXPERTS].astype(jnp.int32)
    seg_end = jnp.cumsum(counts)
    cuts = jnp.concatenate([jnp.arange(n_rows // MOE_TB, dtype=jnp.int32) * MOE_TB, seg_end - counts])
    n_items = cuts.shape[0]
    idx = jnp.arange(n_items, dtype=jnp.int32)
    rank = (jnp.sum(cuts[None, :] < cuts[:, None], axis=1)
            + jnp.sum((cuts[None, :] == cuts[:, None]) & (idx[None, :] < idx[:, None]), axis=1))
    lo_abs = jnp.sum(jnp.where(rank[None, :] == idx[:, None], cuts[None, :], 0), axis=1)
    hi_abs = jnp.concatenate([lo_abs[1:], jnp.full((1,), n_rows, jnp.int32)])
    item_blk = jnp.minimum(lo_abs // MOE_TB, n_rows // MOE_TB - 1)
    item_exp = jnp.minimum(jnp.sum(seg_end[None, :] <= lo_abs[:, None], axis=1), N_EXPERTS - 1)
    item_lo = lo_abs - item_blk * MOE_TB
    item_hi = hi_abs - item_blk * MOE_TB
    items = tuple(a.astype(jnp.int32) for a in (item_blk, item_exp, item_lo, item_hi))
    return pos, items


def _rope_tables(seq, width):
    t = jnp.arange(seq, dtype=jnp.int32)
    pos = jnp.stack([t // GRID_W, t % GRID_W], axis=-1).astype(F32)
    inv_freq = ROPE_THETA ** (-jnp.arange(ROPE_PAIRS, dtype=F32) / ROPE_PAIRS)
    ang = pos[:, :, None] * inv_freq
    cos, sin = jnp.cos(ang), jnp.sin(ang)
    cosf = jnp.stack([cos, cos], axis=2).reshape(seq, HEAD_DIM)
    sinf = jnp.stack([-sin, sin], axis=2).reshape(seq, HEAD_DIM)
    reps = width // HEAD_DIM
    return jnp.tile(cosf, (1, reps)), jnp.tile(sinf, (1, reps))


def _dup_heads(w):
    d = w.shape[0]
    w4 = w.reshape(d, GQA_KV_HEADS, 1, HEAD_DIM)
    return jnp.concatenate([w4, w4], axis=2).reshape(d, GQA_KV_HEADS * HEAD_PAIR)


def kernel(x, c, ada_w, ada_b, ln_g, ln_b, na_w_qkv, na_rpb, na_w_o, gqa_w_qkv, gqa_q_norm, gqa_k_norm, gqa_w_o,
           moe_w_group, moe_b_group, moe_w_expert, moe_b_expert, moe_w_gate, moe_w_up, moe_w_down):
    b, s, d = x.shape
    depth = ada_w.shape[0]
    n_tok = b * s
    rows = s // GRID_W
    kvd = GQA_KV_HEADS * HEAD_DIM
    alpha = (2 * depth) ** 0.25

    mod = _ada_modulation(c, ada_w, ada_b).reshape(depth, b, 6, 1, d)

    cosf, sinf = _rope_tables(s, QKV_CHUNK)
    head_of_lane = jnp.arange(QKV_CHUNK, dtype=jnp.int32) // HEAD_DIM
    seg_t = (head_of_lane[None, :] == jnp.arange(LANES, dtype=jnp.int32)[:, None])
    seg = (seg_t.T.astype(F32) / HEAD_DIM).astype(BF16)
    seg_t = seg_t.astype(BF16)

    def qkv_params(i):
        j = i // 2
        if i % 2 == 0:
            return na_w_qkv[j].astype(BF16), ()
        w = gqa_w_qkv[j]
        w_all = jnp.concatenate([w[:, :d], _dup_heads(w[:, d:d + kvd]), _dup_heads(w[:, d + kvd:])], axis=1)
        reps = QKV_CHUNK // HEAD_DIM
        return w_all.astype(BF16), (jnp.tile(gqa_q_norm[j], reps)[None, :], jnp.tile(gqa_k_norm[j], reps)[None, :],
                                    cosf, sinf, seg, seg_t)

    qkv = None
    for i in range(depth):
        sh1, sc1, g1, sh2, sc2, g2 = (mod[i, :, k] for k in range(6))
        j = i // 2
        if qkv is None:
            w_bf16, extras = qkv_params(i)
            qkv = _qkv_gqa(x, sc1, sh1, w_bf16, *extras) if extras else _qkv_na(x, sc1, sh1, w_bf16)
        if i % 2 == 0:
            attn = _na_attention(qkv, _na_bias_table(na_rpb[j]), rows)
            w_o = na_w_o[j]
        else:
            attn = _gqa_attention(qkv, d, gqa_q_norm[j], gqa_k_norm[j])
            w_o = gqa_w_o[j]

        w_r = jnp.concatenate([moe_w_group[i], moe_w_expert[i]], axis=1)
        w_r = jnp.pad(w_r, ((0, 0), (0, LANES - w_r.shape[1])))
        wr_hi, wr_lo = _split_bf16(w_r)
        b_r = jnp.concatenate([moe_b_group[i], moe_b_expert[i]])
        b_r = jnp.pad(b_r, (0, LANES - b_r.shape[0]))[None, :]
        x, h2, route = _post_attn(attn, w_o.astype(BF16), x, g1, sc2, sh2, ln_g[i, 0][None, :], ln_b[i, 0][None, :],
                                  jnp.concatenate([wr_hi, wr_lo], axis=1), b_r, alpha)

        route2d = route.reshape(n_tok, LANES)
        pos, items = _routing_tables(route2d)
        xs = _dispatch(pos, h2, n_tok)
        ys = _expert_mlps(items, xs, moe_w_gate, moe_w_up, moe_w_down, i)
        lng, lnb = ln_g[i, 1][None, :], ln_b[i, 1][None, :]
        if i + 1 < depth:
            w_bf16, extras = qkv_params(i + 1)
            x2d, qkv2d = _combine_project(pos, ys, x.reshape(n_tok, d), route2d, g2, lng, lnb,
                                          mod[i + 1, :, 1], mod[i + 1, :, 0], w_bf16, extras, s, alpha)
            x, qkv = x2d.reshape(b, s, d), qkv2d.reshape(b, s, -1)
        else:
            x = _combine(pos, ys, x.reshape(n_tok, d), route2d, g2, lng, lnb, s, alpha).reshape(b, s, d)
    return x
```

```python
import functools

import jax
import jax.numpy as jnp
from jax import lax
from jax.experimental import pallas as pl
from jax.experimental.pallas import tpu as pltpu

GRID_W = 64
N_HEADS = 16
HEAD_DIM = 64
NA_ROWS = 8
NA_COLS = 16
GQA_KV_HEADS = 4
ROPE_THETA = 10000.0
ROPE_PAIRS = HEAD_DIM // 4
N_GROUPS = 4
EXPERTS_PER_GROUP = 8
N_EXPERTS = N_GROUPS * EXPERTS_PER_GROUP
TOP_K = 2
LN_EPS = 1e-5
RMS_EPS = 1e-6

LANES = 128
HEAD_PAIR = 2 * HEAD_DIM
VMEM_LIMIT_BYTES = 52 * 1024 * 1024
ADA_TN = 1536
ROW_TILE = 512
QKV_CHUNK = 512
GQA_TQ = 512
NA_ROWS_PER_STEP = 2
NA_STAGE_PAIRS = 4
MOE_TB = 256
DISPATCH_TM = 1024
COMBINE_TM = 1024
DMA_UNROLL = 8
NEG_BIG = -1e30
LOG2_E = 1.4426950408889634
Q_SCALE = (HEAD_DIM ** -0.5) * LOG2_E
GQA_BOUND_MARGIN = 1.02
GQA_BOUND_MAX = 50.0

F32 = jnp.float32
BF16 = jnp.bfloat16


def _cparams(sem):
    return pltpu.CompilerParams(dimension_semantics=sem, vmem_limit_bytes=VMEM_LIMIT_BYTES)


def _split_bf16(a):
    hi = a.astype(BF16)
    lo = (a - hi.astype(F32)).astype(BF16)
    return hi, lo


def _ada_kernel(c_ref, w_ref, b_ref, o_ref):
    c = c_ref[...]
    ca = (c * jax.nn.sigmoid(c)).astype(BF16)
    o_ref[0] = jnp.dot(ca, w_ref[0].astype(BF16), preferred_element_type=F32) + b_ref[0]


def _ada_modulation(c, ada_w, ada_b):
    depth, d, n = ada_w.shape
    b = c.shape[0]
    return pl.pallas_call(
        _ada_kernel,
        grid=(depth, n // ADA_TN),
        in_specs=[
            pl.BlockSpec((b, d), lambda i, j: (0, 0)),
            pl.BlockSpec((1, d, ADA_TN), lambda i, j: (i, 0, j)),
            pl.BlockSpec((1, 1, ADA_TN), lambda i, j: (i, 0, j)),
        ],
        out_specs=pl.BlockSpec((1, b, ADA_TN), lambda i, j: (i, 0, j)),
        out_shape=jax.ShapeDtypeStruct((depth, b, n), F32),
        compiler_params=_cparams(("arbitrary", "arbitrary")),
        name="ada_modulation",
    )(c, ada_w, ada_b.reshape(depth, 1, n))


def _project_na(h, w_ref, o_ref, d_model):
    n = w_ref.shape[1]
    for n0 in range(0, n, QKV_CHUNK):
        acc = jnp.dot(h, w_ref[:, n0:n0 + QKV_CHUNK], preferred_element_type=F32)
        if n0 < d_model:
            acc = acc * Q_SCALE
        o_ref[:, n0:n0 + QKV_CHUNK] = acc.astype(BF16)


def _qkv_na_kernel(x_ref, sc_ref, sh_ref, w_ref, o_ref, *, d_model):
    h = (x_ref[0] * (1.0 + sc_ref[0]) + sh_ref[0]).astype(BF16)
    _project_na(h, w_ref, o_ref.at[0], d_model)


def _qkv_na(x, sc, sh, w_bf16):
    b, s, d = x.shape
    n = w_bf16.shape[1]
    return pl.pallas_call(
        functools.partial(_qkv_na_kernel, d_model=d),
        grid=(b, s // ROW_TILE),
        in_specs=[
            pl.BlockSpec((1, ROW_TILE, d), lambda i, j: (i, j, 0)),
            pl.BlockSpec((1, 1, d), lambda i, j: (i, 0, 0)),
            pl.BlockSpec((1, 1, d), lambda i, j: (i, 0, 0)),
            pl.BlockSpec((d, n), lambda i, j: (0, 0)),
        ],
        out_specs=pl.BlockSpec((1, ROW_TILE, n), lambda i, j: (i, j, 0)),
        out_shape=jax.ShapeDtypeStruct((b, s, n), BF16),
        compiler_params=_cparams(("arbitrary", "arbitrary")),
        name="qkv_na",
    )(x, sc, sh, w_bf16)


N_DR = 2 * NA_ROWS - 1
N_DC = 2 * NA_COLS - 1


def _na_bias_kernel(r_ref, m_ref, valid_ref, o_ref):
    r = r_ref[...]
    hi = r.astype(BF16)
    r1 = r - hi.astype(F32)
    mid = r1.astype(BF16)
    lo = (r1 - mid.astype(F32)).astype(BF16)
    m = m_ref[...]
    acc = (jnp.dot(hi, m, preferred_element_type=F32) + jnp.dot(mid, m, preferred_element_type=F32)
           + jnp.dot(lo, m, preferred_element_type=F32))
    o_ref[...] = jnp.where(valid_ref[...] > 0.0, acc * LOG2_E, NEG_BIG)


def _na_bias_table(rpb):
    qc = jnp.arange(GRID_W, dtype=jnp.int32)
    kc = jnp.arange(GRID_W, dtype=jnp.int32)
    win_start = jnp.clip(qc - NA_COLS // 2, 0, GRID_W - NA_COLS)
    valid = (kc[None, :] >= win_start[:, None]) & (kc[None, :] < win_start[:, None] + NA_COLS)
    dc = jnp.clip(kc[None, :] - qc[:, None] + NA_COLS - 1, 0, N_DC - 1)
    onehot = (jnp.arange(LANES, dtype=jnp.int32)[:, None] == dc.reshape(1, GRID_W * GRID_W)).astype(BF16)
    r2d = jnp.pad(rpb.reshape(N_HEADS * N_DR, N_DC).astype(F32), ((0, 0), (0, LANES - N_DC)))
    tiles = pl.pallas_call(
        _na_bias_kernel,
        out_shape=jax.ShapeDtypeStruct((N_HEADS * N_DR, GRID_W * GRID_W), F32),
        compiler_params=pltpu.CompilerParams(vmem_limit_bytes=VMEM_LIMIT_BYTES),
        name="na_bias_tiles",
    )(r2d, onehot, valid.reshape(1, GRID_W * GRID_W).astype(F32))
    tiles = tiles.reshape(N_HEADS, N_DR, GRID_W, GRID_W)
    pairs = jnp.concatenate([tiles[:, :N_DR - 1], tiles[:, 1:]], axis=-1)
    pairs = pairs.reshape(N_HEADS // 2, 2, N_DR - 1, GRID_W, HEAD_PAIR).transpose(0, 2, 1, 3, 4)
    return pairs.reshape(N_HEADS // 2, N_DR - 1, 2 * GRID_W, HEAD_PAIR)


def _na_kernel(q_ref, k_ref, v_ref, bias_ref, o_ref, *, rows):
    nk = NA_ROWS * GRID_W
    lane = lax.broadcasted_iota(jnp.int32, (GRID_W, HEAD_PAIR), 1)
    first = lane < HEAD_DIM
    for sub_row in range(NA_ROWS_PER_STEP):
        r = pl.program_id(1) * NA_ROWS_PER_STEP + sub_row
        row_start = jnp.clip(r - NA_ROWS // 2, 0, rows - NA_ROWS)
        start = pl.multiple_of(row_start * GRID_W, GRID_W)
        dr_base = row_start - r + (NA_ROWS - 1)
        q_rows = slice(sub_row * GRID_W, (sub_row + 1) * GRID_W)
        for hp0 in range(0, N_HEADS // 2, NA_STAGE_PAIRS):
            pairs = range(hp0, hp0 + NA_STAGE_PAIRS)
            cols = {hp: slice(hp * HEAD_PAIR, (hp + 1) * HEAD_PAIR) for hp in pairs}
            s_all = {}
            for hp in pairs:
                q2 = q_ref[0, q_rows, cols[hp]]
                k2 = k_ref[0, pl.ds(start, nk), cols[hp]]
                zero = jnp.zeros_like(q2)
                qs = jnp.concatenate([jnp.where(first, q2, zero), jnp.where(first, zero, q2)], axis=0)
                s = lax.dot_general(qs, k2, (((1,), (1,)), ((), ())), preferred_element_type=F32)
                s_all[hp] = s + jnp.concatenate([bias_ref[hp, dr_base + 2 * m] for m in range(NA_ROWS // 2)],
                                                axis=1)
            m_all = {hp: jnp.max(s_all[hp], axis=-1, keepdims=True) for hp in pairs}
            p_all = {hp: jnp.exp2(s_all[hp] - m_all[hp]) for hp in pairs}
            l_all = {hp: jnp.sum(p_all[hp], axis=-1, keepdims=True) for hp in pairs}
            o_all = {hp: jnp.dot(p_all[hp].astype(BF16), v_ref[0, pl.ds(start, nk), cols[hp]],
                                 preferred_element_type=F32) for hp in pairs}
            for hp in pairs:
                o = o_all[hp] / l_all[hp]
                o_ref[0, q_rows, cols[hp]] = jnp.where(first, o[:GRID_W], o[GRID_W:]).astype(BF16)


def _na_attention(qkv, bias, rows):
    b, s, n3 = qkv.shape
    d = n3 // 3
    assert rows >= NA_ROWS and NA_ROWS % 2 == 0 and rows % NA_ROWS_PER_STEP == 0
    q_tile = NA_ROWS_PER_STEP * GRID_W
    return pl.pallas_call(
        functools.partial(_na_kernel, rows=rows),
        grid=(b, rows // NA_ROWS_PER_STEP),
        in_specs=[
            pl.BlockSpec((1, q_tile, d), lambda i, r: (i, r, 0)),
            pl.BlockSpec((1, s, d), lambda i, r: (i, 0, 1)),
            pl.BlockSpec((1, s, d), lambda i, r: (i, 0, 2)),
            pl.BlockSpec(bias.shape, lambda i, r: (0, 0, 0, 0)),
        ],
        out_specs=pl.BlockSpec((1, q_tile, d), lambda i, r: (i, r, 0)),
        out_shape=jax.ShapeDtypeStruct((b, s, d), BF16),
        compiler_params=_cparams(("arbitrary", "arbitrary")),
        name="na_attention",
    )(qkv, qkv, qkv, bias)


def _rms_rope(a, gain, cosf, sinf, seg, seg_t):
    qq_hi, qq_lo = _split_bf16(a * a)
    ms = jnp.dot(qq_hi, seg, preferred_element_type=F32) + jnp.dot(qq_lo, seg, preferred_element_type=F32)
    r_hi, r_lo = _split_bf16(lax.rsqrt(ms + RMS_EPS))
    rf = jnp.dot(r_hi, seg_t, preferred_element_type=F32) + jnp.dot(r_lo, seg_t, preferred_element_type=F32)
    an = a * rf * gain
    n = a.shape[1]
    lane = lax.broadcasted_iota(jnp.int32, a.shape, 1)
    partner = jnp.where(lane % (2 * ROPE_PAIRS) < ROPE_PAIRS,
                        pltpu.roll(an, n - ROPE_PAIRS, 1), pltpu.roll(an, ROPE_PAIRS, 1))
    return an * cosf + partner * sinf


def _project_gqa(h, w_ref, qg_ref, kg_ref, cos_ref, sin_ref, seg_ref, segt_ref, o_ref, d_model):
    n = w_ref.shape[1]
    kv_cols = (n - d_model) // 2
    cosf, sinf = cos_ref[...], sin_ref[...]
    seg, seg_t = seg_ref[...], segt_ref[...]
    for n0 in range(0, n, QKV_CHUNK):
        acc = jnp.dot(h, w_ref[:, n0:n0 + QKV_CHUNK], preferred_element_type=F32)
        if n0 < d_model:
            acc = _rms_rope(acc, qg_ref[...], cosf, sinf, seg, seg_t) * Q_SCALE
        elif n0 < d_model + kv_cols:
            acc = _rms_rope(acc, kg_ref[...], cosf, sinf, seg, seg_t)
        else:
            lane = lax.broadcasted_iota(jnp.int32, acc.shape, 1)
            acc = jnp.where(lane % HEAD_PAIR < HEAD_DIM, acc, 1.0)
        o_ref[:, n0:n0 + QKV_CHUNK] = acc.astype(BF16)


def _qkv_gqa_kernel(x_ref, sc_ref, sh_ref, w_ref, qg_ref, kg_ref, cos_ref, sin_ref, seg_ref, segt_ref, o_ref,
                    *, d_model):
    h = (x_ref[0] * (1.0 + sc_ref[0]) + sh_ref[0]).astype(BF16)
    _project_gqa(h, w_ref, qg_ref, kg_ref, cos_ref, sin_ref, seg_ref, segt_ref, o_ref.at[0], d_model)


def _qkv_gqa(x, sc, sh, w_bf16, q_gain, k_gain, cosf, sinf, seg, seg_t):
    b, s, d = x.shape
    n = w_bf16.shape[1]
    const = lambda j, i: (0, 0)
    return pl.pallas_call(
        functools.partial(_qkv_gqa_kernel, d_model=d),
        grid=(s // ROW_TILE, b),
        in_specs=[
            pl.BlockSpec((1, ROW_TILE, d), lambda j, i: (i, j, 0)),
            pl.BlockSpec((1, 1, d), lambda j, i: (i, 0, 0)),
            pl.BlockSpec((1, 1, d), lambda j, i: (i, 0, 0)),
            pl.BlockSpec((d, n), const),
            pl.BlockSpec((1, QKV_CHUNK), const),
            pl.BlockSpec((1, QKV_CHUNK), const),
            pl.BlockSpec((ROW_TILE, QKV_CHUNK), lambda j, i: (j, 0)),
            pl.BlockSpec((ROW_TILE, QKV_CHUNK), lambda j, i: (j, 0)),
            pl.BlockSpec((QKV_CHUNK, LANES), const),
            pl.BlockSpec((LANES, QKV_CHUNK), const),
        ],
        out_specs=pl.BlockSpec((1, ROW_TILE, n), lambda j, i: (i, j, 0)),
        out_shape=jax.ShapeDtypeStruct((b, s, n), BF16),
        compiler_params=_cparams(("arbitrary", "arbitrary")),
        name="qkv_gqa",
    )(x, sc, sh, w_bf16, q_gain, k_gain, cosf, sinf, seg, seg_t)


def _gqa_kernel(bound_ref, q_ref, k_ref, v_ref, o_ref, *, use_bound):
    k2 = k_ref[0]
    v2 = v_ref[0]
    tq = q_ref.shape[1]
    lane = lax.broadcasted_iota(jnp.int32, (tq, HEAD_PAIR), 1)
    first = lane < HEAD_DIM
    for pair in range(q_ref.shape[2] // HEAD_PAIR):
        cols = slice(pair * HEAD_PAIR, (pair + 1) * HEAD_PAIR)
        q2 = q_ref[0, :, cols]
        zero = jnp.zeros_like(q2)
        qs = jnp.concatenate([jnp.where(first, q2, zero), jnp.where(first, zero, q2)], axis=0)
        s = lax.dot_general(qs, k2, (((1,), (1,)), ((), ())), preferred_element_type=F32)
        m = bound_ref[0] if use_bound else jnp.max(s, axis=-1, keepdims=True)
        p = jnp.exp2(s - m).astype(BF16)
        o = jnp.dot(p, v2, preferred_element_type=F32)
        o_a, o_b = o[:tq], o[tq:]
        out = jnp.where(first, o_a / pltpu.roll(o_a, HEAD_DIM, 1), pltpu.roll(o_b, HEAD_DIM, 1) / o_b)
        o_ref[0, :, cols] = out.astype(BF16)


def _gqa_logit_bound(q_gain, k_gain):
    return (GQA_BOUND_MARGIN * LOG2_E * (HEAD_DIM ** 0.5)
            * jnp.max(jnp.abs(q_gain)) * jnp.max(jnp.abs(k_gain))).astype(F32).reshape(1)


def _gqa_attention(qkv, d_model, q_gain, k_gain):
    bound = _gqa_logit_bound(q_gain, k_gain)
    return lax.cond(bound[0] <= GQA_BOUND_MAX,
                    functools.partial(_gqa_attention_call, d_model=d_model, use_bound=True),
                    functools.partial(_gqa_attention_call, d_model=d_model, use_bound=False),
                    bound, qkv)


def _gqa_attention_call(bound, qkv, *, d_model, use_bound):
    b, s, n = qkv.shape
    group_cols = d_model // GQA_KV_HEADS
    k_block0 = d_model // HEAD_PAIR
    v_block0 = k_block0 + GQA_KV_HEADS
    return pl.pallas_call(
        functools.partial(_gqa_kernel, use_bound=use_bound),
        grid=(b, GQA_KV_HEADS, s // GQA_TQ),
        in_specs=[
            pl.BlockSpec(memory_space=pltpu.SMEM),
            pl.BlockSpec((1, GQA_TQ, group_cols), lambda i, g, t: (i, t, g)),
            pl.BlockSpec((1, s, HEAD_PAIR), lambda i, g, t: (i, 0, k_block0 + g)),
            pl.BlockSpec((1, s, HEAD_PAIR), lambda i, g, t: (i, 0, v_block0 + g)),
        ],
        out_specs=pl.BlockSpec((1, GQA_TQ, group_cols), lambda i, g, t: (i, t, g)),
        out_shape=jax.ShapeDtypeStruct((b, s, d_model), BF16),
        compiler_params=_cparams(("arbitrary", "arbitrary", "arbitrary")),
        name="gqa_attention_bounded" if use_bound else "gqa_attention",
    )(bound, qkv, qkv, qkv)


def _store_token_tiles(ref, val, first_token=0):
    t, d = val.shape
    sub = d // LANES
    for s in range(sub):
        ref[pl.ds(first_token * sub + s, t, stride=sub), :] = val[:, s * LANES:(s + 1) * LANES]


def _load_token_tiles(ref, t, dtype):
    sub = ref.shape[0] // t
    return jnp.concatenate([ref[pl.ds(s, t, stride=sub), :].astype(dtype) for s in range(sub)], axis=1)


def _tile_copy(src_hbm, dst_vmem, src_tok, dst_tok, sub, sem):
    return pltpu.make_async_copy(src_hbm.at[pl.ds(pl.multiple_of(src_tok * sub, sub), sub)],
                                 dst_vmem.at[pl.ds(pl.multiple_of(dst_tok * sub, sub), sub)], sem)


def _layer_norm_rows(z, g, b):
    mu = jnp.mean(z, axis=-1, keepdims=True)
    zc = z - mu
    var = jnp.mean(zc * zc, axis=-1, keepdims=True)
    return zc * lax.rsqrt(var + LN_EPS) * g + b


def _router(lg):
    lane = lax.broadcasted_iota(jnp.int32, lg.shape, 1)
    lanef = lane.astype(F32)
    far = float(LANES)
    gmask = lane < N_GROUPS
    lgm = jnp.where(gmask, lg, NEG_BIG)
    mg = jnp.max(lgm, axis=-1, keepdims=True)
    sg = jnp.sum(jnp.exp(lgm - mg), axis=-1, keepdims=True)
    g_prob = 1.0 / sg
    g_idx = jnp.min(jnp.where(lgm == mg, lanef, far), axis=-1, keepdims=True)
    e_lo = N_GROUPS + g_idx * EXPERTS_PER_GROUP
    emask = (lanef >= e_lo) & (lanef < e_lo + EXPERTS_PER_GROUP)
    lem = jnp.where(emask, lg, NEG_BIG)
    me = jnp.max(lem, axis=-1, keepdims=True)
    ee = jnp.exp(lem - me)
    pe = jnp.where(emask, ee / jnp.sum(ee, axis=-1, keepdims=True), -1.0)
    p1 = jnp.max(pe, axis=-1, keepdims=True)
    i1 = jnp.min(jnp.where(pe == p1, lanef, far), axis=-1, keepdims=True)
    pe2 = jnp.where(lanef == i1, -1.0, pe)
    p2 = jnp.max(pe2, axis=-1, keepdims=True)
    i2 = jnp.min(jnp.where(pe2 == p2, lanef, far), axis=-1, keepdims=True)
    den = p1 + p2
    gate1 = g_prob * (p1 / den)
    gate2 = g_prob * (p2 / den)
    out = jnp.where(lane == 0, gate1, 0.0)
    out = jnp.where(lane == 1, gate2, out)
    out = jnp.where(lane == 2, i1 - N_GROUPS, out)
    out = jnp.where(lane == 3, i2 - N_GROUPS, out)
    return out, lanef == i1, lanef == i2


def _post_attn_kernel(a_ref, wo_ref, x_ref, g1_ref, sc2_ref, sh2_ref, lng_ref, lnb_ref, wr_ref, br_ref,
                      before_ref, xo_ref, h2_ref, route_ref, cnt_ref, run_cnt, *, alpha):
    @pl.when((pl.program_id(0) == 0) & (pl.program_id(1) == 0))
    def _():
        run_cnt[...] = jnp.zeros_like(run_cnt)

    y = jnp.dot(a_ref[0], wo_ref[...], preferred_element_type=F32)
    z = alpha * x_ref[0] + (1.0 + g1_ref[0]) * y
    xn = _layer_norm_rows(z, lng_ref[...], lnb_ref[...])
    xo_ref[0] = xn
    h2 = xn * (1.0 + sc2_ref[0]) + sh2_ref[0]
    _store_token_tiles(h2_ref, h2)
    h_hi, h_lo = _split_bf16(h2)
    t = jnp.dot(h_hi, wr_ref[...], preferred_element_type=F32)
    lg = (t[:, :LANES] + t[:, LANES:] + jnp.dot(h_lo, wr_ref[:, :LANES], preferred_element_type=F32)
          + br_ref[...])
    route, m1, m2 = _router(lg)
    mem = jnp.where(m1 | m2, 1.0, 0.0)
    rank = run_cnt[0:1, :] + jnp.dot(before_ref[...], mem.astype(BF16), preferred_element_type=F32)
    rank1 = jnp.sum(jnp.where(m1, rank, 0.0), axis=-1, keepdims=True)
    rank2 = jnp.sum(jnp.where(m2, rank, 0.0), axis=-1, keepdims=True)
    lane = lax.broadcasted_iota(jnp.int32, route.shape, 1)
    route_ref[0] = jnp.where(lane == 4, rank1, jnp.where(lane == 5, rank2, route))
    run_cnt[...] = run_cnt[...] + jnp.sum(mem, axis=0, keepdims=True)
    cnt_ref[...] = run_cnt[...]


def _post_attn(a, wo_bf16, x, g1, sc2, sh2, ln_g, ln_b, wr, br, before, alpha):
    b, s, d = x.shape
    sub = d // LANES
    tiles_per_batch = s // ROW_TILE
    const = lambda i, j: (0, 0)
    tile = lambda i, j: (i, j, 0)
    per_b = lambda i, j: (i, 0, 0)
    return pl.pallas_call(
        functools.partial(_post_attn_kernel, alpha=alpha),
        grid=(b, s // ROW_TILE),
        in_specs=[
            pl.BlockSpec((1, ROW_TILE, d), tile),
            pl.BlockSpec((d, d), const),
            pl.BlockSpec((1, ROW_TILE, d), tile),
            pl.BlockSpec((1, 1, d), per_b),
            pl.BlockSpec((1, 1, d), per_b),
            pl.BlockSpec((1, 1, d), per_b),
            pl.BlockSpec((1, d), const),
            pl.BlockSpec((1, d), const),
            pl.BlockSpec((d, 2 * LANES), const),
            pl.BlockSpec((1, LANES), const),
            pl.BlockSpec((ROW_TILE, ROW_TILE), const),
        ],
        out_specs=[
            pl.BlockSpec((1, ROW_TILE, d), tile),
            pl.BlockSpec((ROW_TILE * sub, LANES), lambda i, j: (i * tiles_per_batch + j, 0)),
            pl.BlockSpec((1, ROW_TILE, LANES), tile),
            pl.BlockSpec((8, LANES), const),
        ],
        out_shape=[
            jax.ShapeDtypeStruct((b, s, d), F32),
            jax.ShapeDtypeStruct((b * s * sub, LANES), F32),
            jax.ShapeDtypeStruct((b, s, LANES), F32),
            jax.ShapeDtypeStruct((8, LANES), F32),
        ],
        scratch_shapes=[pltpu.VMEM((8, LANES), F32)],
        compiler_params=_cparams(("arbitrary", "arbitrary")),
        name="post_attn",
    )(a, wo_bf16, x, g1, sc2, sh2, ln_g, ln_b, wr, br, before)


def _dispatch_kernel(pos_ref, h_ref, xs_hbm, sem):
    tm = DISPATCH_TM
    sub = h_ref.shape[0] // tm

    def issue(jb, carry):
        for u in range(DMA_UNROLL):
            j = jb * DMA_UNROLL + u
            src = h_ref.at[pl.ds(pl.multiple_of(j * sub, sub), sub)]
            for k in range(TOP_K):
                row = pos_ref[0, 0, TOP_K * j + k]
                dst = xs_hbm.at[pl.ds(pl.multiple_of(row * sub, sub), sub)]
                pltpu.make_async_copy(src, dst, sem).start(priority=k % 2)
        return carry

    lax.fori_loop(0, tm // DMA_UNROLL, issue, 0)
    for _ in range(TOP_K):
        pltpu.make_async_copy(h_ref, xs_hbm.at[pl.ds(0, tm * sub)], sem).wait()


def _dispatch(pos, h2_tiles, n_tok):
    sub = h2_tiles.shape[0] // n_tok
    tm = DISPATCH_TM
    n_tiles = n_tok // tm
    return pl.pallas_call(
        _dispatch_kernel,
        grid=(n_tiles,),
        in_specs=[
            pl.BlockSpec((1, 1, TOP_K * tm), lambda i: (i, 0, 0), memory_space=pltpu.SMEM),
            pl.BlockSpec((tm * sub, LANES), lambda i: (i, 0)),
        ],
        out_specs=pl.BlockSpec(memory_space=pl.ANY),
        out_shape=jax.ShapeDtypeStruct((n_tok * TOP_K * sub, LANES), F32),
        scratch_shapes=[pltpu.SemaphoreType.DMA],
        compiler_params=_cparams(("arbitrary",)),
        name="moe_dispatch",
    )(pos.reshape(n_tiles, 1, TOP_K * tm), h2_tiles)


def _expert_kernel(blk_ref, exp_ref, lo_ref, hi_ref, x_ref, wg_ref, wu_ref, wd_ref, y_ref,
                   acc, wgb, wub, wdb, cast_expert, state):
    i = pl.program_id(0)
    n_items = pl.num_programs(0)
    tb = MOE_TB
    prev = jnp.maximum(i - 1, 0)
    nxt = jnp.minimum(i + 1, n_items - 1)
    first = (i == 0) | (blk_ref[i] != blk_ref[prev])
    last = (i == n_items - 1) | (blk_ref[i] != blk_ref[nxt])
    lo, hi = lo_ref[i], hi_ref[i]
    whole = (lo == 0) & (hi == tb)

    @pl.when(i == 0)
    def _():
        cast_expert[0] = -1

    @pl.when(first)
    def _():
        state[0] = 0
        state[1] = 0

    @pl.when(hi > lo)
    def _():
        @pl.when(cast_expert[0] != exp_ref[i])
        def _():
            wgb[...] = wg_ref[0, 0].astype(BF16)
            wub[...] = wu_ref[0, 0].astype(BF16)
            wdb[...] = wd_ref[0, 0].astype(BF16)
            cast_expert[0] = exp_ref[i]

        xb = _load_token_tiles(x_ref, tb, BF16)
        g = jnp.dot(xb, wgb[...], preferred_element_type=F32)
        u = jnp.dot(xb, wub[...], preferred_element_type=F32)
        act = (g * jax.nn.sigmoid(g) * u).astype(BF16)
        y = jnp.dot(act, wdb[...], preferred_element_type=F32)

        @pl.when(whole)
        def _():
            _store_token_tiles(y_ref, y)
            state[0] = 1

        @pl.when(jnp.logical_not(whole))
        def _():
            row = lax.broadcasted_iota(jnp.int32, (tb, 1), 0)
            part = jnp.where((row >= lo) & (row < hi), y, 0.0)

            @pl.when(state[1] == 0)
            def _():
                acc[...] = part

            @pl.when(state[1] != 0)
            def _():
                acc[...] += part

            state[1] = 1

    @pl.when(last & (state[0] == 0))
    def _():
        _store_token_tiles(y_ref, acc[...])


def _expert_mlps(items, xs_tiles, w_gate, w_up, w_down, layer):
    item_blk, item_exp, item_lo, item_hi = items
    n_items = item_blk.shape[0]
    _, _, d, de = w_gate.shape
    sub = d // LANES
    w_index = lambda i, blk, ex, lo, hi: (layer, ex[i], 0, 0)
    grid_spec = pltpu.PrefetchScalarGridSpec(
        num_scalar_prefetch=4,
        grid=(n_items,),
        in_specs=[
            pl.BlockSpec((MOE_TB * sub, LANES), lambda i, blk, ex, lo, hi: (blk[i], 0)),
            pl.BlockSpec((1, 1, d, de), w_index),
            pl.BlockSpec((1, 1, d, de), w_index),
            pl.BlockSpec((1, 1, de, d), w_index),
        ],
        out_specs=pl.BlockSpec((MOE_TB * sub, LANES), lambda i, blk, ex, lo, hi: (blk[i], 0)),
        scratch_shapes=[
            pltpu.VMEM((MOE_TB, d), F32),
            pltpu.VMEM((d, de), BF16),
            pltpu.VMEM((d, de), BF16),
            pltpu.VMEM((de, d), BF16),
            pltpu.SMEM((1,), jnp.int32),
            pltpu.SMEM((2,), jnp.int32),
        ],
    )
    return pl.pallas_call(
        _expert_kernel,
        grid_spec=grid_spec,
        out_shape=jax.ShapeDtypeStruct(xs_tiles.shape, F32),
        compiler_params=_cparams(("arbitrary",)),
        name="expert_mlps",
    )(item_blk, item_exp, item_lo, item_hi, xs_tiles, w_gate, w_up, w_down)


def _combine_kernel(pos_ref, pos_next_ref, ys_hbm, x_ref, route_ref, g2_ref, lng_ref, lnb_ref, xo_ref, buf, sems,
                    *, alpha):
    i = pl.program_id(0)
    n = pl.num_programs(0)
    tm = COMBINE_TM
    sub = buf.shape[2] // tm
    slot = i % 2

    def gather(p_ref, dst_slot):
        def issue(jb, carry):
            for u in range(DMA_UNROLL):
                j = jb * DMA_UNROLL + u
                for k in range(TOP_K):
                    _tile_copy(ys_hbm, buf.at[dst_slot, k], p_ref[0, 0, TOP_K * j + k], j, sub,
                               sems.at[dst_slot]).start(priority=k % 2)
            return carry

        lax.fori_loop(0, tm // DMA_UNROLL, issue, 0)

    @pl.when(i == 0)
    def _():
        gather(pos_ref, slot)

    @pl.when(i + 1 < n)
    def _():
        gather(pos_next_ref, 1 - slot)

    route = route_ref[...]
    lane = lax.broadcasted_iota(jnp.int32, route.shape, 1)
    gates = [jnp.sum(jnp.where(lane == k, route, 0.0), axis=-1, keepdims=True) for k in range(TOP_K)]
    for k in range(TOP_K):
        pltpu.make_async_copy(ys_hbm.at[pl.ds(0, tm * sub)], buf.at[slot, k], sems.at[slot]).wait()
    y = _load_token_tiles(buf.at[slot, 0], tm, F32) * gates[0]
    for k in range(1, TOP_K):
        y = y + _load_token_tiles(buf.at[slot, k], tm, F32) * gates[k]
    z = alpha * x_ref[...] + (1.0 + g2_ref[0]) * y
    xo_ref[...] = _layer_norm_rows(z, lng_ref[...], lnb_ref[...])


def _combine(pos, ys, x2d, route2d, g2, ln_g, ln_b, seq, alpha):
    n_tok, d = x2d.shape
    tm = COMBINE_TM
    n_tiles = n_tok // tm
    tiles_per_batch = seq // tm
    const = lambda i: (0, 0)
    pos3 = pos.reshape(n_tiles, 1, TOP_K * tm)
    return pl.pallas_call(
        functools.partial(_combine_kernel, alpha=alpha),
        grid=(n_tiles,),
        in_specs=[
            pl.BlockSpec((1, 1, TOP_K * tm), lambda i: (i, 0, 0), memory_space=pltpu.SMEM),
            pl.BlockSpec((1, 1, TOP_K * tm), lambda i: (jnp.minimum(i + 1, n_tiles - 1), 0, 0),
                         memory_space=pltpu.SMEM),
            pl.BlockSpec(memory_space=pl.ANY),
            pl.BlockSpec((tm, d), lambda i: (i, 0)),
            pl.BlockSpec((tm, LANES), lambda i: (i, 0)),
            pl.BlockSpec((1, 1, d), lambda i: (i // tiles_per_batch, 0, 0)),
            pl.BlockSpec((1, d), const),
            pl.BlockSpec((1, d), const),
        ],
        out_specs=pl.BlockSpec((tm, d), lambda i: (i, 0)),
        out_shape=jax.ShapeDtypeStruct((n_tok, d), F32),
        scratch_shapes=[
            pltpu.VMEM((2, TOP_K, tm * (d // LANES), LANES), F32),
            pltpu.SemaphoreType.DMA((2,)),
        ],
        compiler_params=_cparams(("arbitrary",)),
        name="moe_combine",
    )(pos3, pos3, ys, x2d, route2d, g2, ln_g, ln_b)


def _combine_project_kernel(pos_ref, pos_next_ref, ys_hbm, x_ref, route_ref, g2_ref, lng_ref, lnb_ref,
                            sc_ref, sh_ref, w_ref, *rest, alpha, d_model, gqa):
    xo_ref, qkv_ref, buf, sems = rest[-4:]
    i = pl.program_id(0)
    n = pl.num_programs(0)
    tm = ROW_TILE
    sub = buf.shape[2] // tm
    slot = i % 2

    def tile_gather(p_ref, dst_slot, j, k):
        return _tile_copy(ys_hbm, buf.at[dst_slot, k], p_ref[0, 0, TOP_K * j + k], j, sub, sems.at[dst_slot])

    @pl.when(i == 0)
    def _():
        def issue(jb, carry):
            for u in range(DMA_UNROLL):
                for k in range(TOP_K):
                    tile_gather(pos_ref, slot, jb * DMA_UNROLL + u, k).start(priority=k % 2)
            return carry

        lax.fori_loop(0, tm // DMA_UNROLL, issue, 0)

    route = route_ref[...]
    lane = lax.broadcasted_iota(jnp.int32, route.shape, 1)
    gates = [jnp.sum(jnp.where(lane == k, route, 0.0), axis=-1, keepdims=True) for k in range(TOP_K)]
    for k in range(TOP_K):
        pltpu.make_async_copy(ys_hbm.at[pl.ds(0, tm * sub)], buf.at[slot, k], sems.at[slot]).wait()
    y = _load_token_tiles(buf.at[slot, 0], tm, F32) * gates[0]
    for k in range(1, TOP_K):
        y = y + _load_token_tiles(buf.at[slot, k], tm, F32) * gates[k]
    z = alpha * x_ref[...] + (1.0 + g2_ref[0]) * y
    xn = _layer_norm_rows(z, lng_ref[...], lnb_ref[...])
    xo_ref[...] = xn
    h = (xn * (1.0 + sc_ref[0]) + sh_ref[0]).astype(BF16)

    for j in range(tm):
        for k in range(TOP_K):
            tile_gather(pos_next_ref, 1 - slot, j, k).start(priority=k % 2)

    if gqa:
        _project_gqa(h, w_ref, *rest[:6], qkv_ref, d_model)
    else:
        _project_na(h, w_ref, qkv_ref, d_model)

    @pl.when(i == n - 1)
    def _():
        for k in range(TOP_K):
            pltpu.make_async_copy(ys_hbm.at[pl.ds(0, tm * sub)], buf.at[1 - slot, k], sems.at[1 - slot]).wait()


def _combine_project(pos, ys, x2d, route2d, g2, ln_g, ln_b, sc, sh, w_bf16, gqa_extras, seq, alpha):
    n_tok, d = x2d.shape
    n = w_bf16.shape[1]
    tm = ROW_TILE
    n_tiles = n_tok // tm
    tiles_per_batch = seq // tm
    const = lambda i: (0, 0)
    per_b = lambda i: (i // tiles_per_batch, 0, 0)
    pos3 = pos.reshape(n_tiles, 1, TOP_K * tm)
    extra_specs = []
    if gqa_extras:
        extra_specs = [
            pl.BlockSpec((1, QKV_CHUNK), const),
            pl.BlockSpec((1, QKV_CHUNK), const),
            pl.BlockSpec((tm, QKV_CHUNK), lambda i: (i % tiles_per_batch, 0)),
            pl.BlockSpec((tm, QKV_CHUNK), lambda i: (i % tiles_per_batch, 0)),
            pl.BlockSpec((QKV_CHUNK, LANES), const),
            pl.BlockSpec((LANES, QKV_CHUNK), const),
        ]
    return pl.pallas_call(
        functools.partial(_combine_project_kernel, alpha=alpha, d_model=d, gqa=bool(gqa_extras)),
        grid=(n_tiles,),
        in_specs=[
            pl.BlockSpec((1, 1, TOP_K * tm), lambda i: (i, 0, 0), memory_space=pltpu.SMEM),
            pl.BlockSpec((1, 1, TOP_K * tm), lambda i: (jnp.minimum(i + 1, n_tiles - 1), 0, 0),
                         memory_space=pltpu.SMEM),
            pl.BlockSpec(memory_space=pl.ANY),
            pl.BlockSpec((tm, d), lambda i: (i, 0)),
            pl.BlockSpec((tm, LANES), lambda i: (i, 0)),
            pl.BlockSpec((1, 1, d), per_b),
            pl.BlockSpec((1, d), const),
            pl.BlockSpec((1, d), const),
            pl.BlockSpec((1, 1, d), per_b),
            pl.BlockSpec((1, 1, d), per_b),
            pl.BlockSpec((d, n), const),
        ] + extra_specs,
        out_specs=[
            pl.BlockSpec((tm, d), lambda i: (i, 0)),
            pl.BlockSpec((tm, n), lambda i: (i, 0)),
        ],
        out_shape=[
            jax.ShapeDtypeStruct((n_tok, d), F32),
            jax.ShapeDtypeStruct((n_tok, n), BF16),
        ],
        scratch_shapes=[
            pltpu.VMEM((2, TOP_K, tm * (d // LANES), LANES), F32),
            pltpu.SemaphoreType.DMA((2,)),
        ],
        compiler_params=_cparams(("arbitrary",)),
        name="combine_qkv_gqa" if gqa_extras else "combine_qkv_na",
    )(pos3, pos3, ys, x2d, route2d, g2, ln_g, ln_b, sc, sh, w_bf16, *gqa_extras)


def _routing_tables(route2d, cnt):
    n_tok = route2d.shape[0]
    n_rows = n_tok * TOP_K
    counts = cnt[0, N_GROUPS:N_GROUPS + N_EXPERTS].astype(jnp.int32)
    seg_end = jnp.cumsum(counts)
    experts = route2d[:, 2:2 + TOP_K].astype(jnp.int32)
    ranks = route2d[:, 4:4 + TOP_K].astype(jnp.int32)
    is_e = experts[:, :, None] == jnp.arange(N_EXPERTS, dtype=jnp.int32)[None, None, :]
    pos = (ranks + jnp.sum(jnp.where(is_e, (seg_end - counts)[None, None, :], 0), axis=-1)).reshape(n_rows)
    cuts = jnp.concatenate([jnp.arange(n_rows // MOE_TB, dtype=jnp.int32) * MOE_TB, seg_end - counts])
    n_items = cuts.shape[0]
    idx = jnp.arange(n_items, dtype=jnp.int32)
    rank = (jnp.sum(cuts[None, :] < cuts[:, None], axis=1)
            + jnp.sum((cuts[None, :] == cuts[:, None]) & (idx[None, :] < idx[:, None]), axis=1))
    lo_abs = jnp.sum(jnp.where(rank[None, :] == idx[:, None], cuts[None, :], 0), axis=1)
    hi_abs = jnp.concatenate([lo_abs[1:], jnp.full((1,), n_rows, jnp.int32)])
    item_blk = jnp.minimum(lo_abs // MOE_TB, n_rows // MOE_TB - 1)
    item_exp = jnp.minimum(jnp.sum(seg_end[None, :] <= lo_abs[:, None], axis=1), N_EXPERTS - 1)
    item_lo = lo_abs - item_blk * MOE_TB
    item_hi = hi_abs - item_blk * MOE_TB
    items = tuple(a.astype(jnp.int32) for a in (item_blk, item_exp, item_lo, item_hi))
    return pos, items


def _rope_tables(seq, width):
    t = jnp.arange(seq, dtype=jnp.int32)
    pos = jnp.stack([t // GRID_W, t % GRID_W], axis=-1).astype(F32)
    inv_freq = ROPE_THETA ** (-jnp.arange(ROPE_PAIRS, dtype=F32) / ROPE_PAIRS)
    ang = pos[:, :, None] * inv_freq
    cos, sin = jnp.cos(ang), jnp.sin(ang)
    cosf = jnp.stack([cos, cos], axis=2).reshape(seq, HEAD_DIM)
    sinf = jnp.stack([-sin, sin], axis=2).reshape(seq, HEAD_DIM)
    reps = width // HEAD_DIM
    return jnp.tile(cosf, (1, reps)), jnp.tile(sinf, (1, reps))


def _dup_heads(w):
    d = w.shape[0]
    w4 = w.reshape(d, GQA_KV_HEADS, 1, HEAD_DIM)
    return jnp.concatenate([w4, w4], axis=2).reshape(d, GQA_KV_HEADS * HEAD_PAIR)


def kernel(x, c, ada_w, ada_b, ln_g, ln_b, na_w_qkv, na_rpb, na_w_o, gqa_w_qkv, gqa_q_norm, gqa_k_norm, gqa_w_o,
           moe_w_group, moe_b_group, moe_w_expert, moe_b_expert, moe_w_gate, moe_w_up, moe_w_down):
    b, s, d = x.shape
    depth = ada_w.shape[0]
    n_tok = b * s
    rows = s // GRID_W
    kvd = GQA_KV_HEADS * HEAD_DIM
    alpha = (2 * depth) ** 0.25

    mod = _ada_modulation(c, ada_w, ada_b).reshape(depth, b, 6, 1, d)

    cosf, sinf = _rope_tables(s, QKV_CHUNK)
    head_of_lane = jnp.arange(QKV_CHUNK, dtype=jnp.int32) // HEAD_DIM
    seg_t = (head_of_lane[None, :] == jnp.arange(LANES, dtype=jnp.int32)[:, None])
    seg = (seg_t.T.astype(F32) / HEAD_DIM).astype(BF16)
    seg_t = seg_t.astype(BF16)
    tok = jnp.arange(ROW_TILE, dtype=jnp.int32)
    before = (tok[None, :] < tok[:, None]).astype(BF16)

    def qkv_params(i):
        j = i // 2
        if i % 2 == 0:
            return na_w_qkv[j].astype(BF16), ()
        w = gqa_w_qkv[j]
        w_all = jnp.concatenate([w[:, :d], _dup_heads(w[:, d:d + kvd]), _dup_heads(w[:, d + kvd:])], axis=1)
        reps = QKV_CHUNK // HEAD_DIM
        return w_all.astype(BF16), (jnp.tile(gqa_q_norm[j], reps)[None, :], jnp.tile(gqa_k_norm[j], reps)[None, :],
                                    cosf, sinf, seg, seg_t)

    qkv = None
    for i in range(depth):
        sh1, sc1, g1, sh2, sc2, g2 = (mod[i, :, k] for k in range(6))
        j = i // 2
        if qkv is None:
            w_bf16, extras = qkv_params(i)
            qkv = _qkv_gqa(x, sc1, sh1, w_bf16, *extras) if extras else _qkv_na(x, sc1, sh1, w_bf16)
        if i % 2 == 0:
            attn = _na_attention(qkv, _na_bias_table(na_rpb[j]), rows)
            w_o = na_w_o[j]
        else:
            attn = _gqa_attention(qkv, d, gqa_q_norm[j], gqa_k_norm[j])
            w_o = gqa_w_o[j]

        w_r = jnp.concatenate([moe_w_group[i], moe_w_expert[i]], axis=1)
        w_r = jnp.pad(w_r, ((0, 0), (0, LANES - w_r.shape[1])))
        wr_hi, wr_lo = _split_bf16(w_r)
        b_r = jnp.concatenate([moe_b_group[i], moe_b_expert[i]])
        b_r = jnp.pad(b_r, (0, LANES - b_r.shape[0]))[None, :]
        x, h2, route, cnt = _post_attn(attn, w_o.astype(BF16), x, g1, sc2, sh2, ln_g[i, 0][None, :],
                                       ln_b[i, 0][None, :], jnp.concatenate([wr_hi, wr_lo], axis=1), b_r, before,
                                       alpha)

        route2d = route.reshape(n_tok, LANES)
        pos, items = _routing_tables(route2d, cnt)
        xs = _dispatch(pos, h2, n_tok)
        ys = _expert_mlps(items, xs, moe_w_gate, moe_w_up, moe_w_down, i)
        lng, lnb = ln_g[i, 1][None, :], ln_b[i, 1][None, :]
        if i + 1 < depth:
            w_bf16, extras = qkv_params(i + 1)
            x2d, qkv2d = _combine_project(pos, ys, x.reshape(n_tok, d), route2d, g2, lng, lnb,
                                          mod[i + 1, :, 1], mod[i + 1, :, 0], w_bf16, extras, s, alpha)
            x, qkv = x2d.reshape(b, s, d), qkv2d.reshape(b, s, -1)
        else:
            x = _combine(pos, ys, x.reshape(n_tok, d), route2d, g2, lng, lnb, s, alpha).reshape(b, s, d)
    return x
```

```python
import functools

import jax
import jax.numpy as jnp
from jax import lax
from jax.experimental import pallas as pl
from jax.experimental.pallas import tpu as pltpu

GRID_W = 64
N_HEADS = 16
HEAD_DIM = 64
NA_ROWS = 8
NA_COLS = 16
GQA_KV_HEADS = 4
ROPE_THETA = 10000.0
ROPE_PAIRS = HEAD_DIM // 4
N_GROUPS = 4
EXPERTS_PER_GROUP = 8
N_EXPERTS = N_GROUPS * EXPERTS_PER_GROUP
TOP_K = 2
LN_EPS = 1e-5
RMS_EPS = 1e-6

LANES = 128
TOKEN_ROWS = 8
HEAD_PAIR = 2 * HEAD_DIM
VMEM_LIMIT_BYTES = 52 * 1024 * 1024
ADA_TN = 1536
ROW_TILE = 512
QKV_CHUNK = 512
GQA_TQ = 1024
NA_ROWS_PER_STEP = 4
NA_STAGE_PAIRS = 4
MOE_TB = 256
DISPATCH_TM = 2048
COMBINE_TM = 1024
DMA_UNROLL = 8
NEG_BIG = -1e30
LOG2_E = 1.4426950408889634
Q_SCALE = (HEAD_DIM ** -0.5) * LOG2_E
GQA_BOUND_MARGIN = 1.02
GQA_BOUND_MAX = 50.0

F32 = jnp.float32
BF16 = jnp.bfloat16


def _cparams(sem):
    return pltpu.CompilerParams(dimension_semantics=sem, vmem_limit_bytes=VMEM_LIMIT_BYTES)


def _split_bf16(a):
    hi = a.astype(BF16)
    lo = (a - hi.astype(F32)).astype(BF16)
    return hi, lo


def _ada_kernel(c_ref, w_ref, b_ref, o_ref):
    c = c_ref[...]
    ca = (c * jax.nn.sigmoid(c)).astype(BF16)
    o_ref[0] = jnp.dot(ca, w_ref[0].astype(BF16), preferred_element_type=F32) + b_ref[0]


def _ada_modulation(c, ada_w, ada_b):
    depth, d, n = ada_w.shape
    b = c.shape[0]
    return pl.pallas_call(
        _ada_kernel,
        grid=(depth, n // ADA_TN),
        in_specs=[
            pl.BlockSpec((b, d), lambda i, j: (0, 0)),
            pl.BlockSpec((1, d, ADA_TN), lambda i, j: (i, 0, j)),
            pl.BlockSpec((1, 1, ADA_TN), lambda i, j: (i, 0, j)),
        ],
        out_specs=pl.BlockSpec((1, b, ADA_TN), lambda i, j: (i, 0, j)),
        out_shape=jax.ShapeDtypeStruct((depth, b, n), F32),
        compiler_params=_cparams(("arbitrary", "arbitrary")),
        name="ada_modulation",
    )(c, ada_w, ada_b.reshape(depth, 1, n))


def _project_na(h, w_ref, o_ref, d_model):
    n = w_ref.shape[1]
    for n0 in range(0, n, QKV_CHUNK):
        acc = jnp.dot(h, w_ref[:, n0:n0 + QKV_CHUNK], preferred_element_type=F32)
        if n0 < d_model:
            acc = acc * Q_SCALE
        o_ref[:, n0:n0 + QKV_CHUNK] = acc.astype(BF16)


def _qkv_na_kernel(x_ref, sc_ref, sh_ref, w_ref, o_ref, *, d_model):
    h = (x_ref[0] * (1.0 + sc_ref[0]) + sh_ref[0]).astype(BF16)
    _project_na(h, w_ref, o_ref.at[0], d_model)


def _qkv_na(x, sc, sh, w_bf16):
    b, s, d = x.shape
    n = w_bf16.shape[1]
    return pl.pallas_call(
        functools.partial(_qkv_na_kernel, d_model=d),
        grid=(b, s // ROW_TILE),
        in_specs=[
            pl.BlockSpec((1, ROW_TILE, d), lambda i, j: (i, j, 0)),
            pl.BlockSpec((1, 1, d), lambda i, j: (i, 0, 0)),
            pl.BlockSpec((1, 1, d), lambda i, j: (i, 0, 0)),
            pl.BlockSpec((d, n), lambda i, j: (0, 0)),
        ],
        out_specs=pl.BlockSpec((1, ROW_TILE, n), lambda i, j: (i, j, 0)),
        out_shape=jax.ShapeDtypeStruct((b, s, n), BF16),
        compiler_params=_cparams(("arbitrary", "arbitrary")),
        name="qkv_na",
    )(x, sc, sh, w_bf16)


N_DR = 2 * NA_ROWS - 1
N_DC = 2 * NA_COLS - 1


def _na_bias_kernel(r_ref, m_ref, valid_ref, o_ref):
    r = r_ref[...]
    hi = r.astype(BF16)
    r1 = r - hi.astype(F32)
    mid = r1.astype(BF16)
    lo = (r1 - mid.astype(F32)).astype(BF16)
    m = m_ref[...]
    acc = (jnp.dot(hi, m, preferred_element_type=F32) + jnp.dot(mid, m, preferred_element_type=F32)
           + jnp.dot(lo, m, preferred_element_type=F32))
    o_ref[...] = jnp.where(valid_ref[...] > 0.0, acc * LOG2_E, NEG_BIG)


def _na_bias_table(rpb):
    qc = jnp.arange(GRID_W, dtype=jnp.int32)
    kc = jnp.arange(GRID_W, dtype=jnp.int32)
    win_start = jnp.clip(qc - NA_COLS // 2, 0, GRID_W - NA_COLS)
    valid = (kc[None, :] >= win_start[:, None]) & (kc[None, :] < win_start[:, None] + NA_COLS)
    dc = jnp.clip(kc[None, :] - qc[:, None] + NA_COLS - 1, 0, N_DC - 1)
    onehot = (jnp.arange(LANES, dtype=jnp.int32)[:, None] == dc.reshape(1, GRID_W * GRID_W)).astype(BF16)
    r2d = jnp.pad(rpb.reshape(N_HEADS * N_DR, N_DC).astype(F32), ((0, 0), (0, LANES - N_DC)))
    tiles = pl.pallas_call(
        _na_bias_kernel,
        out_shape=jax.ShapeDtypeStruct((N_HEADS * N_DR, GRID_W * GRID_W), F32),
        compiler_params=pltpu.CompilerParams(vmem_limit_bytes=VMEM_LIMIT_BYTES),
        name="na_bias_tiles",
    )(r2d, onehot, valid.reshape(1, GRID_W * GRID_W).astype(F32))
    tiles = tiles.reshape(N_HEADS, N_DR, GRID_W, GRID_W)
    pairs = jnp.concatenate([tiles[:, :N_DR - 1], tiles[:, 1:]], axis=-1)
    pairs = pairs.reshape(N_HEADS // 2, 2, N_DR - 1, GRID_W, HEAD_PAIR).transpose(0, 2, 1, 3, 4)
    return pairs.reshape(N_HEADS // 2, N_DR - 1, 2 * GRID_W, HEAD_PAIR)


def _na_kernel(q_ref, k_ref, v_ref, bias_ref, o_ref, *, rows):
    nk = NA_ROWS * GRID_W
    lane = lax.broadcasted_iota(jnp.int32, (GRID_W, HEAD_PAIR), 1)
    first = lane < HEAD_DIM
    for sub_row in range(NA_ROWS_PER_STEP):
        r = pl.program_id(1) * NA_ROWS_PER_STEP + sub_row
        row_start = jnp.clip(r - NA_ROWS // 2, 0, rows - NA_ROWS)
        start = pl.multiple_of(row_start * GRID_W, GRID_W)
        dr_base = row_start - r + (NA_ROWS - 1)
        q_rows = slice(sub_row * GRID_W, (sub_row + 1) * GRID_W)
        for hp0 in range(0, N_HEADS // 2, NA_STAGE_PAIRS):
            pairs = range(hp0, hp0 + NA_STAGE_PAIRS)
            cols = {hp: slice(hp * HEAD_PAIR, (hp + 1) * HEAD_PAIR) for hp in pairs}
            s_all = {}
            for hp in pairs:
                q2 = q_ref[0, q_rows, cols[hp]]
                k2 = k_ref[0, pl.ds(start, nk), cols[hp]]
                zero = jnp.zeros_like(q2)
                qs = jnp.concatenate([jnp.where(first, q2, zero), jnp.where(first, zero, q2)], axis=0)
                s = lax.dot_general(qs, k2, (((1,), (1,)), ((), ())), preferred_element_type=F32)
                s_all[hp] = s + jnp.concatenate([bias_ref[hp, dr_base + 2 * m] for m in range(NA_ROWS // 2)],
                                                axis=1)
            m_all = {hp: jnp.max(s_all[hp], axis=-1, keepdims=True) for hp in pairs}
            p_all = {hp: jnp.exp2(s_all[hp] - m_all[hp]) for hp in pairs}
            l_all = {hp: jnp.sum(p_all[hp], axis=-1, keepdims=True) for hp in pairs}
            o_all = {hp: jnp.dot(p_all[hp].astype(BF16), v_ref[0, pl.ds(start, nk), cols[hp]],
                                 preferred_element_type=F32) for hp in pairs}
            for hp in pairs:
                o = o_all[hp] / l_all[hp]
                o_ref[0, q_rows, cols[hp]] = jnp.where(first, o[:GRID_W], o[GRID_W:]).astype(BF16)


def _na_attention(qkv, bias, rows):
    b, s, n3 = qkv.shape
    d = n3 // 3
    assert rows >= NA_ROWS and NA_ROWS % 2 == 0 and rows % NA_ROWS_PER_STEP == 0
    q_tile = NA_ROWS_PER_STEP * GRID_W
    return pl.pallas_call(
        functools.partial(_na_kernel, rows=rows),
        grid=(b, rows // NA_ROWS_PER_STEP),
        in_specs=[
            pl.BlockSpec((1, q_tile, d), lambda i, r: (i, r, 0)),
            pl.BlockSpec((1, s, d), lambda i, r: (i, 0, 1)),
            pl.BlockSpec((1, s, d), lambda i, r: (i, 0, 2)),
            pl.BlockSpec(bias.shape, lambda i, r: (0, 0, 0, 0)),
        ],
        out_specs=pl.BlockSpec((1, q_tile, d), lambda i, r: (i, r, 0)),
        out_shape=jax.ShapeDtypeStruct((b, s, d), BF16),
        compiler_params=_cparams(("arbitrary", "arbitrary")),
        name="na_attention",
    )(qkv, qkv, qkv, bias)


def _rms_rope(a, gain, cosf, sinf, seg, seg_t):
    qq_hi, qq_lo = _split_bf16(a * a)
    ms = jnp.dot(qq_hi, seg, preferred_element_type=F32) + jnp.dot(qq_lo, seg, preferred_element_type=F32)
    r_hi, r_lo = _split_bf16(lax.rsqrt(ms + RMS_EPS))
    rf = jnp.dot(r_hi, seg_t, preferred_element_type=F32) + jnp.dot(r_lo, seg_t, preferred_element_type=F32)
    an = a * rf * gain
    n = a.shape[1]
    lane = lax.broadcasted_iota(jnp.int32, a.shape, 1)
    partner = jnp.where(lane % (2 * ROPE_PAIRS) < ROPE_PAIRS,
                        pltpu.roll(an, n - ROPE_PAIRS, 1), pltpu.roll(an, ROPE_PAIRS, 1))
    return an * cosf + partner * sinf


def _project_gqa(h, w_ref, qg_ref, kg_ref, cos_ref, sin_ref, seg_ref, segt_ref, o_ref, d_model):
    n = w_ref.shape[1]
    kv_cols = (n - d_model) // 2
    cosf, sinf = cos_ref[...], sin_ref[...]
    seg, seg_t = seg_ref[...], segt_ref[...]
    for n0 in range(0, n, QKV_CHUNK):
        acc = jnp.dot(h, w_ref[:, n0:n0 + QKV_CHUNK], preferred_element_type=F32)
        if n0 < d_model:
            acc = _rms_rope(acc, qg_ref[...], cosf, sinf, seg, seg_t) * Q_SCALE
        elif n0 < d_model + kv_cols:
            acc = _rms_rope(acc, kg_ref[...], cosf, sinf, seg, seg_t)
        else:
            lane = lax.broadcasted_iota(jnp.int32, acc.shape, 1)
            acc = jnp.where(lane % HEAD_PAIR < HEAD_DIM, acc, 1.0)
        o_ref[:, n0:n0 + QKV_CHUNK] = acc.astype(BF16)


def _qkv_gqa_kernel(x_ref, sc_ref, sh_ref, w_ref, qg_ref, kg_ref, cos_ref, sin_ref, seg_ref, segt_ref, o_ref,
                    *, d_model):
    h = (x_ref[0] * (1.0 + sc_ref[0]) + sh_ref[0]).astype(BF16)
    _project_gqa(h, w_ref, qg_ref, kg_ref, cos_ref, sin_ref, seg_ref, segt_ref, o_ref.at[0], d_model)


def _qkv_gqa(x, sc, sh, w_bf16, q_gain, k_gain, cosf, sinf, seg, seg_t):
    b, s, d = x.shape
    n = w_bf16.shape[1]
    const = lambda j, i: (0, 0)
    return pl.pallas_call(
        functools.partial(_qkv_gqa_kernel, d_model=d),
        grid=(s // ROW_TILE, b),
        in_specs=[
            pl.BlockSpec((1, ROW_TILE, d), lambda j, i: (i, j, 0)),
            pl.BlockSpec((1, 1, d), lambda j, i: (i, 0, 0)),
            pl.BlockSpec((1, 1, d), lambda j, i: (i, 0, 0)),
            pl.BlockSpec((d, n), const),
            pl.BlockSpec((1, QKV_CHUNK), const),
            pl.BlockSpec((1, QKV_CHUNK), const),
            pl.BlockSpec((ROW_TILE, QKV_CHUNK), lambda j, i: (j, 0)),
            pl.BlockSpec((ROW_TILE, QKV_CHUNK), lambda j, i: (j, 0)),
            pl.BlockSpec((QKV_CHUNK, LANES), const),
            pl.BlockSpec((LANES, QKV_CHUNK), const),
        ],
        out_specs=pl.BlockSpec((1, ROW_TILE, n), lambda j, i: (i, j, 0)),
        out_shape=jax.ShapeDtypeStruct((b, s, n), BF16),
        compiler_params=_cparams(("arbitrary", "arbitrary")),
        name="qkv_gqa",
    )(x, sc, sh, w_bf16, q_gain, k_gain, cosf, sinf, seg, seg_t)


def _gqa_kernel(bound_ref, q_ref, k_ref, v_ref, o_ref, *, use_bound):
    k2 = k_ref[0]
    v2 = v_ref[0]
    tq = q_ref.shape[1]
    lane = lax.broadcasted_iota(jnp.int32, (tq, HEAD_PAIR), 1)
    first = lane < HEAD_DIM
    for pair in range(q_ref.shape[2] // HEAD_PAIR):
        cols = slice(pair * HEAD_PAIR, (pair + 1) * HEAD_PAIR)
        q2 = q_ref[0, :, cols]
        zero = jnp.zeros_like(q2)
        qs = jnp.concatenate([jnp.where(first, q2, zero), jnp.where(first, zero, q2)], axis=0)
        s = lax.dot_general(qs, k2, (((1,), (1,)), ((), ())), preferred_element_type=F32)
        m = bound_ref[0] if use_bound else jnp.max(s, axis=-1, keepdims=True)
        p = jnp.exp2(s - m).astype(BF16)
        o = jnp.dot(p, v2, preferred_element_type=F32)
        o_a, o_b = o[:tq], o[tq:]
        out = jnp.where(first, o_a / pltpu.roll(o_a, HEAD_DIM, 1), pltpu.roll(o_b, HEAD_DIM, 1) / o_b)
        o_ref[0, :, cols] = out.astype(BF16)


def _gqa_logit_bound(q_gain, k_gain):
    return (GQA_BOUND_MARGIN * LOG2_E * (HEAD_DIM ** 0.5)
            * jnp.max(jnp.abs(q_gain)) * jnp.max(jnp.abs(k_gain))).astype(F32).reshape(1)


def _gqa_attention(qkv, d_model, q_gain, k_gain):
    bound = _gqa_logit_bound(q_gain, k_gain)
    return lax.cond(bound[0] <= GQA_BOUND_MAX,
                    functools.partial(_gqa_attention_call, d_model=d_model, use_bound=True),
                    functools.partial(_gqa_attention_call, d_model=d_model, use_bound=False),
                    bound, qkv)


def _gqa_attention_call(bound, qkv, *, d_model, use_bound):
    b, s, n = qkv.shape
    group_cols = d_model // GQA_KV_HEADS
    k_block0 = d_model // HEAD_PAIR
    v_block0 = k_block0 + GQA_KV_HEADS
    return pl.pallas_call(
        functools.partial(_gqa_kernel, use_bound=use_bound),
        grid=(b, GQA_KV_HEADS, s // GQA_TQ),
        in_specs=[
            pl.BlockSpec(memory_space=pltpu.SMEM),
            pl.BlockSpec((1, GQA_TQ, group_cols), lambda i, g, t: (i, t, g)),
            pl.BlockSpec((1, s, HEAD_PAIR), lambda i, g, t: (i, 0, k_block0 + g)),
            pl.BlockSpec((1, s, HEAD_PAIR), lambda i, g, t: (i, 0, v_block0 + g)),
        ],
        out_specs=pl.BlockSpec((1, GQA_TQ, group_cols), lambda i, g, t: (i, t, g)),
        out_shape=jax.ShapeDtypeStruct((b, s, d_model), BF16),
        compiler_params=_cparams(("arbitrary", "arbitrary", "arbitrary")),
        name="gqa_attention_bounded" if use_bound else "gqa_attention",
    )(bound, qkv, qkv, qkv)


def _store_token_tiles(ref, val, first_token=0):
    t, d = val.shape
    width = ref.shape[-1]
    sub = d // width
    for s in range(sub):
        ref[pl.ds(first_token * sub + s, t, stride=sub), :] = val[:, s * width:(s + 1) * width]


def _load_token_tiles(ref, t, dtype):
    sub = ref.shape[0] // t
    return jnp.concatenate([ref[pl.ds(s, t, stride=sub), :].astype(dtype) for s in range(sub)], axis=1)


def _tile_copy(src_hbm, dst_vmem, src_tok, dst_tok, sub, sem):
    return pltpu.make_async_copy(src_hbm.at[pl.ds(pl.multiple_of(src_tok * sub, sub), sub)],
                                 dst_vmem.at[pl.ds(pl.multiple_of(dst_tok * sub, sub), sub)], sem)


def _layer_norm_rows(z, g, b):
    mu = jnp.mean(z, axis=-1, keepdims=True)
    zc = z - mu
    var = jnp.mean(zc * zc, axis=-1, keepdims=True)
    return zc * lax.rsqrt(var + LN_EPS) * g + b


def _router(lg):
    lane = lax.broadcasted_iota(jnp.int32, lg.shape, 1)
    lanef = lane.astype(F32)
    far = float(LANES)
    gmask = lane < N_GROUPS
    lgm = jnp.where(gmask, lg, NEG_BIG)
    mg = jnp.max(lgm, axis=-1, keepdims=True)
    sg = jnp.sum(jnp.exp(lgm - mg), axis=-1, keepdims=True)
    g_prob = 1.0 / sg
    g_idx = jnp.min(jnp.where(lgm == mg, lanef, far), axis=-1, keepdims=True)
    e_lo = N_GROUPS + g_idx * EXPERTS_PER_GROUP
    emask = (lanef >= e_lo) & (lanef < e_lo + EXPERTS_PER_GROUP)
    lem = jnp.where(emask, lg, NEG_BIG)
    me = jnp.max(lem, axis=-1, keepdims=True)
    ee = jnp.exp(lem - me)
    pe = jnp.where(emask, ee / jnp.sum(ee, axis=-1, keepdims=True), -1.0)
    p1 = jnp.max(pe, axis=-1, keepdims=True)
    i1 = jnp.min(jnp.where(pe == p1, lanef, far), axis=-1, keepdims=True)
    pe2 = jnp.where(lanef == i1, -1.0, pe)
    p2 = jnp.max(pe2, axis=-1, keepdims=True)
    i2 = jnp.min(jnp.where(pe2 == p2, lanef, far), axis=-1, keepdims=True)
    den = p1 + p2
    gate1 = g_prob * (p1 / den)
    gate2 = g_prob * (p2 / den)
    out = jnp.where(lane == 0, gate1, 0.0)
    out = jnp.where(lane == 1, gate2, out)
    out = jnp.where(lane == 2, i1 - N_GROUPS, out)
    out = jnp.where(lane == 3, i2 - N_GROUPS, out)
    return out, lanef == i1, lanef == i2


def _post_attn_kernel(a_ref, wo_ref, x_ref, g1_ref, sc2_ref, sh2_ref, lng_ref, lnb_ref, wr_ref, br_ref,
                      before_ref, xo_ref, h2_ref, route_ref, cnt_ref, run_cnt, *, alpha):
    @pl.when((pl.program_id(0) == 0) & (pl.program_id(1) == 0))
    def _():
        run_cnt[...] = jnp.zeros_like(run_cnt)

    y = jnp.dot(a_ref[0], wo_ref[...], preferred_element_type=F32)
    z = alpha * x_ref[0] + (1.0 + g1_ref[0]) * y
    xn = _layer_norm_rows(z, lng_ref[...], lnb_ref[...])
    xo_ref[0] = xn
    h2 = xn * (1.0 + sc2_ref[0]) + sh2_ref[0]
    _store_token_tiles(h2_ref, h2)
    h_hi, h_lo = _split_bf16(h2)
    t = jnp.dot(h_hi, wr_ref[...], preferred_element_type=F32)
    lg = (t[:, :LANES] + t[:, LANES:] + jnp.dot(h_lo, wr_ref[:, :LANES], preferred_element_type=F32)
          + br_ref[...])
    route, m1, m2 = _router(lg)
    mem = jnp.where(m1 | m2, 1.0, 0.0)
    rank = run_cnt[0:1, :] + jnp.dot(before_ref[...], mem.astype(BF16), preferred_element_type=F32)
    rank1 = jnp.sum(jnp.where(m1, rank, 0.0), axis=-1, keepdims=True)
    rank2 = jnp.sum(jnp.where(m2, rank, 0.0), axis=-1, keepdims=True)
    lane = lax.broadcasted_iota(jnp.int32, route.shape, 1)
    route_ref[0] = jnp.where(lane == 4, rank1, jnp.where(lane == 5, rank2, route))
    run_cnt[...] = run_cnt[...] + jnp.sum(mem, axis=0, keepdims=True)
    cnt_ref[...] = run_cnt[...]


def _post_attn(a, wo_bf16, x, g1, sc2, sh2, ln_g, ln_b, wr, br, before, alpha):
    b, s, d = x.shape
    sub, width = TOKEN_ROWS, d // TOKEN_ROWS
    tiles_per_batch = s // ROW_TILE
    const = lambda i, j: (0, 0)
    tile = lambda i, j: (i, j, 0)
    per_b = lambda i, j: (i, 0, 0)
    return pl.pallas_call(
        functools.partial(_post_attn_kernel, alpha=alpha),
        grid=(b, s // ROW_TILE),
        in_specs=[
            pl.BlockSpec((1, ROW_TILE, d), tile),
            pl.BlockSpec((d, d), const),
            pl.BlockSpec((1, ROW_TILE, d), tile),
            pl.BlockSpec((1, 1, d), per_b),
            pl.BlockSpec((1, 1, d), per_b),
            pl.BlockSpec((1, 1, d), per_b),
            pl.BlockSpec((1, d), const),
            pl.BlockSpec((1, d), const),
            pl.BlockSpec((d, 2 * LANES), const),
            pl.BlockSpec((1, LANES), const),
            pl.BlockSpec((ROW_TILE, ROW_TILE), const),
        ],
        out_specs=[
            pl.BlockSpec((1, ROW_TILE, d), tile),
            pl.BlockSpec((ROW_TILE * sub, width), lambda i, j: (i * tiles_per_batch + j, 0)),
            pl.BlockSpec((1, ROW_TILE, LANES), tile),
            pl.BlockSpec((8, LANES), const),
        ],
        out_shape=[
            jax.ShapeDtypeStruct((b, s, d), F32),
            jax.ShapeDtypeStruct((b * s * sub, width), F32),
            jax.ShapeDtypeStruct((b, s, LANES), F32),
            jax.ShapeDtypeStruct((8, LANES), F32),
        ],
        scratch_shapes=[pltpu.VMEM((8, LANES), F32)],
        compiler_params=_cparams(("arbitrary", "arbitrary")),
        name="post_attn",
    )(a, wo_bf16, x, g1, sc2, sh2, ln_g, ln_b, wr, br, before)


def _dispatch_kernel(pos_ref, h_ref, xs_hbm, sem):
    tm = DISPATCH_TM
    sub = h_ref.shape[0] // tm

    def issue(jb, carry):
        for u in range(DMA_UNROLL):
            j = jb * DMA_UNROLL + u
            src = h_ref.at[pl.ds(pl.multiple_of(j * sub, sub), sub)]
            for k in range(TOP_K):
                row = pos_ref[0, 0, TOP_K * j + k]
                dst = xs_hbm.at[pl.ds(pl.multiple_of(row * sub, sub), sub)]
                pltpu.make_async_copy(src, dst, sem).start(priority=k % 2)
        return carry

    lax.fori_loop(0, tm // DMA_UNROLL, issue, 0)
    for _ in range(TOP_K):
        pltpu.make_async_copy(h_ref, xs_hbm.at[pl.ds(0, tm * sub)], sem).wait()


def _dispatch(pos, h2_tiles, n_tok):
    sub, width = h2_tiles.shape[0] // n_tok, h2_tiles.shape[1]
    tm = DISPATCH_TM
    n_tiles = n_tok // tm
    return pl.pallas_call(
        _dispatch_kernel,
        grid=(n_tiles,),
        in_specs=[
            pl.BlockSpec((1, 1, TOP_K * tm), lambda i: (i, 0, 0), memory_space=pltpu.SMEM),
            pl.BlockSpec((tm * sub, width), lambda i: (i, 0)),
        ],
        out_specs=pl.BlockSpec(memory_space=pl.ANY),
        out_shape=jax.ShapeDtypeStruct((n_tok * TOP_K * sub, width), F32),
        scratch_shapes=[pltpu.SemaphoreType.DMA],
        compiler_params=_cparams(("arbitrary",)),
        name="moe_dispatch",
    )(pos.reshape(n_tiles, 1, TOP_K * tm), h2_tiles)


def _expert_kernel(blk_ref, exp_ref, lo_ref, hi_ref, x_ref, wg_ref, wu_ref, wd_ref, y_ref,
                   acc, wgb, wub, wdb, cast_expert, state):
    i = pl.program_id(0)
    n_items = pl.num_programs(0)
    tb = MOE_TB
    prev = jnp.maximum(i - 1, 0)
    nxt = jnp.minimum(i + 1, n_items - 1)
    first = (i == 0) | (blk_ref[i] != blk_ref[prev])
    last = (i == n_items - 1) | (blk_ref[i] != blk_ref[nxt])
    lo, hi = lo_ref[i], hi_ref[i]
    whole = (lo == 0) & (hi == tb)

    @pl.when(i == 0)
    def _():
        cast_expert[0] = -1

    @pl.when(first)
    def _():
        state[0] = 0
        state[1] = 0

    @pl.when(hi > lo)
    def _():
        @pl.when(cast_expert[0] != exp_ref[i])
        def _():
            wgb[...] = wg_ref[0, 0].astype(BF16)
            wub[...] = wu_ref[0, 0].astype(BF16)
            wdb[...] = wd_ref[0, 0].astype(BF16)
            cast_expert[0] = exp_ref[i]

        xb = _load_token_tiles(x_ref, tb, BF16)
        g = jnp.dot(xb, wgb[...], preferred_element_type=F32)
        u = jnp.dot(xb, wub[...], preferred_element_type=F32)
        act = (g * jax.nn.sigmoid(g) * u).astype(BF16)
        y = jnp.dot(act, wdb[...], preferred_element_type=F32)

        @pl.when(whole)
        def _():
            _store_token_tiles(y_ref, y)
            state[0] = 1

        @pl.when(jnp.logical_not(whole))
        def _():
            row = lax.broadcasted_iota(jnp.int32, (tb, 1), 0)
            part = jnp.where((row >= lo) & (row < hi), y, 0.0)

            @pl.when(state[1] == 0)
            def _():
                acc[...] = part

            @pl.when(state[1] != 0)
            def _():
                acc[...] += part

            state[1] = 1

    @pl.when(last & (state[0] == 0))
    def _():
        _store_token_tiles(y_ref, acc[...])


def _expert_mlps(items, xs_tiles, w_gate, w_up, w_down, layer):
    item_blk, item_exp, item_lo, item_hi = items
    n_items = item_blk.shape[0]
    _, _, d, de = w_gate.shape
    sub, width = TOKEN_ROWS, d // TOKEN_ROWS
    w_index = lambda i, blk, ex, lo, hi: (layer, ex[i], 0, 0)
    grid_spec = pltpu.PrefetchScalarGridSpec(
        num_scalar_prefetch=4,
        grid=(n_items,),
        in_specs=[
            pl.BlockSpec((MOE_TB * sub, width), lambda i, blk, ex, lo, hi: (blk[i], 0)),
            pl.BlockSpec((1, 1, d, de), w_index),
            pl.BlockSpec((1, 1, d, de), w_index),
            pl.BlockSpec((1, 1, de, d), w_index),
        ],
        out_specs=pl.BlockSpec((MOE_TB * sub, width), lambda i, blk, ex, lo, hi: (blk[i], 0)),
        scratch_shapes=[
            pltpu.VMEM((MOE_TB, d), F32),
            pltpu.VMEM((d, de), BF16),
            pltpu.VMEM((d, de), BF16),
            pltpu.VMEM((de, d), BF16),
            pltpu.SMEM((1,), jnp.int32),
            pltpu.SMEM((2,), jnp.int32),
        ],
    )
    return pl.pallas_call(
        _expert_kernel,
        grid_spec=grid_spec,
        out_shape=jax.ShapeDtypeStruct(xs_tiles.shape, F32),
        compiler_params=_cparams(("arbitrary",)),
        name="expert_mlps",
    )(item_blk, item_exp, item_lo, item_hi, xs_tiles, w_gate, w_up, w_down)


def _combine_kernel(pos_ref, pos_next_ref, ys_hbm, x_ref, route_ref, g2_ref, lng_ref, lnb_ref, xo_ref, buf, sems,
                    *, alpha):
    i = pl.program_id(0)
    n = pl.num_programs(0)
    tm = COMBINE_TM
    sub = buf.shape[2] // tm
    slot = i % 2

    def gather(p_ref, dst_slot):
        def issue(jb, carry):
            for u in range(DMA_UNROLL):
                j = jb * DMA_UNROLL + u
                for k in range(TOP_K):
                    _tile_copy(ys_hbm, buf.at[dst_slot, k], p_ref[0, 0, TOP_K * j + k], j, sub,
                               sems.at[dst_slot]).start(priority=k % 2)
            return carry

        lax.fori_loop(0, tm // DMA_UNROLL, issue, 0)

    @pl.when(i == 0)
    def _():
        gather(pos_ref, slot)

    @pl.when(i + 1 < n)
    def _():
        gather(pos_next_ref, 1 - slot)

    route = route_ref[...]
    lane = lax.broadcasted_iota(jnp.int32, route.shape, 1)
    gates = [jnp.sum(jnp.where(lane == k, route, 0.0), axis=-1, keepdims=True) for k in range(TOP_K)]
    for k in range(TOP_K):
        pltpu.make_async_copy(ys_hbm.at[pl.ds(0, tm * sub)], buf.at[slot, k], sems.at[slot]).wait()
    y = _load_token_tiles(buf.at[slot, 0], tm, F32) * gates[0]
    for k in range(1, TOP_K):
        y = y + _load_token_tiles(buf.at[slot, k], tm, F32) * gates[k]
    z = alpha * x_ref[...] + (1.0 + g2_ref[0]) * y
    xo_ref[...] = _layer_norm_rows(z, lng_ref[...], lnb_ref[...])


def _combine(pos, ys, x2d, route2d, g2, ln_g, ln_b, seq, alpha):
    n_tok, d = x2d.shape
    tm = COMBINE_TM
    n_tiles = n_tok // tm
    tiles_per_batch = seq // tm
    const = lambda i: (0, 0)
    pos3 = pos.reshape(n_tiles, 1, TOP_K * tm)
    return pl.pallas_call(
        functools.partial(_combine_kernel, alpha=alpha),
        grid=(n_tiles,),
        in_specs=[
            pl.BlockSpec((1, 1, TOP_K * tm), lambda i: (i, 0, 0), memory_space=pltpu.SMEM),
            pl.BlockSpec((1, 1, TOP_K * tm), lambda i: (jnp.minimum(i + 1, n_tiles - 1), 0, 0),
                         memory_space=pltpu.SMEM),
            pl.BlockSpec(memory_space=pl.ANY),
            pl.BlockSpec((tm, d), lambda i: (i, 0)),
            pl.BlockSpec((tm, LANES), lambda i: (i, 0)),
            pl.BlockSpec((1, 1, d), lambda i: (i // tiles_per_batch, 0, 0)),
            pl.BlockSpec((1, d), const),
            pl.BlockSpec((1, d), const),
        ],
        out_specs=pl.BlockSpec((tm, d), lambda i: (i, 0)),
        out_shape=jax.ShapeDtypeStruct((n_tok, d), F32),
        scratch_shapes=[
            pltpu.VMEM((2, TOP_K, tm * TOKEN_ROWS, d // TOKEN_ROWS), F32),
            pltpu.SemaphoreType.DMA((2,)),
        ],
        compiler_params=_cparams(("arbitrary",)),
        name="moe_combine",
    )(pos3, pos3, ys, x2d, route2d, g2, ln_g, ln_b)


def _combine_project_kernel(pos_ref, pos_next_ref, ys_hbm, x_ref, route_ref, g2_ref, lng_ref, lnb_ref,
                            sc_ref, sh_ref, w_ref, *rest, alpha, d_model, gqa):
    xo_ref, qkv_ref, buf, sems = rest[-4:]
    i = pl.program_id(0)
    n = pl.num_programs(0)
    tm = ROW_TILE
    sub = buf.shape[2] // tm
    slot = i % 2

    def tile_gather(p_ref, dst_slot, j, k):
        return _tile_copy(ys_hbm, buf.at[dst_slot, k], p_ref[0, 0, TOP_K * j + k], j, sub, sems.at[dst_slot])

    @pl.when(i == 0)
    def _():
        def issue(jb, carry):
            for u in range(DMA_UNROLL):
                for k in range(TOP_K):
                    tile_gather(pos_ref, slot, jb * DMA_UNROLL + u, k).start(priority=k % 2)
            return carry

        lax.fori_loop(0, tm // DMA_UNROLL, issue, 0)

    route = route_ref[...]
    lane = lax.broadcasted_iota(jnp.int32, route.shape, 1)
    gates = [jnp.sum(jnp.where(lane == k, route, 0.0), axis=-1, keepdims=True) for k in range(TOP_K)]
    for k in range(TOP_K):
        pltpu.make_async_copy(ys_hbm.at[pl.ds(0, tm * sub)], buf.at[slot, k], sems.at[slot]).wait()
    y = _load_token_tiles(buf.at[slot, 0], tm, F32) * gates[0]
    for k in range(1, TOP_K):
        y = y + _load_token_tiles(buf.at[slot, k], tm, F32) * gates[k]
    z = alpha * x_ref[...] + (1.0 + g2_ref[0]) * y
    xn = _layer_norm_rows(z, lng_ref[...], lnb_ref[...])
    xo_ref[...] = xn
    h = (xn * (1.0 + sc_ref[0]) + sh_ref[0]).astype(BF16)

    for j in range(tm):
        for k in range(TOP_K):
            tile_gather(pos_next_ref, 1 - slot, j, k).start(priority=k % 2)

    if gqa:
        _project_gqa(h, w_ref, *rest[:6], qkv_ref, d_model)
    else:
        _project_na(h, w_ref, qkv_ref, d_model)

    @pl.when(i == n - 1)
    def _():
        for k in range(TOP_K):
            pltpu.make_async_copy(ys_hbm.at[pl.ds(0, tm * sub)], buf.at[1 - slot, k], sems.at[1 - slot]).wait()


def _combine_project(pos, ys, x2d, route2d, g2, ln_g, ln_b, sc, sh, w_bf16, gqa_extras, seq, alpha):
    n_tok, d = x2d.shape
    n = w_bf16.shape[1]
    tm = ROW_TILE
    n_tiles = n_tok // tm
    tiles_per_batch = seq // tm
    const = lambda i: (0, 0)
    per_b = lambda i: (i // tiles_per_batch, 0, 0)
    pos3 = pos.reshape(n_tiles, 1, TOP_K * tm)
    extra_specs = []
    if gqa_extras:
        extra_specs = [
            pl.BlockSpec((1, QKV_CHUNK), const),
            pl.BlockSpec((1, QKV_CHUNK), const),
            pl.BlockSpec((tm, QKV_CHUNK), lambda i: (i % tiles_per_batch, 0)),
            pl.BlockSpec((tm, QKV_CHUNK), lambda i: (i % tiles_per_batch, 0)),
            pl.BlockSpec((QKV_CHUNK, LANES), const),
            pl.BlockSpec((LANES, QKV_CHUNK), const),
        ]
    return pl.pallas_call(
        functools.partial(_combine_project_kernel, alpha=alpha, d_model=d, gqa=bool(gqa_extras)),
        grid=(n_tiles,),
        in_specs=[
            pl.BlockSpec((1, 1, TOP_K * tm), lambda i: (i, 0, 0), memory_space=pltpu.SMEM),
            pl.BlockSpec((1, 1, TOP_K * tm), lambda i: (jnp.minimum(i + 1, n_tiles - 1), 0, 0),
                         memory_space=pltpu.SMEM),
            pl.BlockSpec(memory_space=pl.ANY),
            pl.BlockSpec((tm, d), lambda i: (i, 0)),
            pl.BlockSpec((tm, LANES), lambda i: (i, 0)),
            pl.BlockSpec((1, 1, d), per_b),
            pl.BlockSpec((1, d), const),
            pl.BlockSpec((1, d), const),
            pl.BlockSpec((1, 1, d), per_b),
            pl.BlockSpec((1, 1, d), per_b),
            pl.BlockSpec((d, n), const),
        ] + extra_specs,
        out_specs=[
            pl.BlockSpec((tm, d), lambda i: (i, 0)),
            pl.BlockSpec((tm, n), lambda i: (i, 0)),
        ],
        out_shape=[
            jax.ShapeDtypeStruct((n_tok, d), F32),
            jax.ShapeDtypeStruct((n_tok, n), BF16),
        ],
        scratch_shapes=[
            pltpu.VMEM((2, TOP_K, tm * TOKEN_ROWS, d // TOKEN_ROWS), F32),
            pltpu.SemaphoreType.DMA((2,)),
        ],
        compiler_params=_cparams(("arbitrary",)),
        name="combine_qkv_gqa" if gqa_extras else "combine_qkv_na",
    )(pos3, pos3, ys, x2d, route2d, g2, ln_g, ln_b, sc, sh, w_bf16, *gqa_extras)


def _routing_tables(route2d, cnt):
    n_tok = route2d.shape[0]
    n_rows = n_tok * TOP_K
    counts = cnt[0, N_GROUPS:N_GROUPS + N_EXPERTS].astype(jnp.int32)
    seg_end = jnp.cumsum(counts)
    experts = route2d[:, 2:2 + TOP_K].astype(jnp.int32)
    ranks = route2d[:, 4:4 + TOP_K].astype(jnp.int32)
    is_e = experts[:, :, None] == jnp.arange(N_EXPERTS, dtype=jnp.int32)[None, None, :]
    pos = (ranks + jnp.sum(jnp.where(is_e, (seg_end - counts)[None, None, :], 0), axis=-1)).reshape(n_rows)
    cuts = jnp.concatenate([jnp.arange(n_rows // MOE_TB, dtype=jnp.int32) * MOE_TB, seg_end - counts])
    n_items = cuts.shape[0]
    idx = jnp.arange(n_items, dtype=jnp.int32)
    rank = (jnp.sum(cuts[None, :] < cuts[:, None], axis=1)
            + jnp.sum((cuts[None, :] == cuts[:, None]) & (idx[None, :] < idx[:, None]), axis=1))
    lo_abs = jnp.sum(jnp.where(rank[None, :] == idx[:, None], cuts[None, :], 0), axis=1)
    hi_abs = jnp.concatenate([lo_abs[1:], jnp.full((1,), n_rows, jnp.int32)])
    item_blk = jnp.minimum(lo_abs // MOE_TB, n_rows // MOE_TB - 1)
    item_exp = jnp.minimum(jnp.sum(seg_end[None, :] <= lo_abs[:, None], axis=1), N_EXPERTS - 1)
    item_lo = lo_abs - item_blk * MOE_TB
    item_hi = hi_abs - item_blk * MOE_TB
    items = tuple(a.astype(jnp.int32) for a in (item_blk, item_exp, item_lo, item_hi))
    return pos, items


def _rope_tables(seq, width):
    t = jnp.arange(seq, dtype=jnp.int32)
    pos = jnp.stack([t // GRID_W, t % GRID_W], axis=-1).astype(F32)
    inv_freq = ROPE_THETA ** (-jnp.arange(ROPE_PAIRS, dtype=F32) / ROPE_PAIRS)
    ang = pos[:, :, None] * inv_freq
    cos, sin = jnp.cos(ang), jnp.sin(ang)
    cosf = jnp.stack([cos, cos], axis=2).reshape(seq, HEAD_DIM)
    sinf = jnp.stack([-sin, sin], axis=2).reshape(seq, HEAD_DIM)
    reps = width // HEAD_DIM
    return jnp.tile(cosf, (1, reps)), jnp.tile(sinf, (1, reps))


def _dup_heads(w):
    d = w.shape[0]
    w4 = w.reshape(d, GQA_KV_HEADS, 1, HEAD_DIM)
    return jnp.concatenate([w4, w4], axis=2).reshape(d, GQA_KV_HEADS * HEAD_PAIR)


def kernel(x, c, ada_w, ada_b, ln_g, ln_b, na_w_qkv, na_rpb, na_w_o, gqa_w_qkv, gqa_q_norm, gqa_k_norm, gqa_w_o,
           moe_w_group, moe_b_group, moe_w_expert, moe_b_expert, moe_w_gate, moe_w_up, moe_w_down):
    b, s, d = x.shape
    depth = ada_w.shape[0]
    n_tok = b * s
    rows = s // GRID_W
    kvd = GQA_KV_HEADS * HEAD_DIM
    alpha = (2 * depth) ** 0.25

    mod = _ada_modulation(c, ada_w, ada_b).reshape(depth, b, 6, 1, d)

    cosf, sinf = _rope_tables(s, QKV_CHUNK)
    head_of_lane = jnp.arange(QKV_CHUNK, dtype=jnp.int32) // HEAD_DIM
    seg_t = (head_of_lane[None, :] == jnp.arange(LANES, dtype=jnp.int32)[:, None])
    seg = (seg_t.T.astype(F32) / HEAD_DIM).astype(BF16)
    seg_t = seg_t.astype(BF16)
    tok = jnp.arange(ROW_TILE, dtype=jnp.int32)
    before = (tok[None, :] < tok[:, None]).astype(BF16)

    def qkv_params(i):
        j = i // 2
        if i % 2 == 0:
            return na_w_qkv[j].astype(BF16), ()
        w = gqa_w_qkv[j]
        w_all = jnp.concatenate([w[:, :d], _dup_heads(w[:, d:d + kvd]), _dup_heads(w[:, d + kvd:])], axis=1)
        reps = QKV_CHUNK // HEAD_DIM
        return w_all.astype(BF16), (jnp.tile(gqa_q_norm[j], reps)[None, :], jnp.tile(gqa_k_norm[j], reps)[None, :],
                                    cosf, sinf, seg, seg_t)

    qkv = None
    for i in range(depth):
        sh1, sc1, g1, sh2, sc2, g2 = (mod[i, :, k] for k in range(6))
        j = i // 2
        if qkv is None:
            w_bf16, extras = qkv_params(i)
            qkv = _qkv_gqa(x, sc1, sh1, w_bf16, *extras) if extras else _qkv_na(x, sc1, sh1, w_bf16)
        if i % 2 == 0:
            attn = _na_attention(qkv, _na_bias_table(na_rpb[j]), rows)
            w_o = na_w_o[j]
        else:
            attn = _gqa_attention(qkv, d, gqa_q_norm[j], gqa_k_norm[j])
            w_o = gqa_w_o[j]

        w_r = jnp.concatenate([moe_w_group[i], moe_w_expert[i]], axis=1)
        w_r = jnp.pad(w_r, ((0, 0), (0, LANES - w_r.shape[1])))
        wr_hi, wr_lo = _split_bf16(w_r)
        b_r = jnp.concatenate([moe_b_group[i], moe_b_expert[i]])
        b_r = jnp.pad(b_r, (0, LANES - b_r.shape[0]))[None, :]
        x, h2, route, cnt = _post_attn(attn, w_o.astype(BF16), x, g1, sc2, sh2, ln_g[i, 0][None, :],
                                       ln_b[i, 0][None, :], jnp.concatenate([wr_hi, wr_lo], axis=1), b_r, before,
                                       alpha)

        route2d = route.reshape(n_tok, LANES)
        pos, items = _routing_tables(route2d, cnt)
        xs = _dispatch(pos, h2, n_tok)
        ys = _expert_mlps(items, xs, moe_w_gate, moe_w_up, moe_w_down, i)
        lng, lnb = ln_g[i, 1][None, :], ln_b[i, 1][None, :]
        if i + 1 < depth:
            w_bf16, extras = qkv_params(i + 1)
            x2d, qkv2d = _combine_project(pos, ys, x.reshape(n_tok, d), route2d, g2, lng, lnb,
                                          mod[i + 1, :, 1], mod[i + 1, :, 0], w_bf16, extras, s, alpha)
            x, qkv = x2d.reshape(b, s, d), qkv2d.reshape(b, s, -1)
        else:
            x = _combine(pos, ys, x.reshape(n_tok, d), route2d, g2, lng, lnb, s, alpha).reshape(b, s, d)
    return x
```

```python
import functools

import jax
import jax.numpy as jnp
from jax import lax
from jax.experimental import pallas as pl
from jax.experimental.pallas import tpu as pltpu

GRID_W = 64
N_HEADS = 16
HEAD_DIM = 64
NA_ROWS = 8
NA_COLS = 16
GQA_KV_HEADS = 4
ROPE_THETA = 10000.0
ROPE_PAIRS = HEAD_DIM // 4
N_GROUPS = 4
EXPERTS_PER_GROUP = 8
N_EXPERTS = N_GROUPS * EXPERTS_PER_GROUP
TOP_K = 2
LN_EPS = 1e-5
RMS_EPS = 1e-6

LANES = 128
TOKEN_ROWS = 8
HEAD_PAIR = 2 * HEAD_DIM
VMEM_LIMIT_BYTES = 52 * 1024 * 1024
ADA_TN = 1536
ROW_TILE = 512
QKV_CHUNK = 512
GQA_TQ = 1024
NA_ROWS_PER_STEP = 4
NA_STAGE_PAIRS = 4
MOE_TB = 256
DISPATCH_TM = 2048
COMBINE_TM = 1024
DMA_UNROLL = 8
NEG_BIG = -1e30
LOG2_E = 1.4426950408889634
Q_SCALE = (HEAD_DIM ** -0.5) * LOG2_E
GQA_BOUND_MARGIN = 1.02
GQA_BOUND_MAX = 50.0

F32 = jnp.float32
BF16 = jnp.bfloat16


def _cparams(sem):
    return pltpu.CompilerParams(dimension_semantics=sem, vmem_limit_bytes=VMEM_LIMIT_BYTES)


def _split_bf16(a):
    hi = a.astype(BF16)
    lo = (a - hi.astype(F32)).astype(BF16)
    return hi, lo


def _ada_kernel(c_ref, w_ref, b_ref, o_ref):
    c = c_ref[...]
    ca = (c * jax.nn.sigmoid(c)).astype(BF16)
    o_ref[0] = jnp.dot(ca, w_ref[0].astype(BF16), preferred_element_type=F32) + b_ref[0]


def _ada_modulation(c, ada_w, ada_b):
    depth, d, n = ada_w.shape
    b = c.shape[0]
    return pl.pallas_call(
        _ada_kernel,
        grid=(depth, n // ADA_TN),
        in_specs=[
            pl.BlockSpec((b, d), lambda i, j: (0, 0)),
            pl.BlockSpec((1, d, ADA_TN), lambda i, j: (i, 0, j)),
            pl.BlockSpec((1, 1, ADA_TN), lambda i, j: (i, 0, j)),
        ],
        out_specs=pl.BlockSpec((1, b, ADA_TN), lambda i, j: (i, 0, j)),
        out_shape=jax.ShapeDtypeStruct((depth, b, n), F32),
        compiler_params=_cparams(("arbitrary", "arbitrary")),
        name="ada_modulation",
    )(c, ada_w, ada_b.reshape(depth, 1, n))


def _project_na(h, w_ref, o_ref, d_model):
    n = w_ref.shape[1]
    for n0 in range(0, n, QKV_CHUNK):
        acc = jnp.dot(h, w_ref[:, n0:n0 + QKV_CHUNK], preferred_element_type=F32)
        if n0 < d_model:
            acc = acc * Q_SCALE
        o_ref[:, n0:n0 + QKV_CHUNK] = acc.astype(BF16)


def _qkv_na_kernel(x_ref, sc_ref, sh_ref, w_ref, o_ref, *, d_model):
    h = (x_ref[0] * (1.0 + sc_ref[0]) + sh_ref[0]).astype(BF16)
    _project_na(h, w_ref, o_ref.at[0], d_model)


def _qkv_na(x, sc, sh, w_bf16):
    b, s, d = x.shape
    n = w_bf16.shape[1]
    return pl.pallas_call(
        functools.partial(_qkv_na_kernel, d_model=d),
        grid=(b, s // ROW_TILE),
        in_specs=[
            pl.BlockSpec((1, ROW_TILE, d), lambda i, j: (i, j, 0)),
            pl.BlockSpec((1, 1, d), lambda i, j: (i, 0, 0)),
            pl.BlockSpec((1, 1, d), lambda i, j: (i, 0, 0)),
            pl.BlockSpec((d, n), lambda i, j: (0, 0)),
        ],
        out_specs=pl.BlockSpec((1, ROW_TILE, n), lambda i, j: (i, j, 0)),
        out_shape=jax.ShapeDtypeStruct((b, s, n), BF16),
        compiler_params=_cparams(("arbitrary", "arbitrary")),
        name="qkv_na",
    )(x, sc, sh, w_bf16)


N_DR = 2 * NA_ROWS - 1
N_DC = 2 * NA_COLS - 1


def _na_bias_kernel(r_ref, m_ref, valid_ref, o_ref):
    r = r_ref[...]
    hi = r.astype(BF16)
    r1 = r - hi.astype(F32)
    mid = r1.astype(BF16)
    lo = (r1 - mid.astype(F32)).astype(BF16)
    m = m_ref[...]
    acc = (jnp.dot(hi, m, preferred_element_type=F32) + jnp.dot(mid, m, preferred_element_type=F32)
           + jnp.dot(lo, m, preferred_element_type=F32))
    o_ref[...] = jnp.where(valid_ref[...] > 0.0, acc * LOG2_E, NEG_BIG)


def _na_bias_table(rpb):
    qc = jnp.arange(GRID_W, dtype=jnp.int32)
    kc = jnp.arange(GRID_W, dtype=jnp.int32)
    win_start = jnp.clip(qc - NA_COLS // 2, 0, GRID_W - NA_COLS)
    valid = (kc[None, :] >= win_start[:, None]) & (kc[None, :] < win_start[:, None] + NA_COLS)
    dc = jnp.clip(kc[None, :] - qc[:, None] + NA_COLS - 1, 0, N_DC - 1)
    onehot = (jnp.arange(LANES, dtype=jnp.int32)[:, None] == dc.reshape(1, GRID_W * GRID_W)).astype(BF16)
    r2d = jnp.pad(rpb.reshape(N_HEADS * N_DR, N_DC).astype(F32), ((0, 0), (0, LANES - N_DC)))
    tiles = pl.pallas_call(
        _na_bias_kernel,
        out_shape=jax.ShapeDtypeStruct((N_HEADS * N_DR, GRID_W * GRID_W), F32),
        compiler_params=pltpu.CompilerParams(vmem_limit_bytes=VMEM_LIMIT_BYTES),
        name="na_bias_tiles",
    )(r2d, onehot, valid.reshape(1, GRID_W * GRID_W).astype(F32))
    tiles = tiles.reshape(N_HEADS, N_DR, GRID_W, GRID_W)
    pairs = jnp.concatenate([tiles[:, :N_DR - 1], tiles[:, 1:]], axis=-1)
    pairs = pairs.reshape(N_HEADS // 2, 2, N_DR - 1, GRID_W, HEAD_PAIR).transpose(0, 2, 1, 3, 4)
    return pairs.reshape(N_HEADS // 2, N_DR - 1, 2 * GRID_W, HEAD_PAIR)


def _na_kernel(q_ref, k_ref, v_ref, bias_ref, o_ref, *, rows):
    nk = NA_ROWS * GRID_W
    lane = lax.broadcasted_iota(jnp.int32, (GRID_W, HEAD_PAIR), 1)
    first = lane < HEAD_DIM
    for sub_row in range(NA_ROWS_PER_STEP):
        r = pl.program_id(1) * NA_ROWS_PER_STEP + sub_row
        row_start = jnp.clip(r - NA_ROWS // 2, 0, rows - NA_ROWS)
        start = pl.multiple_of(row_start * GRID_W, GRID_W)
        dr_base = row_start - r + (NA_ROWS - 1)
        q_rows = slice(sub_row * GRID_W, (sub_row + 1) * GRID_W)
        for hp0 in range(0, N_HEADS // 2, NA_STAGE_PAIRS):
            pairs = range(hp0, hp0 + NA_STAGE_PAIRS)
            cols = {hp: slice(hp * HEAD_PAIR, (hp + 1) * HEAD_PAIR) for hp in pairs}
            s_all = {}
            for hp in pairs:
                q2 = q_ref[0, q_rows, cols[hp]]
                k2 = k_ref[0, pl.ds(start, nk), cols[hp]]
                zero = jnp.zeros_like(q2)
                qs = jnp.concatenate([jnp.where(first, q2, zero), jnp.where(first, zero, q2)], axis=0)
                s = lax.dot_general(qs, k2, (((1,), (1,)), ((), ())), preferred_element_type=F32)
                s_all[hp] = s + jnp.concatenate([bias_ref[hp, dr_base + 2 * m] for m in range(NA_ROWS // 2)],
                                                axis=1)
            m_all = {hp: jnp.max(s_all[hp], axis=-1, keepdims=True) for hp in pairs}
            p_all = {hp: jnp.exp2(s_all[hp] - m_all[hp]) for hp in pairs}
            l_all = {hp: jnp.sum(p_all[hp], axis=-1, keepdims=True) for hp in pairs}
            o_all = {hp: jnp.dot(p_all[hp].astype(BF16), v_ref[0, pl.ds(start, nk), cols[hp]],
                                 preferred_element_type=F32) for hp in pairs}
            for hp in pairs:
                o = o_all[hp] / l_all[hp]
                o_ref[0, q_rows, cols[hp]] = jnp.where(first, o[:GRID_W], o[GRID_W:]).astype(BF16)


def _na_attention(qkv, bias, rows):
    b, s, n3 = qkv.shape
    d = n3 // 3
    assert rows >= NA_ROWS and NA_ROWS % 2 == 0 and rows % NA_ROWS_PER_STEP == 0
    q_tile = NA_ROWS_PER_STEP * GRID_W
    return pl.pallas_call(
        functools.partial(_na_kernel, rows=rows),
        grid=(b, rows // NA_ROWS_PER_STEP),
        in_specs=[
            pl.BlockSpec((1, q_tile, d), lambda i, r: (i, r, 0)),
            pl.BlockSpec((1, s, d), lambda i, r: (i, 0, 1)),
            pl.BlockSpec((1, s, d), lambda i, r: (i, 0, 2)),
            pl.BlockSpec(bias.shape, lambda i, r: (0, 0, 0, 0)),
        ],
        out_specs=pl.BlockSpec((1, q_tile, d), lambda i, r: (i, r, 0)),
        out_shape=jax.ShapeDtypeStruct((b, s, d), BF16),
        compiler_params=_cparams(("arbitrary", "arbitrary")),
        name="na_attention",
    )(qkv, qkv, qkv, bias)


def _rms_rope(a, gain, cosf, sinf, seg, seg_t):
    qq_hi, qq_lo = _split_bf16(a * a)
    ms = jnp.dot(qq_hi, seg, preferred_element_type=F32) + jnp.dot(qq_lo, seg, preferred_element_type=F32)
    r_hi, r_lo = _split_bf16(lax.rsqrt(ms + RMS_EPS))
    rf = jnp.dot(r_hi, seg_t, preferred_element_type=F32) + jnp.dot(r_lo, seg_t, preferred_element_type=F32)
    an = a * rf * gain
    n = a.shape[1]
    lane = lax.broadcasted_iota(jnp.int32, a.shape, 1)
    partner = jnp.where(lane % (2 * ROPE_PAIRS) < ROPE_PAIRS,
                        pltpu.roll(an, n - ROPE_PAIRS, 1), pltpu.roll(an, ROPE_PAIRS, 1))
    return an * cosf + partner * sinf


def _spread_heads(a, fill):
    lane = lax.broadcasted_iota(jnp.int32, (a.shape[0], HEAD_PAIR), 1)
    low = lane < HEAD_DIM
    out = []
    for g in range(a.shape[1] // HEAD_DIM):
        blk = a[:, (g // 2) * HEAD_PAIR:(g // 2 + 1) * HEAD_PAIR]
        swapped = pltpu.roll(blk, HEAD_DIM, 1)
        head_low, head_high = (blk, swapped) if g % 2 == 0 else (swapped, blk)
        out.append(jnp.where(low, head_low, head_high if fill is None else fill))
    return jnp.concatenate(out, axis=1)


def _gqa_out_cols(d_model):
    return d_model + 2 * GQA_KV_HEADS * HEAD_PAIR


def _project_gqa(h, w_ref, qg_ref, kg_ref, cos_ref, sin_ref, seg_ref, segt_ref, o_ref, d_model):
    kvd = GQA_KV_HEADS * HEAD_DIM
    cosf, sinf = cos_ref[...], sin_ref[...]
    seg, seg_t = seg_ref[...], segt_ref[...]
    for n0 in range(0, d_model, QKV_CHUNK):
        acc = jnp.dot(h, w_ref[:, n0:n0 + QKV_CHUNK], preferred_element_type=F32)
        acc = _rms_rope(acc, qg_ref[...], cosf, sinf, seg, seg_t) * Q_SCALE
        o_ref[:, n0:n0 + QKV_CHUNK] = acc.astype(BF16)
    kv = jnp.dot(h, w_ref[:, d_model:d_model + 2 * kvd], preferred_element_type=F32)
    k = _rms_rope(kv[:, :kvd], kg_ref[:, :kvd], cosf[:, :kvd], sinf[:, :kvd], seg[:kvd], seg_t[:, :kvd])
    wide = GQA_KV_HEADS * HEAD_PAIR
    o_ref[:, d_model:d_model + wide] = _spread_heads(k, None).astype(BF16)
    o_ref[:, d_model + wide:d_model + 2 * wide] = _spread_heads(kv[:, kvd:], 1.0).astype(BF16)


def _qkv_gqa_kernel(x_ref, sc_ref, sh_ref, w_ref, qg_ref, kg_ref, cos_ref, sin_ref, seg_ref, segt_ref, o_ref,
                    *, d_model):
    h = (x_ref[0] * (1.0 + sc_ref[0]) + sh_ref[0]).astype(BF16)
    _project_gqa(h, w_ref, qg_ref, kg_ref, cos_ref, sin_ref, seg_ref, segt_ref, o_ref.at[0], d_model)


def _qkv_gqa(x, sc, sh, w_bf16, q_gain, k_gain, cosf, sinf, seg, seg_t):
    b, s, d = x.shape
    n = w_bf16.shape[1]
    const = lambda j, i: (0, 0)
    return pl.pallas_call(
        functools.partial(_qkv_gqa_kernel, d_model=d),
        grid=(s // ROW_TILE, b),
        in_specs=[
            pl.BlockSpec((1, ROW_TILE, d), lambda j, i: (i, j, 0)),
            pl.BlockSpec((1, 1, d), lambda j, i: (i, 0, 0)),
            pl.BlockSpec((1, 1, d), lambda j, i: (i, 0, 0)),
            pl.BlockSpec((d, n), const),
            pl.BlockSpec((1, QKV_CHUNK), const),
            pl.BlockSpec((1, QKV_CHUNK), const),
            pl.BlockSpec((ROW_TILE, QKV_CHUNK), lambda j, i: (j, 0)),
            pl.BlockSpec((ROW_TILE, QKV_CHUNK), lambda j, i: (j, 0)),
            pl.BlockSpec((QKV_CHUNK, LANES), const),
            pl.BlockSpec((LANES, QKV_CHUNK), const),
        ],
        out_specs=pl.BlockSpec((1, ROW_TILE, _gqa_out_cols(d)), lambda j, i: (i, j, 0)),
        out_shape=jax.ShapeDtypeStruct((b, s, _gqa_out_cols(d)), BF16),
        compiler_params=_cparams(("arbitrary", "arbitrary")),
        name="qkv_gqa",
    )(x, sc, sh, w_bf16, q_gain, k_gain, cosf, sinf, seg, seg_t)


def _gqa_kernel(bound_ref, q_ref, k_ref, v_ref, o_ref, *, use_bound):
    k2 = k_ref[0]
    v2 = v_ref[0]
    tq = q_ref.shape[1]
    lane = lax.broadcasted_iota(jnp.int32, (tq, HEAD_PAIR), 1)
    first = lane < HEAD_DIM
    for pair in range(q_ref.shape[2] // HEAD_PAIR):
        cols = slice(pair * HEAD_PAIR, (pair + 1) * HEAD_PAIR)
        q2 = q_ref[0, :, cols]
        zero = jnp.zeros_like(q2)
        qs = jnp.concatenate([jnp.where(first, q2, zero), jnp.where(first, zero, q2)], axis=0)
        s = lax.dot_general(qs, k2, (((1,), (1,)), ((), ())), preferred_element_type=F32)
        m = bound_ref[0] if use_bound else jnp.max(s, axis=-1, keepdims=True)
        p = jnp.exp2(s - m).astype(BF16)
        o = jnp.dot(p, v2, preferred_element_type=F32)
        o_a, o_b = o[:tq], o[tq:]
        out = jnp.where(first, o_a / pltpu.roll(o_a, HEAD_DIM, 1), pltpu.roll(o_b, HEAD_DIM, 1) / o_b)
        o_ref[0, :, cols] = out.astype(BF16)


def _gqa_logit_bound(q_gain, k_gain):
    return (GQA_BOUND_MARGIN * LOG2_E * (HEAD_DIM ** 0.5)
            * jnp.max(jnp.abs(q_gain)) * jnp.max(jnp.abs(k_gain))).astype(F32).reshape(1)


def _gqa_attention(qkv, d_model, q_gain, k_gain):
    bound = _gqa_logit_bound(q_gain, k_gain)
    return lax.cond(bound[0] <= GQA_BOUND_MAX,
                    functools.partial(_gqa_attention_call, d_model=d_model, use_bound=True),
                    functools.partial(_gqa_attention_call, d_model=d_model, use_bound=False),
                    bound, qkv)


def _gqa_attention_call(bound, qkv, *, d_model, use_bound):
    b, s, n = qkv.shape
    group_cols = d_model // GQA_KV_HEADS
    k_block0 = d_model // HEAD_PAIR
    v_block0 = k_block0 + GQA_KV_HEADS
    return pl.pallas_call(
        functools.partial(_gqa_kernel, use_bound=use_bound),
        grid=(b, GQA_KV_HEADS, s // GQA_TQ),
        in_specs=[
            pl.BlockSpec(memory_space=pltpu.SMEM),
            pl.BlockSpec((1, GQA_TQ, group_cols), lambda i, g, t: (i, t, g)),
            pl.BlockSpec((1, s, HEAD_PAIR), lambda i, g, t: (i, 0, k_block0 + g)),
            pl.BlockSpec((1, s, HEAD_PAIR), lambda i, g, t: (i, 0, v_block0 + g)),
        ],
        out_specs=pl.BlockSpec((1, GQA_TQ, group_cols), lambda i, g, t: (i, t, g)),
        out_shape=jax.ShapeDtypeStruct((b, s, d_model), BF16),
        compiler_params=_cparams(("arbitrary", "arbitrary", "arbitrary")),
        name="gqa_attention_bounded" if use_bound else "gqa_attention",
    )(bound, qkv, qkv, qkv)


def _store_token_tiles(ref, val, first_token=0):
    t, d = val.shape
    width = ref.shape[-1]
    sub = d // width
    for s in range(sub):
        ref[pl.ds(first_token * sub + s, t, stride=sub), :] = val[:, s * width:(s + 1) * width]


def _load_token_tiles(ref, t, dtype):
    sub = ref.shape[0] // t
    return jnp.concatenate([ref[pl.ds(s, t, stride=sub), :].astype(dtype) for s in range(sub)], axis=1)


def _tile_copy(src_hbm, dst_vmem, src_tok, dst_tok, sub, sem):
    return pltpu.make_async_copy(src_hbm.at[pl.ds(pl.multiple_of(src_tok * sub, sub), sub)],
                                 dst_vmem.at[pl.ds(pl.multiple_of(dst_tok * sub, sub), sub)], sem)


def _layer_norm_rows(z, g, b):
    mu = jnp.mean(z, axis=-1, keepdims=True)
    zc = z - mu
    var = jnp.mean(zc * zc, axis=-1, keepdims=True)
    return zc * lax.rsqrt(var + LN_EPS) * g + b


def _router(lg):
    lane = lax.broadcasted_iota(jnp.int32, lg.shape, 1)
    lanef = lane.astype(F32)
    far = float(LANES)
    gmask = lane < N_GROUPS
    lgm = jnp.where(gmask, lg, NEG_BIG)
    mg = jnp.max(lgm, axis=-1, keepdims=True)
    sg = jnp.sum(jnp.exp(lgm - mg), axis=-1, keepdims=True)
    g_prob = 1.0 / sg
    g_idx = jnp.min(jnp.where(lgm == mg, lanef, far), axis=-1, keepdims=True)
    e_lo = N_GROUPS + g_idx * EXPERTS_PER_GROUP
    emask = (lanef >= e_lo) & (lanef < e_lo + EXPERTS_PER_GROUP)
    lem = jnp.where(emask, lg, NEG_BIG)
    me = jnp.max(lem, axis=-1, keepdims=True)
    ee = jnp.exp(lem - me)
    pe = jnp.where(emask, ee / jnp.sum(ee, axis=-1, keepdims=True), -1.0)
    p1 = jnp.max(pe, axis=-1, keepdims=True)
    i1 = jnp.min(jnp.where(pe == p1, lanef, far), axis=-1, keepdims=True)
    pe2 = jnp.where(lanef == i1, -1.0, pe)
    p2 = jnp.max(pe2, axis=-1, keepdims=True)
    i2 = jnp.min(jnp.where(pe2 == p2, lanef, far), axis=-1, keepdims=True)
    den = p1 + p2
    gate1 = g_prob * (p1 / den)
    gate2 = g_prob * (p2 / den)
    out = jnp.where(lane == 0, gate1, 0.0)
    out = jnp.where(lane == 1, gate2, out)
    out = jnp.where(lane == 2, i1 - N_GROUPS, out)
    out = jnp.where(lane == 3, i2 - N_GROUPS, out)
    return out, lanef == i1, lanef == i2


def _post_attn_kernel(a_ref, wo_ref, x_ref, g1_ref, sc2_ref, sh2_ref, lng_ref, lnb_ref, wr_ref, br_ref,
                      before_ref, xo_ref, h2_ref, route_ref, cnt_ref, run_cnt, wo_bf16, *, alpha):
    @pl.when((pl.program_id(0) == 0) & (pl.program_id(1) == 0))
    def _():
        run_cnt[...] = jnp.zeros_like(run_cnt)
        wo_bf16[...] = wo_ref[0].astype(BF16)

    y = jnp.dot(a_ref[0], wo_bf16[...], preferred_element_type=F32)
    z = alpha * x_ref[0] + (1.0 + g1_ref[0]) * y
    xn = _layer_norm_rows(z, lng_ref[...], lnb_ref[...])
    xo_ref[0] = xn
    h2 = xn * (1.0 + sc2_ref[0]) + sh2_ref[0]
    _store_token_tiles(h2_ref, h2)
    h_hi, h_lo = _split_bf16(h2)
    t = jnp.dot(h_hi, wr_ref[...], preferred_element_type=F32)
    lg = (t[:, :LANES] + t[:, LANES:] + jnp.dot(h_lo, wr_ref[:, :LANES], preferred_element_type=F32)
          + br_ref[...])
    route, m1, m2 = _router(lg)
    mem = jnp.where(m1 | m2, 1.0, 0.0)
    rank = run_cnt[0:1, :] + jnp.dot(before_ref[...], mem.astype(BF16), preferred_element_type=F32)
    rank1 = jnp.sum(jnp.where(m1, rank, 0.0), axis=-1, keepdims=True)
    rank2 = jnp.sum(jnp.where(m2, rank, 0.0), axis=-1, keepdims=True)
    lane = lax.broadcasted_iota(jnp.int32, route.shape, 1)
    route_ref[0] = jnp.where(lane == 4, rank1, jnp.where(lane == 5, rank2, route))
    run_cnt[...] = run_cnt[...] + jnp.sum(mem, axis=0, keepdims=True)
    cnt_ref[...] = run_cnt[...]


def _post_attn(a, w_o, layer, x, g1, sc2, sh2, ln_g, ln_b, wr, br, before, alpha):
    b, s, d = x.shape
    sub, width = TOKEN_ROWS, d // TOKEN_ROWS
    tiles_per_batch = s // ROW_TILE
    const = lambda i, j: (0, 0)
    tile = lambda i, j: (i, j, 0)
    per_b = lambda i, j: (i, 0, 0)
    return pl.pallas_call(
        functools.partial(_post_attn_kernel, alpha=alpha),
        grid=(b, s // ROW_TILE),
        in_specs=[
            pl.BlockSpec((1, ROW_TILE, d), tile),
            pl.BlockSpec((1, d, d), lambda i, j: (layer, 0, 0), pipeline_mode=pl.Buffered(1)),
            pl.BlockSpec((1, ROW_TILE, d), tile),
            pl.BlockSpec((1, 1, d), per_b),
            pl.BlockSpec((1, 1, d), per_b),
            pl.BlockSpec((1, 1, d), per_b),
            pl.BlockSpec((1, d), const),
            pl.BlockSpec((1, d), const),
            pl.BlockSpec((d, 2 * LANES), const),
            pl.BlockSpec((1, LANES), const),
            pl.BlockSpec((ROW_TILE, ROW_TILE), const),
        ],
        out_specs=[
            pl.BlockSpec((1, ROW_TILE, d), tile),
            pl.BlockSpec((ROW_TILE * sub, width), lambda i, j: (i * tiles_per_batch + j, 0)),
            pl.BlockSpec((1, ROW_TILE, LANES), tile),
            pl.BlockSpec((8, LANES), const),
        ],
        out_shape=[
            jax.ShapeDtypeStruct((b, s, d), F32),
            jax.ShapeDtypeStruct((b * s * sub, width), F32),
            jax.ShapeDtypeStruct((b, s, LANES), F32),
            jax.ShapeDtypeStruct((8, LANES), F32),
        ],
        scratch_shapes=[pltpu.VMEM((8, LANES), F32), pltpu.VMEM((d, d), BF16)],
        compiler_params=_cparams(("arbitrary", "arbitrary")),
        name="post_attn",
    )(a, w_o, x, g1, sc2, sh2, ln_g, ln_b, wr, br, before)


def _dispatch_kernel(pos_ref, h_ref, xs_hbm, sem):
    tm = DISPATCH_TM
    sub = h_ref.shape[0] // tm

    def issue(jb, carry):
        for u in range(DMA_UNROLL):
            j = jb * DMA_UNROLL + u
            src = h_ref.at[pl.ds(pl.multiple_of(j * sub, sub), sub)]
            for k in range(TOP_K):
                row = pos_ref[0, 0, TOP_K * j + k]
                dst = xs_hbm.at[pl.ds(pl.multiple_of(row * sub, sub), sub)]
                pltpu.make_async_copy(src, dst, sem).start(priority=k % 2)
        return carry

    lax.fori_loop(0, tm // DMA_UNROLL, issue, 0)
    for _ in range(TOP_K):
        pltpu.make_async_copy(h_ref, xs_hbm.at[pl.ds(0, tm * sub)], sem).wait()


def _dispatch(pos, h2_tiles, n_tok):
    sub, width = h2_tiles.shape[0] // n_tok, h2_tiles.shape[1]
    tm = DISPATCH_TM
    n_tiles = n_tok // tm
    return pl.pallas_call(
        _dispatch_kernel,
        grid=(n_tiles,),
        in_specs=[
            pl.BlockSpec((1, 1, TOP_K * tm), lambda i: (i, 0, 0), memory_space=pltpu.SMEM),
            pl.BlockSpec((tm * sub, width), lambda i: (i, 0)),
        ],
        out_specs=pl.BlockSpec(memory_space=pl.ANY),
        out_shape=jax.ShapeDtypeStruct((n_tok * TOP_K * sub, width), F32),
        scratch_shapes=[pltpu.SemaphoreType.DMA],
        compiler_params=_cparams(("arbitrary",)),
        name="moe_dispatch",
    )(pos.reshape(n_tiles, 1, TOP_K * tm), h2_tiles)


def _expert_kernel(blk_ref, exp_ref, lo_ref, hi_ref, x_ref, wg_ref, wu_ref, wd_ref, y_ref,
                   acc, wgb, wub, wdb, cast_expert, state):
    i = pl.program_id(0)
    n_items = pl.num_programs(0)
    tb = MOE_TB
    prev = jnp.maximum(i - 1, 0)
    nxt = jnp.minimum(i + 1, n_items - 1)
    first = (i == 0) | (blk_ref[i] != blk_ref[prev])
    last = (i == n_items - 1) | (blk_ref[i] != blk_ref[nxt])
    lo, hi = lo_ref[i], hi_ref[i]
    whole = (lo == 0) & (hi == tb)

    @pl.when(i == 0)
    def _():
        cast_expert[0] = -1

    @pl.when(first)
    def _():
        state[0] = 0
        state[1] = 0

    @pl.when(hi > lo)
    def _():
        @pl.when(cast_expert[0] != exp_ref[i])
        def _():
            wgb[...] = wg_ref[0, 0].astype(BF16)
            wub[...] = wu_ref[0, 0].astype(BF16)
            wdb[...] = wd_ref[0, 0].astype(BF16)
            cast_expert[0] = exp_ref[i]

        xb = _load_token_tiles(x_ref, tb, BF16)
        g = jnp.dot(xb, wgb[...], preferred_element_type=F32)
        u = jnp.dot(xb, wub[...], preferred_element_type=F32)
        act = (g * jax.nn.sigmoid(g) * u).astype(BF16)
        y = jnp.dot(act, wdb[...], preferred_element_type=F32)

        @pl.when(whole)
        def _():
            _store_token_tiles(y_ref, y)
            state[0] = 1

        @pl.when(jnp.logical_not(whole))
        def _():
            row = lax.broadcasted_iota(jnp.int32, (tb, 1), 0)
            part = jnp.where((row >= lo) & (row < hi), y, 0.0)

            @pl.when(state[1] == 0)
            def _():
                acc[...] = part

            @pl.when(state[1] != 0)
            def _():
                acc[...] += part

            state[1] = 1

    @pl.when(last & (state[0] == 0))
    def _():
        _store_token_tiles(y_ref, acc[...])


def _expert_mlps(items, xs_tiles, w_gate, w_up, w_down, layer):
    item_blk, item_exp, item_lo, item_hi = items
    n_items = item_blk.shape[0]
    _, _, d, de = w_gate.shape
    sub, width = TOKEN_ROWS, d // TOKEN_ROWS
    w_index = lambda i, blk, ex, lo, hi: (layer, ex[i], 0, 0)
    grid_spec = pltpu.PrefetchScalarGridSpec(
        num_scalar_prefetch=4,
        grid=(n_items,),
        in_specs=[
            pl.BlockSpec((MOE_TB * sub, width), lambda i, blk, ex, lo, hi: (blk[i], 0)),
            pl.BlockSpec((1, 1, d, de), w_index),
            pl.BlockSpec((1, 1, d, de), w_index),
            pl.BlockSpec((1, 1, de, d), w_index),
        ],
        out_specs=pl.BlockSpec((MOE_TB * sub, width), lambda i, blk, ex, lo, hi: (blk[i], 0)),
        scratch_shapes=[
            pltpu.VMEM((MOE_TB, d), F32),
            pltpu.VMEM((d, de), BF16),
            pltpu.VMEM((d, de), BF16),
            pltpu.VMEM((de, d), BF16),
            pltpu.SMEM((1,), jnp.int32),
            pltpu.SMEM((2,), jnp.int32),
        ],
    )
    return pl.pallas_call(
        _expert_kernel,
        grid_spec=grid_spec,
        out_shape=jax.ShapeDtypeStruct(xs_tiles.shape, F32),
        compiler_params=_cparams(("arbitrary",)),
        name="expert_mlps",
    )(item_blk, item_exp, item_lo, item_hi, xs_tiles, w_gate, w_up, w_down)


def _combine_kernel(pos_ref, pos_next_ref, ys_hbm, x_ref, route_ref, g2_ref, lng_ref, lnb_ref, xo_ref, buf, sems,
                    *, alpha):
    i = pl.program_id(0)
    n = pl.num_programs(0)
    tm = COMBINE_TM
    sub = buf.shape[2] // tm
    slot = i % 2

    def gather(p_ref, dst_slot):
        def issue(jb, carry):
            for u in range(DMA_UNROLL):
                j = jb * DMA_UNROLL + u
                for k in range(TOP_K):
                    _tile_copy(ys_hbm, buf.at[dst_slot, k], p_ref[0, 0, TOP_K * j + k], j, sub,
                               sems.at[dst_slot]).start(priority=k % 2)
            return carry

        lax.fori_loop(0, tm // DMA_UNROLL, issue, 0)

    @pl.when(i == 0)
    def _():
        gather(pos_ref, slot)

    @pl.when(i + 1 < n)
    def _():
        gather(pos_next_ref, 1 - slot)

    route = route_ref[...]
    lane = lax.broadcasted_iota(jnp.int32, route.shape, 1)
    gates = [jnp.sum(jnp.where(lane == k, route, 0.0), axis=-1, keepdims=True) for k in range(TOP_K)]
    for k in range(TOP_K):
        pltpu.make_async_copy(ys_hbm.at[pl.ds(0, tm * sub)], buf.at[slot, k], sems.at[slot]).wait()
    y = _load_token_tiles(buf.at[slot, 0], tm, F32) * gates[0]
    for k in range(1, TOP_K):
        y = y + _load_token_tiles(buf.at[slot, k], tm, F32) * gates[k]
    z = alpha * x_ref[...] + (1.0 + g2_ref[0]) * y
    xo_ref[...] = _layer_norm_rows(z, lng_ref[...], lnb_ref[...])


def _combine(pos, ys, x2d, route2d, g2, ln_g, ln_b, seq, alpha):
    n_tok, d = x2d.shape
    tm = COMBINE_TM
    n_tiles = n_tok // tm
    tiles_per_batch = seq // tm
    const = lambda i: (0, 0)
    pos3 = pos.reshape(n_tiles, 1, TOP_K * tm)
    return pl.pallas_call(
        functools.partial(_combine_kernel, alpha=alpha),
        grid=(n_tiles,),
        in_specs=[
            pl.BlockSpec((1, 1, TOP_K * tm), lambda i: (i, 0, 0), memory_space=pltpu.SMEM),
            pl.BlockSpec((1, 1, TOP_K * tm), lambda i: (jnp.minimum(i + 1, n_tiles - 1), 0, 0),
                         memory_space=pltpu.SMEM),
            pl.BlockSpec(memory_space=pl.ANY),
            pl.BlockSpec((tm, d), lambda i: (i, 0)),
            pl.BlockSpec((tm, LANES), lambda i: (i, 0)),
            pl.BlockSpec((1, 1, d), lambda i: (i // tiles_per_batch, 0, 0)),
            pl.BlockSpec((1, d), const),
            pl.BlockSpec((1, d), const),
        ],
        out_specs=pl.BlockSpec((tm, d), lambda i: (i, 0)),
        out_shape=jax.ShapeDtypeStruct((n_tok, d), F32),
        scratch_shapes=[
            pltpu.VMEM((2, TOP_K, tm * TOKEN_ROWS, d // TOKEN_ROWS), F32),
            pltpu.SemaphoreType.DMA((2,)),
        ],
        compiler_params=_cparams(("arbitrary",)),
        name="moe_combine",
    )(pos3, pos3, ys, x2d, route2d, g2, ln_g, ln_b)


def _combine_project_kernel(pos_ref, pos_next_ref, ys_hbm, x_ref, route_ref, g2_ref, lng_ref, lnb_ref,
                            sc_ref, sh_ref, w_ref, *rest, alpha, d_model, gqa):
    xo_ref, qkv_ref, buf, sems = rest[-4:]
    i = pl.program_id(0)
    n = pl.num_programs(0)
    tm = ROW_TILE
    sub = buf.shape[2] // tm
    slot = i % 2

    def tile_gather(p_ref, dst_slot, j, k):
        return _tile_copy(ys_hbm, buf.at[dst_slot, k], p_ref[0, 0, TOP_K * j + k], j, sub, sems.at[dst_slot])

    @pl.when(i == 0)
    def _():
        def issue(jb, carry):
            for u in range(DMA_UNROLL):
                for k in range(TOP_K):
                    tile_gather(pos_ref, slot, jb * DMA_UNROLL + u, k).start(priority=k % 2)
            return carry

        lax.fori_loop(0, tm // DMA_UNROLL, issue, 0)

    route = route_ref[...]
    lane = lax.broadcasted_iota(jnp.int32, route.shape, 1)
    gates = [jnp.sum(jnp.where(lane == k, route, 0.0), axis=-1, keepdims=True) for k in range(TOP_K)]
    for k in range(TOP_K):
        pltpu.make_async_copy(ys_hbm.at[pl.ds(0, tm * sub)], buf.at[slot, k], sems.at[slot]).wait()
    y = _load_token_tiles(buf.at[slot, 0], tm, F32) * gates[0]
    for k in range(1, TOP_K):
        y = y + _load_token_tiles(buf.at[slot, k], tm, F32) * gates[k]
    z = alpha * x_ref[...] + (1.0 + g2_ref[0]) * y
    xn = _layer_norm_rows(z, lng_ref[...], lnb_ref[...])
    xo_ref[...] = xn
    h = (xn * (1.0 + sc_ref[0]) + sh_ref[0]).astype(BF16)

    for j in range(tm):
        for k in range(TOP_K):
            tile_gather(pos_next_ref, 1 - slot, j, k).start(priority=k % 2)

    if gqa:
        _project_gqa(h, w_ref, *rest[:6], qkv_ref, d_model)
    else:
        _project_na(h, w_ref, qkv_ref, d_model)

    @pl.when(i == n - 1)
    def _():
        for k in range(TOP_K):
            pltpu.make_async_copy(ys_hbm.at[pl.ds(0, tm * sub)], buf.at[1 - slot, k], sems.at[1 - slot]).wait()


def _combine_project(pos, ys, x2d, route2d, g2, ln_g, ln_b, sc, sh, w_bf16, gqa_extras, seq, alpha):
    n_tok, d = x2d.shape
    n = w_bf16.shape[1]
    n_out = _gqa_out_cols(d) if gqa_extras else n
    tm = ROW_TILE
    n_tiles = n_tok // tm
    tiles_per_batch = seq // tm
    const = lambda i: (0, 0)
    per_b = lambda i: (i // tiles_per_batch, 0, 0)
    pos3 = pos.reshape(n_tiles, 1, TOP_K * tm)
    extra_specs = []
    if gqa_extras:
        extra_specs = [
            pl.BlockSpec((1, QKV_CHUNK), const),
            pl.BlockSpec((1, QKV_CHUNK), const),
            pl.BlockSpec((tm, QKV_CHUNK), lambda i: (i % tiles_per_batch, 0)),
            pl.BlockSpec((tm, QKV_CHUNK), lambda i: (i % tiles_per_batch, 0)),
            pl.BlockSpec((QKV_CHUNK, LANES), const),
            pl.BlockSpec((LANES, QKV_CHUNK), const),
        ]
    return pl.pallas_call(
        functools.partial(_combine_project_kernel, alpha=alpha, d_model=d, gqa=bool(gqa_extras)),
        grid=(n_tiles,),
        in_specs=[
            pl.BlockSpec((1, 1, TOP_K * tm), lambda i: (i, 0, 0), memory_space=pltpu.SMEM),
            pl.BlockSpec((1, 1, TOP_K * tm), lambda i: (jnp.minimum(i + 1, n_tiles - 1), 0, 0),
                         memory_space=pltpu.SMEM),
            pl.BlockSpec(memory_space=pl.ANY),
            pl.BlockSpec((tm, d), lambda i: (i, 0)),
            pl.BlockSpec((tm, LANES), lambda i: (i, 0)),
            pl.BlockSpec((1, 1, d), per_b),
            pl.BlockSpec((1, d), const),
            pl.BlockSpec((1, d), const),
            pl.BlockSpec((1, 1, d), per_b),
            pl.BlockSpec((1, 1, d), per_b),
            pl.BlockSpec((d, n), const),
        ] + extra_specs,
        out_specs=[
            pl.BlockSpec((tm, d), lambda i: (i, 0)),
            pl.BlockSpec((tm, n_out), lambda i: (i, 0)),
        ],
        out_shape=[
            jax.ShapeDtypeStruct((n_tok, d), F32),
            jax.ShapeDtypeStruct((n_tok, n_out), BF16),
        ],
        scratch_shapes=[
            pltpu.VMEM((2, TOP_K, tm * TOKEN_ROWS, d // TOKEN_ROWS), F32),
            pltpu.SemaphoreType.DMA((2,)),
        ],
        compiler_params=_cparams(("arbitrary",)),
        name="combine_qkv_gqa" if gqa_extras else "combine_qkv_na",
    )(pos3, pos3, ys, x2d, route2d, g2, ln_g, ln_b, sc, sh, w_bf16, *gqa_extras)


def _routing_tables(route2d, cnt):
    n_tok = route2d.shape[0]
    n_rows = n_tok * TOP_K
    counts = cnt[0, N_GROUPS:N_GROUPS + N_EXPERTS].astype(jnp.int32)
    seg_end = jnp.cumsum(counts)
    experts = route2d[:, 2:2 + TOP_K].astype(jnp.int32)
    ranks = route2d[:, 4:4 + TOP_K].astype(jnp.int32)
    is_e = experts[:, :, None] == jnp.arange(N_EXPERTS, dtype=jnp.int32)[None, None, :]
    pos = (ranks + jnp.sum(jnp.where(is_e, (seg_end - counts)[None, None, :], 0), axis=-1)).reshape(n_rows)
    cuts = jnp.concatenate([jnp.arange(n_rows // MOE_TB, dtype=jnp.int32) * MOE_TB, seg_end - counts])
    n_items = cuts.shape[0]
    idx = jnp.arange(n_items, dtype=jnp.int32)
    rank = (jnp.sum(cuts[None, :] < cuts[:, None], axis=1)
            + jnp.sum((cuts[None, :] == cuts[:, None]) & (idx[None, :] < idx[:, None]), axis=1))
    lo_abs = jnp.sum(jnp.where(rank[None, :] == idx[:, None], cuts[None, :], 0), axis=1)
    hi_abs = jnp.concatenate([lo_abs[1:], jnp.full((1,), n_rows, jnp.int32)])
    item_blk = jnp.minimum(lo_abs // MOE_TB, n_rows // MOE_TB - 1)
    item_exp = jnp.minimum(jnp.sum(seg_end[None, :] <= lo_abs[:, None], axis=1), N_EXPERTS - 1)
    item_lo = lo_abs - item_blk * MOE_TB
    item_hi = hi_abs - item_blk * MOE_TB
    items = tuple(a.astype(jnp.int32) for a in (item_blk, item_exp, item_lo, item_hi))
    return pos, items


def _rope_tables(seq, width):
    t = jnp.arange(seq, dtype=jnp.int32)
    pos = jnp.stack([t // GRID_W, t % GRID_W], axis=-1).astype(F32)
    inv_freq = ROPE_THETA ** (-jnp.arange(ROPE_PAIRS, dtype=F32) / ROPE_PAIRS)
    ang = pos[:, :, None] * inv_freq
    cos, sin = jnp.cos(ang), jnp.sin(ang)
    cosf = jnp.stack([cos, cos], axis=2).reshape(seq, HEAD_DIM)
    sinf = jnp.stack([-sin, sin], axis=2).reshape(seq, HEAD_DIM)
    reps = width // HEAD_DIM
    return jnp.tile(cosf, (1, reps)), jnp.tile(sinf, (1, reps))


def kernel(x, c, ada_w, ada_b, ln_g, ln_b, na_w_qkv, na_rpb, na_w_o, gqa_w_qkv, gqa_q_norm, gqa_k_norm, gqa_w_o,
           moe_w_group, moe_b_group, moe_w_expert, moe_b_expert, moe_w_gate, moe_w_up, moe_w_down):
    b, s, d = x.shape
    depth = ada_w.shape[0]
    n_tok = b * s
    rows = s // GRID_W
    kvd = GQA_KV_HEADS * HEAD_DIM
    alpha = (2 * depth) ** 0.25

    mod = _ada_modulation(c, ada_w, ada_b).reshape(depth, b, 6, 1, d)

    cosf, sinf = _rope_tables(s, QKV_CHUNK)
    head_of_lane = jnp.arange(QKV_CHUNK, dtype=jnp.int32) // HEAD_DIM
    seg_t = (head_of_lane[None, :] == jnp.arange(LANES, dtype=jnp.int32)[:, None])
    seg = (seg_t.T.astype(F32) / HEAD_DIM).astype(BF16)
    seg_t = seg_t.astype(BF16)
    tok = jnp.arange(ROW_TILE, dtype=jnp.int32)
    before = (tok[None, :] < tok[:, None]).astype(BF16)

    def qkv_params(i):
        j = i // 2
        if i % 2 == 0:
            return na_w_qkv[j].astype(BF16), ()
        reps = QKV_CHUNK // HEAD_DIM
        return gqa_w_qkv[j].astype(BF16), (jnp.tile(gqa_q_norm[j], reps)[None, :],
                                           jnp.tile(gqa_k_norm[j], reps)[None, :], cosf, sinf, seg, seg_t)

    qkv = None
    for i in range(depth):
        sh1, sc1, g1, sh2, sc2, g2 = (mod[i, :, k] for k in range(6))
        j = i // 2
        if qkv is None:
            w_bf16, extras = qkv_params(i)
            qkv = _qkv_gqa(x, sc1, sh1, w_bf16, *extras) if extras else _qkv_na(x, sc1, sh1, w_bf16)
        if i % 2 == 0:
            attn = _na_attention(qkv, _na_bias_table(na_rpb[j]), rows)
            w_o = na_w_o
        else:
            attn = _gqa_attention(qkv, d, gqa_q_norm[j], gqa_k_norm[j])
            w_o = gqa_w_o

        w_r = jnp.concatenate([moe_w_group[i], moe_w_expert[i]], axis=1)
        w_r = jnp.pad(w_r, ((0, 0), (0, LANES - w_r.shape[1])))
        wr_hi, wr_lo = _split_bf16(w_r)
        b_r = jnp.concatenate([moe_b_group[i], moe_b_expert[i]])
        b_r = jnp.pad(b_r, (0, LANES - b_r.shape[0]))[None, :]
        x, h2, route, cnt = _post_attn(attn, w_o, j, x, g1, sc2, sh2, ln_g[i, 0][None, :],
                                       ln_b[i, 0][None, :], jnp.concatenate([wr_hi, wr_lo], axis=1), b_r, before,
                                       alpha)

        route2d = route.reshape(n_tok, LANES)
        pos, items = _routing_tables(route2d, cnt)
        xs = _dispatch(pos, h2, n_tok)
        ys = _expert_mlps(items, xs, moe_w_gate, moe_w_up, moe_w_down, i)
        lng, lnb = ln_g[i, 1][None, :], ln_b[i, 1][None, :]
        if i + 1 < depth:
            w_bf16, extras = qkv_params(i + 1)
            x2d, qkv2d = _combine_project(pos, ys, x.reshape(n_tok, d), route2d, g2, lng, lnb,
                                          mod[i + 1, :, 1], mod[i + 1, :, 0], w_bf16, extras, s, alpha)
            x, qkv = x2d.reshape(b, s, d), qkv2d.reshape(b, s, -1)
        else:
            x = _combine(pos, ys, x.reshape(n_tok, d), route2d, g2, lng, lnb, s, alpha).reshape(b, s, d)
    return x
```

```python
import functools

import jax
import jax.numpy as jnp
from jax import lax
from jax.experimental import pallas as pl
from jax.experimental.pallas import tpu as pltpu

GRID_W = 64
N_HEADS = 16
HEAD_DIM = 64
NA_ROWS = 8
NA_COLS = 16
GQA_KV_HEADS = 4
ROPE_THETA = 10000.0
ROPE_PAIRS = HEAD_DIM // 4
N_GROUPS = 4
EXPERTS_PER_GROUP = 8
N_EXPERTS = N_GROUPS * EXPERTS_PER_GROUP
TOP_K = 2
LN_EPS = 1e-5
RMS_EPS = 1e-6

LANES = 128
TOKEN_ROWS = 8
HEAD_PAIR = 2 * HEAD_DIM
VMEM_LIMIT_BYTES = 52 * 1024 * 1024
ADA_TN = 1536
ROW_TILE = 512
QKV_CHUNK = 512
GQA_TQ = 1024
NA_ROWS_PER_STEP = 4
NA_STAGE_PAIRS = 4
MOE_TB = 256
POS_TILE = ROW_TILE
DISPATCH_TM = 2048
COMBINE_TM = POS_TILE
DMA_UNROLL = 8
NEG_BIG = -1e30
LOG2_E = 1.4426950408889634
Q_SCALE = (HEAD_DIM ** -0.5) * LOG2_E
GQA_BOUND_MARGIN = 1.02
GQA_BOUND_MAX = 50.0

F32 = jnp.float32
BF16 = jnp.bfloat16


def _cparams(sem):
    return pltpu.CompilerParams(dimension_semantics=sem, vmem_limit_bytes=VMEM_LIMIT_BYTES)


def _split_bf16(a):
    hi = a.astype(BF16)
    lo = (a - hi.astype(F32)).astype(BF16)
    return hi, lo


def _ada_kernel(c_ref, w_ref, b_ref, o_ref):
    c = c_ref[...]
    ca = (c * jax.nn.sigmoid(c)).astype(BF16)
    o_ref[0] = jnp.dot(ca, w_ref[0].astype(BF16), preferred_element_type=F32) + b_ref[0]


def _ada_modulation(c, ada_w, ada_b):
    depth, d, n = ada_w.shape
    b = c.shape[0]
    return pl.pallas_call(
        _ada_kernel,
        grid=(depth, n // ADA_TN),
        in_specs=[
            pl.BlockSpec((b, d), lambda i, j: (0, 0)),
            pl.BlockSpec((1, d, ADA_TN), lambda i, j: (i, 0, j)),
            pl.BlockSpec((1, 1, ADA_TN), lambda i, j: (i, 0, j)),
        ],
        out_specs=pl.BlockSpec((1, b, ADA_TN), lambda i, j: (i, 0, j)),
        out_shape=jax.ShapeDtypeStruct((depth, b, n), F32),
        compiler_params=_cparams(("arbitrary", "arbitrary")),
        name="ada_modulation",
    )(c, ada_w, ada_b.reshape(depth, 1, n))


def _project_na(h, w_ref, o_ref, d_model):
    n = w_ref.shape[1]
    for n0 in range(0, n, QKV_CHUNK):
        acc = jnp.dot(h, w_ref[:, n0:n0 + QKV_CHUNK], preferred_element_type=F32)
        if n0 < d_model:
            acc = acc * Q_SCALE
        o_ref[:, n0:n0 + QKV_CHUNK] = acc.astype(BF16)


def _qkv_na_kernel(x_ref, sc_ref, sh_ref, w_ref, o_ref, *, d_model):
    h = (x_ref[0] * (1.0 + sc_ref[0]) + sh_ref[0]).astype(BF16)
    _project_na(h, w_ref, o_ref.at[0], d_model)


def _qkv_na(x, sc, sh, w_bf16):
    b, s, d = x.shape
    n = w_bf16.shape[1]
    return pl.pallas_call(
        functools.partial(_qkv_na_kernel, d_model=d),
        grid=(b, s // ROW_TILE),
        in_specs=[
            pl.BlockSpec((1, ROW_TILE, d), lambda i, j: (i, j, 0)),
            pl.BlockSpec((1, 1, d), lambda i, j: (i, 0, 0)),
            pl.BlockSpec((1, 1, d), lambda i, j: (i, 0, 0)),
            pl.BlockSpec((d, n), lambda i, j: (0, 0)),
        ],
        out_specs=pl.BlockSpec((1, ROW_TILE, n), lambda i, j: (i, j, 0)),
        out_shape=jax.ShapeDtypeStruct((b, s, n), BF16),
        compiler_params=_cparams(("arbitrary", "arbitrary")),
        name="qkv_na",
    )(x, sc, sh, w_bf16)


N_DR = 2 * NA_ROWS - 1
N_DC = 2 * NA_COLS - 1


def _na_bias_kernel(r_ref, m_ref, valid_ref, o_ref):
    r = r_ref[...]
    hi = r.astype(BF16)
    r1 = r - hi.astype(F32)
    mid = r1.astype(BF16)
    lo = (r1 - mid.astype(F32)).astype(BF16)
    m = m_ref[...]
    acc = (jnp.dot(hi, m, preferred_element_type=F32) + jnp.dot(mid, m, preferred_element_type=F32)
           + jnp.dot(lo, m, preferred_element_type=F32))
    o_ref[...] = jnp.where(valid_ref[...] > 0.0, acc * LOG2_E, NEG_BIG)


def _na_bias_table(rpb):
    qc = jnp.arange(GRID_W, dtype=jnp.int32)
    kc = jnp.arange(GRID_W, dtype=jnp.int32)
    win_start = jnp.clip(qc - NA_COLS // 2, 0, GRID_W - NA_COLS)
    valid = (kc[None, :] >= win_start[:, None]) & (kc[None, :] < win_start[:, None] + NA_COLS)
    dc = jnp.clip(kc[None, :] - qc[:, None] + NA_COLS - 1, 0, N_DC - 1)
    onehot = (jnp.arange(LANES, dtype=jnp.int32)[:, None] == dc.reshape(1, GRID_W * GRID_W)).astype(BF16)
    r2d = jnp.pad(rpb.reshape(N_HEADS * N_DR, N_DC).astype(F32), ((0, 0), (0, LANES - N_DC)))
    tiles = pl.pallas_call(
        _na_bias_kernel,
        out_shape=jax.ShapeDtypeStruct((N_HEADS * N_DR, GRID_W * GRID_W), F32),
        compiler_params=pltpu.CompilerParams(vmem_limit_bytes=VMEM_LIMIT_BYTES),
        name="na_bias_tiles",
    )(r2d, onehot, valid.reshape(1, GRID_W * GRID_W).astype(F32))
    tiles = tiles.reshape(N_HEADS, N_DR, GRID_W, GRID_W)
    pairs = jnp.concatenate([tiles[:, :N_DR - 1], tiles[:, 1:]], axis=-1)
    pairs = pairs.reshape(N_HEADS // 2, 2, N_DR - 1, GRID_W, HEAD_PAIR).transpose(0, 2, 1, 3, 4)
    return pairs.reshape(N_HEADS // 2, N_DR - 1, 2 * GRID_W, HEAD_PAIR)


def _na_kernel(q_ref, k_ref, v_ref, bias_ref, o_ref, *, rows):
    nk = NA_ROWS * GRID_W
    lane = lax.broadcasted_iota(jnp.int32, (GRID_W, HEAD_PAIR), 1)
    first = lane < HEAD_DIM
    for sub_row in range(NA_ROWS_PER_STEP):
        r = pl.program_id(1) * NA_ROWS_PER_STEP + sub_row
        row_start = jnp.clip(r - NA_ROWS // 2, 0, rows - NA_ROWS)
        start = pl.multiple_of(row_start * GRID_W, GRID_W)
        dr_base = row_start - r + (NA_ROWS - 1)
        q_rows = slice(sub_row * GRID_W, (sub_row + 1) * GRID_W)
        for hp0 in range(0, N_HEADS // 2, NA_STAGE_PAIRS):
            pairs = range(hp0, hp0 + NA_STAGE_PAIRS)
            cols = {hp: slice(hp * HEAD_PAIR, (hp + 1) * HEAD_PAIR) for hp in pairs}
            s_all = {}
            for hp in pairs:
                q2 = q_ref[0, q_rows, cols[hp]]
                k2 = k_ref[0, pl.ds(start, nk), cols[hp]]
                zero = jnp.zeros_like(q2)
                qs = jnp.concatenate([jnp.where(first, q2, zero), jnp.where(first, zero, q2)], axis=0)
                s = lax.dot_general(qs, k2, (((1,), (1,)), ((), ())), preferred_element_type=F32)
                s_all[hp] = s + jnp.concatenate([bias_ref[hp, dr_base + 2 * m] for m in range(NA_ROWS // 2)],
                                                axis=1)
            m_all = {hp: jnp.max(s_all[hp], axis=-1, keepdims=True) for hp in pairs}
            p_all = {hp: jnp.exp2(s_all[hp] - m_all[hp]) for hp in pairs}
            l_all = {hp: jnp.sum(p_all[hp], axis=-1, keepdims=True) for hp in pairs}
            o_all = {hp: jnp.dot(p_all[hp].astype(BF16), v_ref[0, pl.ds(start, nk), cols[hp]],
                                 preferred_element_type=F32) for hp in pairs}
            for hp in pairs:
                o = o_all[hp] / l_all[hp]
                o_ref[0, q_rows, cols[hp]] = jnp.where(first, o[:GRID_W], o[GRID_W:]).astype(BF16)


def _na_attention(qkv, bias, rows):
    b, s, n3 = qkv.shape
    d = n3 // 3
    assert rows >= NA_ROWS and NA_ROWS % 2 == 0 and rows % NA_ROWS_PER_STEP == 0
    q_tile = NA_ROWS_PER_STEP * GRID_W
    return pl.pallas_call(
        functools.partial(_na_kernel, rows=rows),
        grid=(b, rows // NA_ROWS_PER_STEP),
        in_specs=[
            pl.BlockSpec((1, q_tile, d), lambda i, r: (i, r, 0)),
            pl.BlockSpec((1, s, d), lambda i, r: (i, 0, 1)),
            pl.BlockSpec((1, s, d), lambda i, r: (i, 0, 2)),
            pl.BlockSpec(bias.shape, lambda i, r: (0, 0, 0, 0)),
        ],
        out_specs=pl.BlockSpec((1, q_tile, d), lambda i, r: (i, r, 0)),
        out_shape=jax.ShapeDtypeStruct((b, s, d), BF16),
        compiler_params=_cparams(("arbitrary", "arbitrary")),
        name="na_attention",
    )(qkv, qkv, qkv, bias)


def _rms_rope(a, gain, cosf, sinf, seg, seg_t):
    qq_hi, qq_lo = _split_bf16(a * a)
    ms = jnp.dot(qq_hi, seg, preferred_element_type=F32) + jnp.dot(qq_lo, seg, preferred_element_type=F32)
    r_hi, r_lo = _split_bf16(lax.rsqrt(ms + RMS_EPS))
    rf = jnp.dot(r_hi, seg_t, preferred_element_type=F32) + jnp.dot(r_lo, seg_t, preferred_element_type=F32)
    an = a * rf * gain
    n = a.shape[1]
    lane = lax.broadcasted_iota(jnp.int32, a.shape, 1)
    partner = jnp.where(lane % (2 * ROPE_PAIRS) < ROPE_PAIRS,
                        pltpu.roll(an, n - ROPE_PAIRS, 1), pltpu.roll(an, ROPE_PAIRS, 1))
    return an * cosf + partner * sinf


def _spread_heads(a, fill):
    lane = lax.broadcasted_iota(jnp.int32, (a.shape[0], HEAD_PAIR), 1)
    low = lane < HEAD_DIM
    out = []
    for g in range(a.shape[1] // HEAD_DIM):
        blk = a[:, (g // 2) * HEAD_PAIR:(g // 2 + 1) * HEAD_PAIR]
        swapped = pltpu.roll(blk, HEAD_DIM, 1)
        head_low, head_high = (blk, swapped) if g % 2 == 0 else (swapped, blk)
        out.append(jnp.where(low, head_low, head_high if fill is None else fill))
    return jnp.concatenate(out, axis=1)


def _gqa_out_cols(d_model):
    return d_model + 2 * GQA_KV_HEADS * HEAD_PAIR


def _project_gqa(h, w_ref, qg_ref, kg_ref, cos_ref, sin_ref, seg_ref, segt_ref, o_ref, d_model):
    kvd = GQA_KV_HEADS * HEAD_DIM
    cosf, sinf = cos_ref[...], sin_ref[...]
    seg, seg_t = seg_ref[...], segt_ref[...]
    for n0 in range(0, d_model, QKV_CHUNK):
        acc = jnp.dot(h, w_ref[:, n0:n0 + QKV_CHUNK], preferred_element_type=F32)
        acc = _rms_rope(acc, qg_ref[...], cosf, sinf, seg, seg_t) * Q_SCALE
        o_ref[:, n0:n0 + QKV_CHUNK] = acc.astype(BF16)
    kv = jnp.dot(h, w_ref[:, d_model:d_model + 2 * kvd], preferred_element_type=F32)
    k = _rms_rope(kv[:, :kvd], kg_ref[:, :kvd], cosf[:, :kvd], sinf[:, :kvd], seg[:kvd], seg_t[:, :kvd])
    wide = GQA_KV_HEADS * HEAD_PAIR
    o_ref[:, d_model:d_model + wide] = _spread_heads(k, None).astype(BF16)
    o_ref[:, d_model + wide:d_model + 2 * wide] = _spread_heads(kv[:, kvd:], 1.0).astype(BF16)


def _qkv_gqa_kernel(x_ref, sc_ref, sh_ref, w_ref, qg_ref, kg_ref, cos_ref, sin_ref, seg_ref, segt_ref, o_ref,
                    *, d_model):
    h = (x_ref[0] * (1.0 + sc_ref[0]) + sh_ref[0]).astype(BF16)
    _project_gqa(h, w_ref, qg_ref, kg_ref, cos_ref, sin_ref, seg_ref, segt_ref, o_ref.at[0], d_model)


def _qkv_gqa(x, sc, sh, w_bf16, q_gain, k_gain, cosf, sinf, seg, seg_t):
    b, s, d = x.shape
    n = w_bf16.shape[1]
    const = lambda j, i: (0, 0)
    return pl.pallas_call(
        functools.partial(_qkv_gqa_kernel, d_model=d),
        grid=(s // ROW_TILE, b),
        in_specs=[
            pl.BlockSpec((1, ROW_TILE, d), lambda j, i: (i, j, 0)),
            pl.BlockSpec((1, 1, d), lambda j, i: (i, 0, 0)),
            pl.BlockSpec((1, 1, d), lambda j, i: (i, 0, 0)),
            pl.BlockSpec((d, n), const),
            pl.BlockSpec((1, QKV_CHUNK), const),
            pl.BlockSpec((1, QKV_CHUNK), const),
            pl.BlockSpec((ROW_TILE, QKV_CHUNK), lambda j, i: (j, 0)),
            pl.BlockSpec((ROW_TILE, QKV_CHUNK), lambda j, i: (j, 0)),
            pl.BlockSpec((QKV_CHUNK, LANES), const),
            pl.BlockSpec((LANES, QKV_CHUNK), const),
        ],
        out_specs=pl.BlockSpec((1, ROW_TILE, _gqa_out_cols(d)), lambda j, i: (i, j, 0)),
        out_shape=jax.ShapeDtypeStruct((b, s, _gqa_out_cols(d)), BF16),
        compiler_params=_cparams(("arbitrary", "arbitrary")),
        name="qkv_gqa",
    )(x, sc, sh, w_bf16, q_gain, k_gain, cosf, sinf, seg, seg_t)


def _gqa_kernel(bound_ref, q_ref, k_ref, v_ref, o_ref, *, use_bound):
    k2 = k_ref[0]
    v2 = v_ref[0]
    tq = q_ref.shape[1]
    lane = lax.broadcasted_iota(jnp.int32, (tq, HEAD_PAIR), 1)
    first = lane < HEAD_DIM
    for pair in range(q_ref.shape[2] // HEAD_PAIR):
        cols = slice(pair * HEAD_PAIR, (pair + 1) * HEAD_PAIR)
        q2 = q_ref[0, :, cols]
        zero = jnp.zeros_like(q2)
        qs = jnp.concatenate([jnp.where(first, q2, zero), jnp.where(first, zero, q2)], axis=0)
        s = lax.dot_general(qs, k2, (((1,), (1,)), ((), ())), preferred_element_type=F32)
        m = bound_ref[0] if use_bound else jnp.max(s, axis=-1, keepdims=True)
        p = jnp.exp2(s - m).astype(BF16)
        o = jnp.dot(p, v2, preferred_element_type=F32)
        o_a, o_b = o[:tq], o[tq:]
        out = jnp.where(first, o_a / pltpu.roll(o_a, HEAD_DIM, 1), pltpu.roll(o_b, HEAD_DIM, 1) / o_b)
        o_ref[0, :, cols] = out.astype(BF16)


def _gqa_logit_bound(q_gain, k_gain):
    return (GQA_BOUND_MARGIN * LOG2_E * (HEAD_DIM ** 0.5)
            * jnp.max(jnp.abs(q_gain)) * jnp.max(jnp.abs(k_gain))).astype(F32).reshape(1)


def _gqa_attention(qkv, d_model, q_gain, k_gain):
    bound = _gqa_logit_bound(q_gain, k_gain)
    return lax.cond(bound[0] <= GQA_BOUND_MAX,
                    functools.partial(_gqa_attention_call, d_model=d_model, use_bound=True),
                    functools.partial(_gqa_attention_call, d_model=d_model, use_bound=False),
                    bound, qkv)


def _gqa_attention_call(bound, qkv, *, d_model, use_bound):
    b, s, n = qkv.shape
    group_cols = d_model // GQA_KV_HEADS
    k_block0 = d_model // HEAD_PAIR
    v_block0 = k_block0 + GQA_KV_HEADS
    return pl.pallas_call(
        functools.partial(_gqa_kernel, use_bound=use_bound),
        grid=(b, GQA_KV_HEADS, s // GQA_TQ),
        in_specs=[
            pl.BlockSpec(memory_space=pltpu.SMEM),
            pl.BlockSpec((1, GQA_TQ, group_cols), lambda i, g, t: (i, t, g)),
            pl.BlockSpec((1, s, HEAD_PAIR), lambda i, g, t: (i, 0, k_block0 + g)),
            pl.BlockSpec((1, s, HEAD_PAIR), lambda i, g, t: (i, 0, v_block0 + g)),
        ],
        out_specs=pl.BlockSpec((1, GQA_TQ, group_cols), lambda i, g, t: (i, t, g)),
        out_shape=jax.ShapeDtypeStruct((b, s, d_model), BF16),
        compiler_params=_cparams(("arbitrary", "arbitrary", "arbitrary")),
        name="gqa_attention_bounded" if use_bound else "gqa_attention",
    )(bound, qkv, qkv, qkv)


def _store_token_tiles(ref, val, first_token=0):
    t, d = val.shape
    width = ref.shape[-1]
    sub = d // width
    for s in range(sub):
        ref[pl.ds(first_token * sub + s, t, stride=sub), :] = val[:, s * width:(s + 1) * width]


def _load_token_tiles(ref, t, dtype):
    sub = ref.shape[0] // t
    return jnp.concatenate([ref[pl.ds(s, t, stride=sub), :].astype(dtype) for s in range(sub)], axis=1)


def _tile_copy(src_hbm, dst_vmem, src_tok, dst_tok, sub, sem):
    return pltpu.make_async_copy(src_hbm.at[pl.ds(pl.multiple_of(src_tok * sub, sub), sub)],
                                 dst_vmem.at[pl.ds(pl.multiple_of(dst_tok * sub, sub), sub)], sem)


def _layer_norm_rows(z, g, b):
    mu = jnp.mean(z, axis=-1, keepdims=True)
    zc = z - mu
    var = jnp.mean(zc * zc, axis=-1, keepdims=True)
    return zc * lax.rsqrt(var + LN_EPS) * g + b


def _router(lg):
    lane = lax.broadcasted_iota(jnp.int32, lg.shape, 1)
    lanef = lane.astype(F32)
    far = float(LANES)
    gmask = lane < N_GROUPS
    lgm = jnp.where(gmask, lg, NEG_BIG)
    mg = jnp.max(lgm, axis=-1, keepdims=True)
    sg = jnp.sum(jnp.exp(lgm - mg), axis=-1, keepdims=True)
    g_prob = 1.0 / sg
    g_idx = jnp.min(jnp.where(lgm == mg, lanef, far), axis=-1, keepdims=True)
    e_lo = N_GROUPS + g_idx * EXPERTS_PER_GROUP
    emask = (lanef >= e_lo) & (lanef < e_lo + EXPERTS_PER_GROUP)
    lem = jnp.where(emask, lg, NEG_BIG)
    me = jnp.max(lem, axis=-1, keepdims=True)
    ee = jnp.exp(lem - me)
    pe = jnp.where(emask, ee / jnp.sum(ee, axis=-1, keepdims=True), -1.0)
    p1 = jnp.max(pe, axis=-1, keepdims=True)
    i1 = jnp.min(jnp.where(pe == p1, lanef, far), axis=-1, keepdims=True)
    pe2 = jnp.where(lanef == i1, -1.0, pe)
    p2 = jnp.max(pe2, axis=-1, keepdims=True)
    i2 = jnp.min(jnp.where(pe2 == p2, lanef, far), axis=-1, keepdims=True)
    den = p1 + p2
    gate1 = g_prob * (p1 / den)
    gate2 = g_prob * (p2 / den)
    out = jnp.where(lane == 0, gate1, 0.0)
    out = jnp.where(lane == 1, gate2, out)
    out = jnp.where(lane == 2, i1 - N_GROUPS, out)
    out = jnp.where(lane == 3, i2 - N_GROUPS, out)
    return out, lanef == i1, lanef == i2


def _post_attn_kernel(a_ref, wo_ref, x_ref, g1_ref, sc2_ref, sh2_ref, lng_ref, lnb_ref, wr_ref, br_ref,
                      before_ref, xo_ref, h2_ref, route_ref, cnt_ref, run_cnt, wo_bf16, *, alpha):
    @pl.when((pl.program_id(0) == 0) & (pl.program_id(1) == 0))
    def _():
        run_cnt[...] = jnp.zeros_like(run_cnt)
        wo_bf16[...] = wo_ref[0].astype(BF16)

    y = jnp.dot(a_ref[0], wo_bf16[...], preferred_element_type=F32)
    z = alpha * x_ref[0] + (1.0 + g1_ref[0]) * y
    xn = _layer_norm_rows(z, lng_ref[...], lnb_ref[...])
    xo_ref[0] = xn
    h2 = xn * (1.0 + sc2_ref[0]) + sh2_ref[0]
    _store_token_tiles(h2_ref, h2)
    h_hi, h_lo = _split_bf16(h2)
    t = jnp.dot(h_hi, wr_ref[...], preferred_element_type=F32)
    lg = (t[:, :LANES] + t[:, LANES:] + jnp.dot(h_lo, wr_ref[:, :LANES], preferred_element_type=F32)
          + br_ref[...])
    route, m1, m2 = _router(lg)
    mem = jnp.where(m1 | m2, 1.0, 0.0)
    rank = run_cnt[0:1, :] + jnp.dot(before_ref[...], mem.astype(BF16), preferred_element_type=F32)
    rank1 = jnp.sum(jnp.where(m1, rank, 0.0), axis=-1, keepdims=True)
    rank2 = jnp.sum(jnp.where(m2, rank, 0.0), axis=-1, keepdims=True)
    lane = lax.broadcasted_iota(jnp.int32, route.shape, 1)
    route_ref[0] = jnp.where(lane == 4, rank1, jnp.where(lane == 5, rank2, route))
    run_cnt[...] = run_cnt[...] + jnp.sum(mem, axis=0, keepdims=True)
    cnt_ref[...] = run_cnt[...]


def _post_attn(a, w_o, layer, x, g1, sc2, sh2, ln_g, ln_b, wr, br, before, alpha):
    b, s, d = x.shape
    sub, width = TOKEN_ROWS, d // TOKEN_ROWS
    tiles_per_batch = s // ROW_TILE
    const = lambda i, j: (0, 0)
    tile = lambda i, j: (i, j, 0)
    per_b = lambda i, j: (i, 0, 0)
    return pl.pallas_call(
        functools.partial(_post_attn_kernel, alpha=alpha),
        grid=(b, s // ROW_TILE),
        in_specs=[
            pl.BlockSpec((1, ROW_TILE, d), tile),
            pl.BlockSpec((1, d, d), lambda i, j: (layer, 0, 0), pipeline_mode=pl.Buffered(1)),
            pl.BlockSpec((1, ROW_TILE, d), tile),
            pl.BlockSpec((1, 1, d), per_b),
            pl.BlockSpec((1, 1, d), per_b),
            pl.BlockSpec((1, 1, d), per_b),
            pl.BlockSpec((1, d), const),
            pl.BlockSpec((1, d), const),
            pl.BlockSpec((d, 2 * LANES), const),
            pl.BlockSpec((1, LANES), const),
            pl.BlockSpec((ROW_TILE, ROW_TILE), const),
        ],
        out_specs=[
            pl.BlockSpec((1, ROW_TILE, d), tile),
            pl.BlockSpec((ROW_TILE * sub, width), lambda i, j: (i * tiles_per_batch + j, 0)),
            pl.BlockSpec((1, ROW_TILE, LANES), tile),
            pl.BlockSpec((8, LANES), const),
        ],
        out_shape=[
            jax.ShapeDtypeStruct((b, s, d), F32),
            jax.ShapeDtypeStruct((b * s * sub, width), F32),
            jax.ShapeDtypeStruct((b, s, LANES), F32),
            jax.ShapeDtypeStruct((8, LANES), F32),
        ],
        scratch_shapes=[pltpu.VMEM((8, LANES), F32), pltpu.VMEM((d, d), BF16)],
        compiler_params=_cparams(("arbitrary", "arbitrary")),
        name="post_attn",
    )(a, w_o, x, g1, sc2, sh2, ln_g, ln_b, wr, br, before)


def _dispatch_kernel(pos_ref, h_ref, xs_hbm, sem):
    tm = DISPATCH_TM
    sub = h_ref.shape[0] // tm

    for t in range(tm // POS_TILE):
        def issue(jb, carry, t=t):
            for u in range(DMA_UNROLL):
                j = jb * DMA_UNROLL + u
                src = h_ref.at[pl.ds(pl.multiple_of((t * POS_TILE + j) * sub, sub), sub)]
                for k in range(TOP_K):
                    row = pos_ref[t, 0, TOP_K * j + k]
                    dst = xs_hbm.at[pl.ds(pl.multiple_of(row * sub, sub), sub)]
                    pltpu.make_async_copy(src, dst, sem).start(priority=k % 2)
            return carry

        lax.fori_loop(0, POS_TILE // DMA_UNROLL, issue, 0)
    for _ in range(TOP_K):
        pltpu.make_async_copy(h_ref, xs_hbm.at[pl.ds(0, tm * sub)], sem).wait()


def _dispatch(pos3, h2_tiles, n_tok):
    sub, width = h2_tiles.shape[0] // n_tok, h2_tiles.shape[1]
    tm = DISPATCH_TM
    n_tiles = n_tok // tm
    return pl.pallas_call(
        _dispatch_kernel,
        grid=(n_tiles,),
        in_specs=[
            pl.BlockSpec((tm // POS_TILE, 1, TOP_K * POS_TILE), lambda i: (i, 0, 0), memory_space=pltpu.SMEM),
            pl.BlockSpec((tm * sub, width), lambda i: (i, 0)),
        ],
        out_specs=pl.BlockSpec(memory_space=pl.ANY),
        out_shape=jax.ShapeDtypeStruct((n_tok * TOP_K * sub, width), F32),
        scratch_shapes=[pltpu.SemaphoreType.DMA],
        compiler_params=_cparams(("arbitrary",)),
        name="moe_dispatch",
    )(pos3, h2_tiles)


def _expert_kernel(blk_ref, exp_ref, lo_ref, hi_ref, x_ref, wg_ref, wu_ref, wd_ref, y_ref,
                   acc, wgb, wub, wdb, cast_expert, state):
    i = pl.program_id(0)
    n_items = pl.num_programs(0)
    tb = MOE_TB
    prev = jnp.maximum(i - 1, 0)
    nxt = jnp.minimum(i + 1, n_items - 1)
    first = (i == 0) | (blk_ref[i] != blk_ref[prev])
    last = (i == n_items - 1) | (blk_ref[i] != blk_ref[nxt])
    lo, hi = lo_ref[i], hi_ref[i]
    whole = (lo == 0) & (hi == tb)

    @pl.when(i == 0)
    def _():
        cast_expert[0] = -1

    @pl.when(first)
    def _():
        state[0] = 0
        state[1] = 0

    @pl.when(hi > lo)
    def _():
        @pl.when(cast_expert[0] != exp_ref[i])
        def _():
            wgb[...] = wg_ref[0, 0].astype(BF16)
            wub[...] = wu_ref[0, 0].astype(BF16)
            wdb[...] = wd_ref[0, 0].astype(BF16)
            cast_expert[0] = exp_ref[i]

        xb = _load_token_tiles(x_ref, tb, BF16)
        g = jnp.dot(xb, wgb[...], preferred_element_type=F32)
        u = jnp.dot(xb, wub[...], preferred_element_type=F32)
        act = (g * jax.nn.sigmoid(g) * u).astype(BF16)
        y = jnp.dot(act, wdb[...], preferred_element_type=F32)

        @pl.when(whole)
        def _():
            _store_token_tiles(y_ref, y)
            state[0] = 1

        @pl.when(jnp.logical_not(whole))
        def _():
            row = lax.broadcasted_iota(jnp.int32, (tb, 1), 0)
            part = jnp.where((row >= lo) & (row < hi), y, 0.0)

            @pl.when(state[1] == 0)
            def _():
                acc[...] = part

            @pl.when(state[1] != 0)
            def _():
                acc[...] += part

            state[1] = 1

    @pl.when(last & (state[0] == 0))
    def _():
        _store_token_tiles(y_ref, acc[...])


def _expert_mlps(items, xs_tiles, w_gate, w_up, w_down, layer):
    item_blk, item_exp, item_lo, item_hi = items
    n_items = item_blk.shape[0]
    _, _, d, de = w_gate.shape
    sub, width = TOKEN_ROWS, d // TOKEN_ROWS
    w_index = lambda i, blk, ex, lo, hi: (layer, ex[i], 0, 0)
    grid_spec = pltpu.PrefetchScalarGridSpec(
        num_scalar_prefetch=4,
        grid=(n_items,),
        in_specs=[
            pl.BlockSpec((MOE_TB * sub, width), lambda i, blk, ex, lo, hi: (blk[i], 0)),
            pl.BlockSpec((1, 1, d, de), w_index),
            pl.BlockSpec((1, 1, d, de), w_index),
            pl.BlockSpec((1, 1, de, d), w_index),
        ],
        out_specs=pl.BlockSpec((MOE_TB * sub, width), lambda i, blk, ex, lo, hi: (blk[i], 0)),
        scratch_shapes=[
            pltpu.VMEM((MOE_TB, d), F32),
            pltpu.VMEM((d, de), BF16),
            pltpu.VMEM((d, de), BF16),
            pltpu.VMEM((de, d), BF16),
            pltpu.SMEM((1,), jnp.int32),
            pltpu.SMEM((2,), jnp.int32),
        ],
    )
    return pl.pallas_call(
        _expert_kernel,
        grid_spec=grid_spec,
        out_shape=jax.ShapeDtypeStruct(xs_tiles.shape, F32),
        compiler_params=_cparams(("arbitrary",)),
        name="expert_mlps",
    )(item_blk, item_exp, item_lo, item_hi, xs_tiles, w_gate, w_up, w_down)


def _combine_kernel(pos_ref, pos_next_ref, ys_hbm, x_ref, route_ref, g2_ref, lng_ref, lnb_ref, xo_ref, buf, sems,
                    *, alpha):
    i = pl.program_id(0)
    n = pl.num_programs(0)
    tm = COMBINE_TM
    sub = buf.shape[2] // tm
    slot = i % 2

    def gather(p_ref, dst_slot):
        def issue(jb, carry):
            for u in range(DMA_UNROLL):
                j = jb * DMA_UNROLL + u
                for k in range(TOP_K):
                    _tile_copy(ys_hbm, buf.at[dst_slot, k], p_ref[0, 0, TOP_K * j + k], j, sub,
                               sems.at[dst_slot]).start(priority=k % 2)
            return carry

        lax.fori_loop(0, tm // DMA_UNROLL, issue, 0)

    @pl.when(i == 0)
    def _():
        gather(pos_ref, slot)

    @pl.when(i + 1 < n)
    def _():
        gather(pos_next_ref, 1 - slot)

    route = route_ref[...]
    lane = lax.broadcasted_iota(jnp.int32, route.shape, 1)
    gates = [jnp.sum(jnp.where(lane == k, route, 0.0), axis=-1, keepdims=True) for k in range(TOP_K)]
    for k in range(TOP_K):
        pltpu.make_async_copy(ys_hbm.at[pl.ds(0, tm * sub)], buf.at[slot, k], sems.at[slot]).wait()
    y = _load_token_tiles(buf.at[slot, 0], tm, F32) * gates[0]
    for k in range(1, TOP_K):
        y = y + _load_token_tiles(buf.at[slot, k], tm, F32) * gates[k]
    z = alpha * x_ref[...] + (1.0 + g2_ref[0]) * y
    xo_ref[...] = _layer_norm_rows(z, lng_ref[...], lnb_ref[...])


def _combine(pos3, ys, x2d, route2d, g2, ln_g, ln_b, seq, alpha):
    n_tok, d = x2d.shape
    tm = COMBINE_TM
    n_tiles = n_tok // tm
    tiles_per_batch = seq // tm
    const = lambda i: (0, 0)
    return pl.pallas_call(
        functools.partial(_combine_kernel, alpha=alpha),
        grid=(n_tiles,),
        in_specs=[
            pl.BlockSpec((1, 1, TOP_K * tm), lambda i: (i, 0, 0), memory_space=pltpu.SMEM),
            pl.BlockSpec((1, 1, TOP_K * tm), lambda i: (jnp.minimum(i + 1, n_tiles - 1), 0, 0),
                         memory_space=pltpu.SMEM),
            pl.BlockSpec(memory_space=pl.ANY),
            pl.BlockSpec((tm, d), lambda i: (i, 0)),
            pl.BlockSpec((tm, LANES), lambda i: (i, 0)),
            pl.BlockSpec((1, 1, d), lambda i: (i // tiles_per_batch, 0, 0)),
            pl.BlockSpec((1, d), const),
            pl.BlockSpec((1, d), const),
        ],
        out_specs=pl.BlockSpec((tm, d), lambda i: (i, 0)),
        out_shape=jax.ShapeDtypeStruct((n_tok, d), F32),
        scratch_shapes=[
            pltpu.VMEM((2, TOP_K, tm * TOKEN_ROWS, d // TOKEN_ROWS), F32),
            pltpu.SemaphoreType.DMA((2,)),
        ],
        compiler_params=_cparams(("arbitrary",)),
        name="moe_combine",
    )(pos3, pos3, ys, x2d, route2d, g2, ln_g, ln_b)


def _combine_project_kernel(pos_ref, pos_next_ref, ys_hbm, x_ref, route_ref, g2_ref, lng_ref, lnb_ref,
                            sc_ref, sh_ref, w_ref, *rest, alpha, d_model, gqa):
    xo_ref, qkv_ref, buf, sems = rest[-4:]
    i = pl.program_id(0)
    n = pl.num_programs(0)
    tm = ROW_TILE
    sub = buf.shape[2] // tm
    slot = i % 2

    def tile_gather(p_ref, dst_slot, j, k):
        return _tile_copy(ys_hbm, buf.at[dst_slot, k], p_ref[0, 0, TOP_K * j + k], j, sub, sems.at[dst_slot])

    @pl.when(i == 0)
    def _():
        def issue(jb, carry):
            for u in range(DMA_UNROLL):
                for k in range(TOP_K):
                    tile_gather(pos_ref, slot, jb * DMA_UNROLL + u, k).start(priority=k % 2)
            return carry

        lax.fori_loop(0, tm // DMA_UNROLL, issue, 0)

    route = route_ref[...]
    lane = lax.broadcasted_iota(jnp.int32, route.shape, 1)
    gates = [jnp.sum(jnp.where(lane == k, route, 0.0), axis=-1, keepdims=True) for k in range(TOP_K)]
    for k in range(TOP_K):
        pltpu.make_async_copy(ys_hbm.at[pl.ds(0, tm * sub)], buf.at[slot, k], sems.at[slot]).wait()
    y = _load_token_tiles(buf.at[slot, 0], tm, F32) * gates[0]
    for k in range(1, TOP_K):
        y = y + _load_token_tiles(buf.at[slot, k], tm, F32) * gates[k]
    z = alpha * x_ref[...] + (1.0 + g2_ref[0]) * y
    xn = _layer_norm_rows(z, lng_ref[...], lnb_ref[...])
    xo_ref[...] = xn
    h = (xn * (1.0 + sc_ref[0]) + sh_ref[0]).astype(BF16)

    for j in range(tm):
        for k in range(TOP_K):
            tile_gather(pos_next_ref, 1 - slot, j, k).start(priority=k % 2)

    if gqa:
        _project_gqa(h, w_ref, *rest[:6], qkv_ref, d_model)
    else:
        _project_na(h, w_ref, qkv_ref, d_model)

    @pl.when(i == n - 1)
    def _():
        for k in range(TOP_K):
            pltpu.make_async_copy(ys_hbm.at[pl.ds(0, tm * sub)], buf.at[1 - slot, k], sems.at[1 - slot]).wait()


def _combine_project(pos3, ys, x2d, route2d, g2, ln_g, ln_b, sc, sh, w_bf16, gqa_extras, seq, alpha):
    n_tok, d = x2d.shape
    n = w_bf16.shape[1]
    n_out = _gqa_out_cols(d) if gqa_extras else n
    tm = ROW_TILE
    n_tiles = n_tok // tm
    tiles_per_batch = seq // tm
    const = lambda i: (0, 0)
    per_b = lambda i: (i // tiles_per_batch, 0, 0)
    extra_specs = []
    if gqa_extras:
        extra_specs = [
            pl.BlockSpec((1, QKV_CHUNK), const),
            pl.BlockSpec((1, QKV_CHUNK), const),
            pl.BlockSpec((tm, QKV_CHUNK), lambda i: (i % tiles_per_batch, 0)),
            pl.BlockSpec((tm, QKV_CHUNK), lambda i: (i % tiles_per_batch, 0)),
            pl.BlockSpec((QKV_CHUNK, LANES), const),
            pl.BlockSpec((LANES, QKV_CHUNK), const),
        ]
    return pl.pallas_call(
        functools.partial(_combine_project_kernel, alpha=alpha, d_model=d, gqa=bool(gqa_extras)),
        grid=(n_tiles,),
        in_specs=[
            pl.BlockSpec((1, 1, TOP_K * tm), lambda i: (i, 0, 0), memory_space=pltpu.SMEM),
            pl.BlockSpec((1, 1, TOP_K * tm), lambda i: (jnp.minimum(i + 1, n_tiles - 1), 0, 0),
                         memory_space=pltpu.SMEM),
            pl.BlockSpec(memory_space=pl.ANY),
            pl.BlockSpec((tm, d), lambda i: (i, 0)),
            pl.BlockSpec((tm, LANES), lambda i: (i, 0)),
            pl.BlockSpec((1, 1, d), per_b),
            pl.BlockSpec((1, d), const),
            pl.BlockSpec((1, d), const),
            pl.BlockSpec((1, 1, d), per_b),
            pl.BlockSpec((1, 1, d), per_b),
            pl.BlockSpec((d, n), const),
        ] + extra_specs,
        out_specs=[
            pl.BlockSpec((tm, d), lambda i: (i, 0)),
            pl.BlockSpec((tm, n_out), lambda i: (i, 0)),
        ],
        out_shape=[
            jax.ShapeDtypeStruct((n_tok, d), F32),
            jax.ShapeDtypeStruct((n_tok, n_out), BF16),
        ],
        scratch_shapes=[
            pltpu.VMEM((2, TOP_K, tm * TOKEN_ROWS, d // TOKEN_ROWS), F32),
            pltpu.SemaphoreType.DMA((2,)),
        ],
        compiler_params=_cparams(("arbitrary",)),
        name="combine_qkv_gqa" if gqa_extras else "combine_qkv_na",
    )(pos3, pos3, ys, x2d, route2d, g2, ln_g, ln_b, sc, sh, w_bf16, *gqa_extras)


def _routing_tables(route2d, cnt):
    n_tok = route2d.shape[0]
    n_rows = n_tok * TOP_K
    counts = cnt[0, N_GROUPS:N_GROUPS + N_EXPERTS].astype(jnp.int32)
    seg_end = jnp.cumsum(counts)
    experts = route2d[:, 2:2 + TOP_K].astype(jnp.int32)
    ranks = route2d[:, 4:4 + TOP_K].astype(jnp.int32)
    is_e = experts[:, :, None] == jnp.arange(N_EXPERTS, dtype=jnp.int32)[None, None, :]
    pos = (ranks + jnp.sum(jnp.where(is_e, (seg_end - counts)[None, None, :], 0), axis=-1)).reshape(n_rows)
    cuts = jnp.concatenate([jnp.arange(n_rows // MOE_TB, dtype=jnp.int32) * MOE_TB, seg_end - counts])
    n_items = cuts.shape[0]
    idx = jnp.arange(n_items, dtype=jnp.int32)
    rank = (jnp.sum(cuts[None, :] < cuts[:, None], axis=1)
            + jnp.sum((cuts[None, :] == cuts[:, None]) & (idx[None, :] < idx[:, None]), axis=1))
    lo_abs = jnp.sum(jnp.where(rank[None, :] == idx[:, None], cuts[None, :], 0), axis=1)
    hi_abs = jnp.concatenate([lo_abs[1:], jnp.full((1,), n_rows, jnp.int32)])
    item_blk = jnp.minimum(lo_abs // MOE_TB, n_rows // MOE_TB - 1)
    item_exp = jnp.minimum(jnp.sum(seg_end[None, :] <= lo_abs[:, None], axis=1), N_EXPERTS - 1)
    item_lo = lo_abs - item_blk * MOE_TB
    item_hi = hi_abs - item_blk * MOE_TB
    items = tuple(a.astype(jnp.int32) for a in (item_blk, item_exp, item_lo, item_hi))
    return pos, items


def _rope_tables(seq, width):
    t = jnp.arange(seq, dtype=jnp.int32)
    pos = jnp.stack([t // GRID_W, t % GRID_W], axis=-1).astype(F32)
    inv_freq = ROPE_THETA ** (-jnp.arange(ROPE_PAIRS, dtype=F32) / ROPE_PAIRS)
    ang = pos[:, :, None] * inv_freq
    cos, sin = jnp.cos(ang), jnp.sin(ang)
    cosf = jnp.stack([cos, cos], axis=2).reshape(seq, HEAD_DIM)
    sinf = jnp.stack([-sin, sin], axis=2).reshape(seq, HEAD_DIM)
    reps = width // HEAD_DIM
    return jnp.tile(cosf, (1, reps)), jnp.tile(sinf, (1, reps))


def kernel(x, c, ada_w, ada_b, ln_g, ln_b, na_w_qkv, na_rpb, na_w_o, gqa_w_qkv, gqa_q_norm, gqa_k_norm, gqa_w_o,
           moe_w_group, moe_b_group, moe_w_expert, moe_b_expert, moe_w_gate, moe_w_up, moe_w_down):
    b, s, d = x.shape
    depth = ada_w.shape[0]
    n_tok = b * s
    rows = s // GRID_W
    kvd = GQA_KV_HEADS * HEAD_DIM
    alpha = (2 * depth) ** 0.25

    mod = _ada_modulation(c, ada_w, ada_b).reshape(depth, b, 6, 1, d)

    cosf, sinf = _rope_tables(s, QKV_CHUNK)
    head_of_lane = jnp.arange(QKV_CHUNK, dtype=jnp.int32) // HEAD_DIM
    seg_t = (head_of_lane[None, :] == jnp.arange(LANES, dtype=jnp.int32)[:, None])
    seg = (seg_t.T.astype(F32) / HEAD_DIM).astype(BF16)
    seg_t = seg_t.astype(BF16)
    tok = jnp.arange(ROW_TILE, dtype=jnp.int32)
    before = (tok[None, :] < tok[:, None]).astype(BF16)

    def qkv_params(i):
        j = i // 2
        if i % 2 == 0:
            return na_w_qkv[j].astype(BF16), ()
        reps = QKV_CHUNK // HEAD_DIM
        return gqa_w_qkv[j].astype(BF16), (jnp.tile(gqa_q_norm[j], reps)[None, :],
                                           jnp.tile(gqa_k_norm[j], reps)[None, :], cosf, sinf, seg, seg_t)

    qkv = None
    for i in range(depth):
        sh1, sc1, g1, sh2, sc2, g2 = (mod[i, :, k] for k in range(6))
        j = i // 2
        if qkv is None:
            w_bf16, extras = qkv_params(i)
            qkv = _qkv_gqa(x, sc1, sh1, w_bf16, *extras) if extras else _qkv_na(x, sc1, sh1, w_bf16)
        if i % 2 == 0:
            attn = _na_attention(qkv, _na_bias_table(na_rpb[j]), rows)
            w_o = na_w_o
        else:
            attn = _gqa_attention(qkv, d, gqa_q_norm[j], gqa_k_norm[j])
            w_o = gqa_w_o

        w_r = jnp.concatenate([moe_w_group[i], moe_w_expert[i]], axis=1)
        w_r = jnp.pad(w_r, ((0, 0), (0, LANES - w_r.shape[1])))
        wr_hi, wr_lo = _split_bf16(w_r)
        b_r = jnp.concatenate([moe_b_group[i], moe_b_expert[i]])
        b_r = jnp.pad(b_r, (0, LANES - b_r.shape[0]))[None, :]
        x, h2, route, cnt = _post_attn(attn, w_o, j, x, g1, sc2, sh2, ln_g[i, 0][None, :],
                                       ln_b[i, 0][None, :], jnp.concatenate([wr_hi, wr_lo], axis=1), b_r, before,
                                       alpha)

        route2d = route.reshape(n_tok, LANES)
        pos, items = _routing_tables(route2d, cnt)
        pos3 = pos.reshape(n_tok // POS_TILE, 1, TOP_K * POS_TILE)
        xs = _dispatch(pos3, h2, n_tok)
        ys = _expert_mlps(items, xs, moe_w_gate, moe_w_up, moe_w_down, i)
        lng, lnb = ln_g[i, 1][None, :], ln_b[i, 1][None, :]
        if i + 1 < depth:
            w_bf16, extras = qkv_params(i + 1)
            x2d, qkv2d = _combine_project(pos3, ys, x.reshape(n_tok, d), route2d, g2, lng, lnb,
                                          mod[i + 1, :, 1], mod[i + 1, :, 0], w_bf16, extras, s, alpha)
            x, qkv = x2d.reshape(b, s, d), qkv2d.reshape(b, s, -1)
        else:
            x = _combine(pos3, ys, x.reshape(n_tok, d), route2d, g2, lng, lnb, s, alpha).reshape(b, s, d)
    return x
```

```python
import functools

import jax
import jax.numpy as jnp
from jax import lax
from jax.experimental import pallas as pl
from jax.experimental.pallas import tpu as pltpu

GRID_W = 64
N_HEADS = 16
HEAD_DIM = 64
NA_ROWS = 8
NA_COLS = 16
GQA_KV_HEADS = 4
ROPE_THETA = 10000.0
ROPE_PAIRS = HEAD_DIM // 4
N_GROUPS = 4
EXPERTS_PER_GROUP = 8
N_EXPERTS = N_GROUPS * EXPERTS_PER_GROUP
TOP_K = 2
LN_EPS = 1e-5
RMS_EPS = 1e-6

LANES = 128
TOKEN_ROWS = 8
HEAD_PAIR = 2 * HEAD_DIM
VMEM_LIMIT_BYTES = 52 * 1024 * 1024
ADA_TN = 1536
ROW_TILE = 512
QKV_CHUNK = 512
GQA_TQ = 1024
NA_ROWS_PER_STEP = 4
NA_STAGE_PAIRS = 4
MOE_TB = 256
POS_TILE = ROW_TILE
DISPATCH_TM = 2048
COMBINE_TM = POS_TILE
DMA_UNROLL = 8
NEG_BIG = -1e30
LOG2_E = 1.4426950408889634
Q_SCALE = (HEAD_DIM ** -0.5) * LOG2_E
GQA_BOUND_MARGIN = 1.02
GQA_BOUND_MAX = 50.0

F32 = jnp.float32
BF16 = jnp.bfloat16


def _cparams(sem):
    return pltpu.CompilerParams(dimension_semantics=sem, vmem_limit_bytes=VMEM_LIMIT_BYTES)


def _split_bf16(a):
    hi = a.astype(BF16)
    lo = (a - hi.astype(F32)).astype(BF16)
    return hi, lo


def _ada_kernel(c_ref, w_ref, b_ref, o_ref):
    c = c_ref[...]
    ca = (c * jax.nn.sigmoid(c)).astype(BF16)
    o_ref[0] = jnp.dot(ca, w_ref[0].astype(BF16), preferred_element_type=F32) + b_ref[0]


def _ada_modulation(c, ada_w, ada_b):
    depth, d, n = ada_w.shape
    b = c.shape[0]
    return pl.pallas_call(
        _ada_kernel,
        grid=(depth, n // ADA_TN),
        in_specs=[
            pl.BlockSpec((b, d), lambda i, j: (0, 0)),
            pl.BlockSpec((1, d, ADA_TN), lambda i, j: (i, 0, j)),
            pl.BlockSpec((1, 1, ADA_TN), lambda i, j: (i, 0, j)),
        ],
        out_specs=pl.BlockSpec((1, b, ADA_TN), lambda i, j: (i, 0, j)),
        out_shape=jax.ShapeDtypeStruct((depth, b, n), F32),
        compiler_params=_cparams(("arbitrary", "arbitrary")),
        name="ada_modulation",
    )(c, ada_w, ada_b.reshape(depth, 1, n))


def _project_na(h, w_ref, o_ref, d_model):
    n = w_ref.shape[1]
    for n0 in range(0, n, QKV_CHUNK):
        acc = jnp.dot(h, w_ref[:, n0:n0 + QKV_CHUNK], preferred_element_type=F32)
        if n0 < d_model:
            acc = acc * Q_SCALE
        o_ref[:, n0:n0 + QKV_CHUNK] = acc.astype(BF16)


def _qkv_na_kernel(x_ref, sc_ref, sh_ref, w_ref, o_ref, *, d_model):
    h = (x_ref[0] * (1.0 + sc_ref[0]) + sh_ref[0]).astype(BF16)
    _project_na(h, w_ref, o_ref.at[0], d_model)


def _qkv_na(x, sc, sh, w_bf16):
    b, s, d = x.shape
    n = w_bf16.shape[1]
    return pl.pallas_call(
        functools.partial(_qkv_na_kernel, d_model=d),
        grid=(b, s // ROW_TILE),
        in_specs=[
            pl.BlockSpec((1, ROW_TILE, d), lambda i, j: (i, j, 0)),
            pl.BlockSpec((1, 1, d), lambda i, j: (i, 0, 0)),
            pl.BlockSpec((1, 1, d), lambda i, j: (i, 0, 0)),
            pl.BlockSpec((d, n), lambda i, j: (0, 0)),
        ],
        out_specs=pl.BlockSpec((1, ROW_TILE, n), lambda i, j: (i, j, 0)),
        out_shape=jax.ShapeDtypeStruct((b, s, n), BF16),
        compiler_params=_cparams(("arbitrary", "arbitrary")),
        name="qkv_na",
    )(x, sc, sh, w_bf16)


N_DR = 2 * NA_ROWS - 1
N_DC = 2 * NA_COLS - 1


def _na_bias_kernel(r_ref, m_ref, valid_ref, o_ref):
    r = r_ref[...]
    hi = r.astype(BF16)
    r1 = r - hi.astype(F32)
    mid = r1.astype(BF16)
    lo = (r1 - mid.astype(F32)).astype(BF16)
    m = m_ref[...]
    acc = (jnp.dot(hi, m, preferred_element_type=F32) + jnp.dot(mid, m, preferred_element_type=F32)
           + jnp.dot(lo, m, preferred_element_type=F32))
    o_ref[...] = jnp.where(valid_ref[...] > 0.0, acc * LOG2_E, NEG_BIG)


def _na_bias_table(rpb):
    qc = jnp.arange(GRID_W, dtype=jnp.int32)
    kc = jnp.arange(GRID_W, dtype=jnp.int32)
    win_start = jnp.clip(qc - NA_COLS // 2, 0, GRID_W - NA_COLS)
    valid = (kc[None, :] >= win_start[:, None]) & (kc[None, :] < win_start[:, None] + NA_COLS)
    dc = jnp.clip(kc[None, :] - qc[:, None] + NA_COLS - 1, 0, N_DC - 1)
    onehot = (jnp.arange(LANES, dtype=jnp.int32)[:, None] == dc.reshape(1, GRID_W * GRID_W)).astype(BF16)
    r2d = jnp.pad(rpb.reshape(N_HEADS * N_DR, N_DC).astype(F32), ((0, 0), (0, LANES - N_DC)))
    tiles = pl.pallas_call(
        _na_bias_kernel,
        out_shape=jax.ShapeDtypeStruct((N_HEADS * N_DR, GRID_W * GRID_W), F32),
        compiler_params=pltpu.CompilerParams(vmem_limit_bytes=VMEM_LIMIT_BYTES),
        name="na_bias_tiles",
    )(r2d, onehot, valid.reshape(1, GRID_W * GRID_W).astype(F32))
    tiles = tiles.reshape(N_HEADS, N_DR, GRID_W, GRID_W)
    pairs = jnp.concatenate([tiles[:, :N_DR - 1], tiles[:, 1:]], axis=-1)
    pairs = pairs.reshape(N_HEADS // 2, 2, N_DR - 1, GRID_W, HEAD_PAIR).transpose(0, 2, 1, 3, 4)
    return pairs.reshape(N_HEADS // 2, N_DR - 1, 2 * GRID_W, HEAD_PAIR)


def _na_kernel(q_ref, k_ref, v_ref, bias_ref, o_ref, *, rows):
    nk = NA_ROWS * GRID_W
    lane = lax.broadcasted_iota(jnp.int32, (GRID_W, HEAD_PAIR), 1)
    first = lane < HEAD_DIM
    for sub_row in range(NA_ROWS_PER_STEP):
        r = pl.program_id(1) * NA_ROWS_PER_STEP + sub_row
        row_start = jnp.clip(r - NA_ROWS // 2, 0, rows - NA_ROWS)
        start = pl.multiple_of(row_start * GRID_W, GRID_W)
        dr_base = row_start - r + (NA_ROWS - 1)
        q_rows = slice(sub_row * GRID_W, (sub_row + 1) * GRID_W)
        for hp0 in range(0, N_HEADS // 2, NA_STAGE_PAIRS):
            pairs = range(hp0, hp0 + NA_STAGE_PAIRS)
            cols = {hp: slice(hp * HEAD_PAIR, (hp + 1) * HEAD_PAIR) for hp in pairs}
            s_all = {}
            for hp in pairs:
                q2 = q_ref[0, q_rows, cols[hp]]
                k2 = k_ref[0, pl.ds(start, nk), cols[hp]]
                zero = jnp.zeros_like(q2)
                qs = jnp.concatenate([jnp.where(first, q2, zero), jnp.where(first, zero, q2)], axis=0)
                s = lax.dot_general(qs, k2, (((1,), (1,)), ((), ())), preferred_element_type=F32)
                s_all[hp] = s + jnp.concatenate([bias_ref[hp, dr_base + 2 * m] for m in range(NA_ROWS // 2)],
                                                axis=1)
            m_all = {hp: jnp.max(s_all[hp], axis=-1, keepdims=True) for hp in pairs}
            p_all = {hp: jnp.exp2(s_all[hp] - m_all[hp]) for hp in pairs}
            l_all = {hp: jnp.sum(p_all[hp], axis=-1, keepdims=True) for hp in pairs}
            o_all = {hp: jnp.dot(p_all[hp].astype(BF16), v_ref[0, pl.ds(start, nk), cols[hp]],
                                 preferred_element_type=F32) for hp in pairs}
            for hp in pairs:
                o = o_all[hp] / l_all[hp]
                o_ref[0, q_rows, cols[hp]] = jnp.where(first, o[:GRID_W], o[GRID_W:]).astype(BF16)


def _na_attention(qkv, bias, rows):
    b, s, n3 = qkv.shape
    d = n3 // 3
    assert rows >= NA_ROWS and NA_ROWS % 2 == 0 and rows % NA_ROWS_PER_STEP == 0
    q_tile = NA_ROWS_PER_STEP * GRID_W
    return pl.pallas_call(
        functools.partial(_na_kernel, rows=rows),
        grid=(b, rows // NA_ROWS_PER_STEP),
        in_specs=[
            pl.BlockSpec((1, q_tile, d), lambda i, r: (i, r, 0)),
            pl.BlockSpec((1, s, d), lambda i, r: (i, 0, 1)),
            pl.BlockSpec((1, s, d), lambda i, r: (i, 0, 2)),
            pl.BlockSpec(bias.shape, lambda i, r: (0, 0, 0, 0)),
        ],
        out_specs=pl.BlockSpec((1, q_tile, d), lambda i, r: (i, r, 0)),
        out_shape=jax.ShapeDtypeStruct((b, s, d), BF16),
        compiler_params=_cparams(("arbitrary", "arbitrary")),
        name="na_attention",
    )(qkv, qkv, qkv, bias)


def _rms_rope(a, gain, cosf, sinf, seg, seg_t):
    qq_hi, qq_lo = _split_bf16(a * a)
    ms = jnp.dot(qq_hi, seg, preferred_element_type=F32) + jnp.dot(qq_lo, seg, preferred_element_type=F32)
    r_hi, r_lo = _split_bf16(lax.rsqrt(ms + RMS_EPS))
    rf = jnp.dot(r_hi, seg_t, preferred_element_type=F32) + jnp.dot(r_lo, seg_t, preferred_element_type=F32)
    an = a * rf * gain
    n = a.shape[1]
    lane = lax.broadcasted_iota(jnp.int32, a.shape, 1)
    partner = jnp.where(lane % (2 * ROPE_PAIRS) < ROPE_PAIRS,
                        pltpu.roll(an, n - ROPE_PAIRS, 1), pltpu.roll(an, ROPE_PAIRS, 1))
    return an * cosf + partner * sinf


def _spread_heads(a, fill):
    lane = lax.broadcasted_iota(jnp.int32, (a.shape[0], HEAD_PAIR), 1)
    low = lane < HEAD_DIM
    out = []
    for g in range(a.shape[1] // HEAD_DIM):
        blk = a[:, (g // 2) * HEAD_PAIR:(g // 2 + 1) * HEAD_PAIR]
        swapped = pltpu.roll(blk, HEAD_DIM, 1)
        head_low, head_high = (blk, swapped) if g % 2 == 0 else (swapped, blk)
        out.append(jnp.where(low, head_low, head_high if fill is None else fill))
    return jnp.concatenate(out, axis=1)


def _gqa_out_cols(d_model):
    return d_model + 2 * GQA_KV_HEADS * HEAD_PAIR


def _project_gqa(h, w_ref, qg_ref, kg_ref, cos_ref, sin_ref, seg_ref, segt_ref, o_ref, d_model):
    kvd = GQA_KV_HEADS * HEAD_DIM
    cosf, sinf = cos_ref[...], sin_ref[...]
    seg, seg_t = seg_ref[...], segt_ref[...]
    for n0 in range(0, d_model, QKV_CHUNK):
        acc = jnp.dot(h, w_ref[:, n0:n0 + QKV_CHUNK], preferred_element_type=F32)
        acc = _rms_rope(acc, qg_ref[...], cosf, sinf, seg, seg_t) * Q_SCALE
        o_ref[:, n0:n0 + QKV_CHUNK] = acc.astype(BF16)
    kv = jnp.dot(h, w_ref[:, d_model:d_model + 2 * kvd], preferred_element_type=F32)
    k = _rms_rope(kv[:, :kvd], kg_ref[:, :kvd], cosf[:, :kvd], sinf[:, :kvd], seg[:kvd], seg_t[:, :kvd])
    wide = GQA_KV_HEADS * HEAD_PAIR
    o_ref[:, d_model:d_model + wide] = _spread_heads(k, None).astype(BF16)
    o_ref[:, d_model + wide:d_model + 2 * wide] = _spread_heads(kv[:, kvd:], 1.0).astype(BF16)


def _gqa_kernel(bound_ref, q_ref, k_ref, v_ref, o_ref, *, use_bound):
    k2 = k_ref[0]
    v2 = v_ref[0]
    tq = q_ref.shape[1]
    lane = lax.broadcasted_iota(jnp.int32, (tq, HEAD_PAIR), 1)
    first = lane < HEAD_DIM
    for pair in range(q_ref.shape[2] // HEAD_PAIR):
        cols = slice(pair * HEAD_PAIR, (pair + 1) * HEAD_PAIR)
        q2 = q_ref[0, :, cols]
        zero = jnp.zeros_like(q2)
        qs = jnp.concatenate([jnp.where(first, q2, zero), jnp.where(first, zero, q2)], axis=0)
        s = lax.dot_general(qs, k2, (((1,), (1,)), ((), ())), preferred_element_type=F32)
        m = bound_ref[0] if use_bound else jnp.max(s, axis=-1, keepdims=True)
        p = jnp.exp2(s - m).astype(BF16)
        o = jnp.dot(p, v2, preferred_element_type=F32)
        o_a, o_b = o[:tq], o[tq:]
        out = jnp.where(first, o_a / pltpu.roll(o_a, HEAD_DIM, 1), pltpu.roll(o_b, HEAD_DIM, 1) / o_b)
        o_ref[0, :, cols] = out.astype(BF16)


def _gqa_logit_bound(q_gain, k_gain):
    return (GQA_BOUND_MARGIN * LOG2_E * (HEAD_DIM ** 0.5)
            * jnp.max(jnp.abs(q_gain)) * jnp.max(jnp.abs(k_gain))).astype(F32).reshape(1)


def _gqa_attention(qkv, d_model, q_gain, k_gain):
    bound = _gqa_logit_bound(q_gain, k_gain)
    return lax.cond(bound[0] <= GQA_BOUND_MAX,
                    functools.partial(_gqa_attention_call, d_model=d_model, use_bound=True),
                    functools.partial(_gqa_attention_call, d_model=d_model, use_bound=False),
                    bound, qkv)


def _gqa_attention_call(bound, qkv, *, d_model, use_bound):
    b, s, n = qkv.shape
    group_cols = d_model // GQA_KV_HEADS
    k_block0 = d_model // HEAD_PAIR
    v_block0 = k_block0 + GQA_KV_HEADS
    return pl.pallas_call(
        functools.partial(_gqa_kernel, use_bound=use_bound),
        grid=(b, GQA_KV_HEADS, s // GQA_TQ),
        in_specs=[
            pl.BlockSpec(memory_space=pltpu.SMEM),
            pl.BlockSpec((1, GQA_TQ, group_cols), lambda i, g, t: (i, t, g)),
            pl.BlockSpec((1, s, HEAD_PAIR), lambda i, g, t: (i, 0, k_block0 + g)),
            pl.BlockSpec((1, s, HEAD_PAIR), lambda i, g, t: (i, 0, v_block0 + g)),
        ],
        out_specs=pl.BlockSpec((1, GQA_TQ, group_cols), lambda i, g, t: (i, t, g)),
        out_shape=jax.ShapeDtypeStruct((b, s, d_model), BF16),
        compiler_params=_cparams(("arbitrary", "arbitrary", "arbitrary")),
        name="gqa_attention_bounded" if use_bound else "gqa_attention",
    )(bound, qkv, qkv, qkv)


def _store_token_tiles(ref, val, first_token=0):
    t, d = val.shape
    width = ref.shape[-1]
    sub = d // width
    for s in range(sub):
        ref[pl.ds(first_token * sub + s, t, stride=sub), :] = val[:, s * width:(s + 1) * width]


def _load_token_tiles(ref, t, dtype):
    sub = ref.shape[0] // t
    return jnp.concatenate([ref[pl.ds(s, t, stride=sub), :].astype(dtype) for s in range(sub)], axis=1)


def _tile_copy(src_hbm, dst_vmem, src_tok, dst_tok, sub, sem):
    return pltpu.make_async_copy(src_hbm.at[pl.ds(pl.multiple_of(src_tok * sub, sub), sub)],
                                 dst_vmem.at[pl.ds(pl.multiple_of(dst_tok * sub, sub), sub)], sem)


def _layer_norm_rows(z, g, b):
    mu = jnp.mean(z, axis=-1, keepdims=True)
    zc = z - mu
    var = jnp.mean(zc * zc, axis=-1, keepdims=True)
    return zc * lax.rsqrt(var + LN_EPS) * g + b


def _router(lg):
    lane = lax.broadcasted_iota(jnp.int32, lg.shape, 1)
    lanef = lane.astype(F32)
    far = float(LANES)
    gmask = lane < N_GROUPS
    lgm = jnp.where(gmask, lg, NEG_BIG)
    mg = jnp.max(lgm, axis=-1, keepdims=True)
    sg = jnp.sum(jnp.exp(lgm - mg), axis=-1, keepdims=True)
    g_prob = 1.0 / sg
    g_idx = jnp.min(jnp.where(lgm == mg, lanef, far), axis=-1, keepdims=True)
    e_lo = N_GROUPS + g_idx * EXPERTS_PER_GROUP
    emask = (lanef >= e_lo) & (lanef < e_lo + EXPERTS_PER_GROUP)
    lem = jnp.where(emask, lg, NEG_BIG)
    me = jnp.max(lem, axis=-1, keepdims=True)
    ee = jnp.exp(lem - me)
    pe = jnp.where(emask, ee / jnp.sum(ee, axis=-1, keepdims=True), -1.0)
    p1 = jnp.max(pe, axis=-1, keepdims=True)
    i1 = jnp.min(jnp.where(pe == p1, lanef, far), axis=-1, keepdims=True)
    pe2 = jnp.where(lanef == i1, -1.0, pe)
    p2 = jnp.max(pe2, axis=-1, keepdims=True)
    i2 = jnp.min(jnp.where(pe2 == p2, lanef, far), axis=-1, keepdims=True)
    den = p1 + p2
    gate1 = g_prob * (p1 / den)
    gate2 = g_prob * (p2 / den)
    out = jnp.where(lane == 0, gate1, 0.0)
    out = jnp.where(lane == 1, gate2, out)
    out = jnp.where(lane == 2, i1 - N_GROUPS, out)
    out = jnp.where(lane == 3, i2 - N_GROUPS, out)
    return out, lanef == i1, lanef == i2


def _post_attn_kernel(a_ref, wo_ref, x_ref, g1_ref, sc2_ref, sh2_ref, lng_ref, lnb_ref, wr_ref, br_ref,
                      before_ref, xo_ref, h2_ref, route_ref, cnt_ref, run_cnt, wo_bf16, *, alpha):
    @pl.when((pl.program_id(0) == 0) & (pl.program_id(1) == 0))
    def _():
        run_cnt[...] = jnp.zeros_like(run_cnt)
        wo_bf16[...] = wo_ref[0].astype(BF16)

    y = jnp.dot(a_ref[0], wo_bf16[...], preferred_element_type=F32)
    z = alpha * x_ref[0] + (1.0 + g1_ref[0]) * y
    xn = _layer_norm_rows(z, lng_ref[...], lnb_ref[...])
    xo_ref[0] = xn
    h2 = xn * (1.0 + sc2_ref[0]) + sh2_ref[0]
    _store_token_tiles(h2_ref, h2)
    h_hi, h_lo = _split_bf16(h2)
    t = jnp.dot(h_hi, wr_ref[...], preferred_element_type=F32)
    lg = (t[:, :LANES] + t[:, LANES:] + jnp.dot(h_lo, wr_ref[:, :LANES], preferred_element_type=F32)
          + br_ref[...])
    route, m1, m2 = _router(lg)
    mem = jnp.where(m1 | m2, 1.0, 0.0)
    rank = run_cnt[0:1, :] + jnp.dot(before_ref[...], mem.astype(BF16), preferred_element_type=F32)
    rank1 = jnp.sum(jnp.where(m1, rank, 0.0), axis=-1, keepdims=True)
    rank2 = jnp.sum(jnp.where(m2, rank, 0.0), axis=-1, keepdims=True)
    lane = lax.broadcasted_iota(jnp.int32, route.shape, 1)
    route_ref[0] = jnp.where(lane == 4, rank1, jnp.where(lane == 5, rank2, route))
    run_cnt[...] = run_cnt[...] + jnp.sum(mem, axis=0, keepdims=True)
    cnt_ref[...] = run_cnt[...]


def _post_attn(a, w_o, layer, x, g1, sc2, sh2, ln_g, ln_b, wr, br, before, alpha):
    b, s, d = x.shape
    sub, width = TOKEN_ROWS, d // TOKEN_ROWS
    tiles_per_batch = s // ROW_TILE
    const = lambda i, j: (0, 0)
    tile = lambda i, j: (i, j, 0)
    per_b = lambda i, j: (i, 0, 0)
    return pl.pallas_call(
        functools.partial(_post_attn_kernel, alpha=alpha),
        grid=(b, s // ROW_TILE),
        in_specs=[
            pl.BlockSpec((1, ROW_TILE, d), tile),
            pl.BlockSpec((1, d, d), lambda i, j: (layer, 0, 0), pipeline_mode=pl.Buffered(1)),
            pl.BlockSpec((1, ROW_TILE, d), tile),
            pl.BlockSpec((1, 1, d), per_b),
            pl.BlockSpec((1, 1, d), per_b),
            pl.BlockSpec((1, 1, d), per_b),
            pl.BlockSpec((1, d), const),
            pl.BlockSpec((1, d), const),
            pl.BlockSpec((d, 2 * LANES), const),
            pl.BlockSpec((1, LANES), const),
            pl.BlockSpec((ROW_TILE, ROW_TILE), const),
        ],
        out_specs=[
            pl.BlockSpec((1, ROW_TILE, d), tile),
            pl.BlockSpec((ROW_TILE * sub, width), lambda i, j: (i * tiles_per_batch + j, 0)),
            pl.BlockSpec((1, ROW_TILE, LANES), tile),
            pl.BlockSpec((8, LANES), const),
        ],
        out_shape=[
            jax.ShapeDtypeStruct((b, s, d), F32),
            jax.ShapeDtypeStruct((b * s * sub, width), F32),
            jax.ShapeDtypeStruct((b, s, LANES), F32),
            jax.ShapeDtypeStruct((8, LANES), F32),
        ],
        scratch_shapes=[pltpu.VMEM((8, LANES), F32), pltpu.VMEM((d, d), BF16)],
        compiler_params=_cparams(("arbitrary", "arbitrary")),
        name="post_attn",
    )(a, w_o, x, g1, sc2, sh2, ln_g, ln_b, wr, br, before)


def _dispatch_kernel(pos_ref, h_ref, xs_hbm, sem):
    tm = DISPATCH_TM
    sub = h_ref.shape[0] // tm

    def issue(jb, carry):
        for u in range(DMA_UNROLL):
            j = jb * DMA_UNROLL + u
            src = h_ref.at[pl.ds(pl.multiple_of(j * sub, sub), sub)]
            for k in range(TOP_K):
                row = pos_ref[0, 0, TOP_K * j + k]
                dst = xs_hbm.at[pl.ds(pl.multiple_of(row * sub, sub), sub)]
                pltpu.make_async_copy(src, dst, sem).start(priority=k % 2)
        return carry

    lax.fori_loop(0, tm // DMA_UNROLL, issue, 0)
    for _ in range(TOP_K):
        pltpu.make_async_copy(h_ref, xs_hbm.at[pl.ds(0, tm * sub)], sem).wait()


def _dispatch(pos, h2_tiles, n_tok):
    sub, width = h2_tiles.shape[0] // n_tok, h2_tiles.shape[1]
    tm = DISPATCH_TM
    n_tiles = n_tok // tm
    return pl.pallas_call(
        _dispatch_kernel,
        grid=(n_tiles,),
        in_specs=[
            pl.BlockSpec((1, 1, TOP_K * tm), lambda i: (i, 0, 0), memory_space=pltpu.SMEM),
            pl.BlockSpec((tm * sub, width), lambda i: (i, 0)),
        ],
        out_specs=pl.BlockSpec(memory_space=pl.ANY),
        out_shape=jax.ShapeDtypeStruct((n_tok * TOP_K * sub, width), F32),
        scratch_shapes=[pltpu.SemaphoreType.DMA],
        compiler_params=_cparams(("arbitrary",)),
        name="moe_dispatch",
    )(pos.reshape(n_tiles, 1, TOP_K * tm), h2_tiles)


def _expert_kernel(blk_ref, exp_ref, lo_ref, hi_ref, x_ref, wg_ref, wu_ref, wd_ref, y_ref,
                   acc, wgb, wub, wdb, cast_expert, state):
    i = pl.program_id(0)
    n_items = pl.num_programs(0)
    tb = MOE_TB
    prev = jnp.maximum(i - 1, 0)
    nxt = jnp.minimum(i + 1, n_items - 1)
    first = (i == 0) | (blk_ref[i] != blk_ref[prev])
    last = (i == n_items - 1) | (blk_ref[i] != blk_ref[nxt])
    lo, hi = lo_ref[i], hi_ref[i]
    whole = (lo == 0) & (hi == tb)

    @pl.when(i == 0)
    def _():
        cast_expert[0] = -1

    @pl.when(first)
    def _():
        state[0] = 0
        state[1] = 0

    @pl.when(hi > lo)
    def _():
        @pl.when(cast_expert[0] != exp_ref[i])
        def _():
            wgb[...] = wg_ref[0, 0].astype(BF16)
            wub[...] = wu_ref[0, 0].astype(BF16)
            wdb[...] = wd_ref[0, 0].astype(BF16)
            cast_expert[0] = exp_ref[i]

        xb = _load_token_tiles(x_ref, tb, BF16)
        g = jnp.dot(xb, wgb[...], preferred_element_type=F32)
        u = jnp.dot(xb, wub[...], preferred_element_type=F32)
        act = (g * jax.nn.sigmoid(g) * u).astype(BF16)
        y = jnp.dot(act, wdb[...], preferred_element_type=F32)

        @pl.when(whole)
        def _():
            _store_token_tiles(y_ref, y)
            state[0] = 1

        @pl.when(jnp.logical_not(whole))
        def _():
            row = lax.broadcasted_iota(jnp.int32, (tb, 1), 0)
            part = jnp.where((row >= lo) & (row < hi), y, 0.0)

            @pl.when(state[1] == 0)
            def _():
                acc[...] = part

            @pl.when(state[1] != 0)
            def _():
                acc[...] += part

            state[1] = 1

    @pl.when(last & (state[0] == 0))
    def _():
        _store_token_tiles(y_ref, acc[...])


def _expert_mlps(items, xs_tiles, w_gate, w_up, w_down, layer):
    item_blk, item_exp, item_lo, item_hi = items
    n_items = item_blk.shape[0]
    _, _, d, de = w_gate.shape
    sub, width = TOKEN_ROWS, d // TOKEN_ROWS
    w_index = lambda i, blk, ex, lo, hi: (layer, ex[i], 0, 0)
    grid_spec = pltpu.PrefetchScalarGridSpec(
        num_scalar_prefetch=4,
        grid=(n_items,),
        in_specs=[
            pl.BlockSpec((MOE_TB * sub, width), lambda i, blk, ex, lo, hi: (blk[i], 0)),
            pl.BlockSpec((1, 1, d, de), w_index),
            pl.BlockSpec((1, 1, d, de), w_index),
            pl.BlockSpec((1, 1, de, d), w_index),
        ],
        out_specs=pl.BlockSpec((MOE_TB * sub, width), lambda i, blk, ex, lo, hi: (blk[i], 0)),
        scratch_shapes=[
            pltpu.VMEM((MOE_TB, d), F32),
            pltpu.VMEM((d, de), BF16),
            pltpu.VMEM((d, de), BF16),
            pltpu.VMEM((de, d), BF16),
            pltpu.SMEM((1,), jnp.int32),
            pltpu.SMEM((2,), jnp.int32),
        ],
    )
    return pl.pallas_call(
        _expert_kernel,
        grid_spec=grid_spec,
        out_shape=jax.ShapeDtypeStruct(xs_tiles.shape, F32),
        compiler_params=_cparams(("arbitrary",)),
        name="expert_mlps",
    )(item_blk, item_exp, item_lo, item_hi, xs_tiles, w_gate, w_up, w_down)


def _combine_kernel(pos_ref, pos_next_ref, ys_hbm, x_ref, route_ref, g2_ref, lng_ref, lnb_ref, xo_ref, buf, sems,
                    *, alpha):
    i = pl.program_id(0)
    n = pl.num_programs(0)
    tm = COMBINE_TM
    sub = buf.shape[2] // tm
    slot = i % 2

    def gather(p_ref, dst_slot):
        def issue(jb, carry):
            for u in range(DMA_UNROLL):
                j = jb * DMA_UNROLL + u
                for k in range(TOP_K):
                    _tile_copy(ys_hbm, buf.at[dst_slot, k], p_ref[0, 0, TOP_K * j + k], j, sub,
                               sems.at[dst_slot]).start(priority=k % 2)
            return carry

        lax.fori_loop(0, tm // DMA_UNROLL, issue, 0)

    @pl.when(i == 0)
    def _():
        gather(pos_ref, slot)

    @pl.when(i + 1 < n)
    def _():
        gather(pos_next_ref, 1 - slot)

    route = route_ref[...]
    lane = lax.broadcasted_iota(jnp.int32, route.shape, 1)
    gates = [jnp.sum(jnp.where(lane == k, route, 0.0), axis=-1, keepdims=True) for k in range(TOP_K)]
    for k in range(TOP_K):
        pltpu.make_async_copy(ys_hbm.at[pl.ds(0, tm * sub)], buf.at[slot, k], sems.at[slot]).wait()
    y = _load_token_tiles(buf.at[slot, 0], tm, F32) * gates[0]
    for k in range(1, TOP_K):
        y = y + _load_token_tiles(buf.at[slot, k], tm, F32) * gates[k]
    z = alpha * x_ref[...] + (1.0 + g2_ref[0]) * y
    xo_ref[...] = _layer_norm_rows(z, lng_ref[...], lnb_ref[...])


def _combine(pos3, ys, x2d, route2d, g2, ln_g, ln_b, seq, alpha):
    n_tok, d = x2d.shape
    tm = COMBINE_TM
    n_tiles = n_tok // tm
    tiles_per_batch = seq // tm
    const = lambda i: (0, 0)
    return pl.pallas_call(
        functools.partial(_combine_kernel, alpha=alpha),
        grid=(n_tiles,),
        in_specs=[
            pl.BlockSpec((1, 1, TOP_K * tm), lambda i: (i, 0, 0), memory_space=pltpu.SMEM),
            pl.BlockSpec((1, 1, TOP_K * tm), lambda i: (jnp.minimum(i + 1, n_tiles - 1), 0, 0),
                         memory_space=pltpu.SMEM),
            pl.BlockSpec(memory_space=pl.ANY),
            pl.BlockSpec((tm, d), lambda i: (i, 0)),
            pl.BlockSpec((tm, LANES), lambda i: (i, 0)),
            pl.BlockSpec((1, 1, d), lambda i: (i // tiles_per_batch, 0, 0)),
            pl.BlockSpec((1, d), const),
            pl.BlockSpec((1, d), const),
        ],
        out_specs=pl.BlockSpec((tm, d), lambda i: (i, 0)),
        out_shape=jax.ShapeDtypeStruct((n_tok, d), F32),
        scratch_shapes=[
            pltpu.VMEM((2, TOP_K, tm * TOKEN_ROWS, d // TOKEN_ROWS), F32),
            pltpu.SemaphoreType.DMA((2,)),
        ],
        compiler_params=_cparams(("arbitrary",)),
        name="moe_combine",
    )(pos3, pos3, ys, x2d, route2d, g2, ln_g, ln_b)


def _combine_project_kernel(pos_ref, pos_next_ref, ys_hbm, x_ref, route_ref, g2_ref, lng_ref, lnb_ref,
                            sc_ref, sh_ref, w_ref, *rest, alpha, d_model, gqa):
    xo_ref, qkv_ref, buf, sems = rest[-4:]
    i = pl.program_id(0)
    n = pl.num_programs(0)
    tm = ROW_TILE
    sub = buf.shape[2] // tm
    slot = i % 2

    def tile_gather(p_ref, dst_slot, j, k):
        return _tile_copy(ys_hbm, buf.at[dst_slot, k], p_ref[0, 0, TOP_K * j + k], j, sub, sems.at[dst_slot])

    @pl.when(i == 0)
    def _():
        def issue(jb, carry):
            for u in range(DMA_UNROLL):
                for k in range(TOP_K):
                    tile_gather(pos_ref, slot, jb * DMA_UNROLL + u, k).start(priority=k % 2)
            return carry

        lax.fori_loop(0, tm // DMA_UNROLL, issue, 0)

    route = route_ref[...]
    lane = lax.broadcasted_iota(jnp.int32, route.shape, 1)
    gates = [jnp.sum(jnp.where(lane == k, route, 0.0), axis=-1, keepdims=True) for k in range(TOP_K)]
    for k in range(TOP_K):
        pltpu.make_async_copy(ys_hbm.at[pl.ds(0, tm * sub)], buf.at[slot, k], sems.at[slot]).wait()
    y = _load_token_tiles(buf.at[slot, 0], tm, F32) * gates[0]
    for k in range(1, TOP_K):
        y = y + _load_token_tiles(buf.at[slot, k], tm, F32) * gates[k]
    z = alpha * x_ref[...] + (1.0 + g2_ref[0]) * y
    xn = _layer_norm_rows(z, lng_ref[...], lnb_ref[...])
    xo_ref[...] = xn
    h = (xn * (1.0 + sc_ref[0]) + sh_ref[0]).astype(BF16)

    for j in range(tm):
        for k in range(TOP_K):
            tile_gather(pos_next_ref, 1 - slot, j, k).start(priority=k % 2)

    if gqa:
        _project_gqa(h, w_ref, *rest[:6], qkv_ref, d_model)
    else:
        _project_na(h, w_ref, qkv_ref, d_model)

    @pl.when(i == n - 1)
    def _():
        for k in range(TOP_K):
            pltpu.make_async_copy(ys_hbm.at[pl.ds(0, tm * sub)], buf.at[1 - slot, k], sems.at[1 - slot]).wait()


def _combine_project(pos3, ys, x2d, route2d, g2, ln_g, ln_b, sc, sh, w_bf16, gqa_extras, seq, alpha):
    n_tok, d = x2d.shape
    n = w_bf16.shape[1]
    n_out = _gqa_out_cols(d) if gqa_extras else n
    tm = ROW_TILE
    n_tiles = n_tok // tm
    tiles_per_batch = seq // tm
    const = lambda i: (0, 0)
    per_b = lambda i: (i // tiles_per_batch, 0, 0)
    extra_specs = []
    if gqa_extras:
        extra_specs = [
            pl.BlockSpec((1, QKV_CHUNK), const),
            pl.BlockSpec((1, QKV_CHUNK), const),
            pl.BlockSpec((tm, QKV_CHUNK), lambda i: (i % tiles_per_batch, 0)),
            pl.BlockSpec((tm, QKV_CHUNK), lambda i: (i % tiles_per_batch, 0)),
            pl.BlockSpec((QKV_CHUNK, LANES), const),
            pl.BlockSpec((LANES, QKV_CHUNK), const),
        ]
    return pl.pallas_call(
        functools.partial(_combine_project_kernel, alpha=alpha, d_model=d, gqa=bool(gqa_extras)),
        grid=(n_tiles,),
        in_specs=[
            pl.BlockSpec((1, 1, TOP_K * tm), lambda i: (i, 0, 0), memory_space=pltpu.SMEM),
            pl.BlockSpec((1, 1, TOP_K * tm), lambda i: (jnp.minimum(i + 1, n_tiles - 1), 0, 0),
                         memory_space=pltpu.SMEM),
            pl.BlockSpec(memory_space=pl.ANY),
            pl.BlockSpec((tm, d), lambda i: (i, 0)),
            pl.BlockSpec((tm, LANES), lambda i: (i, 0)),
            pl.BlockSpec((1, 1, d), per_b),
            pl.BlockSpec((1, d), const),
            pl.BlockSpec((1, d), const),
            pl.BlockSpec((1, 1, d), per_b),
            pl.BlockSpec((1, 1, d), per_b),
            pl.BlockSpec((d, n), const),
        ] + extra_specs,
        out_specs=[
            pl.BlockSpec((tm, d), lambda i: (i, 0)),
            pl.BlockSpec((tm, n_out), lambda i: (i, 0)),
        ],
        out_shape=[
            jax.ShapeDtypeStruct((n_tok, d), F32),
            jax.ShapeDtypeStruct((n_tok, n_out), BF16),
        ],
        scratch_shapes=[
            pltpu.VMEM((2, TOP_K, tm * TOKEN_ROWS, d // TOKEN_ROWS), F32),
            pltpu.SemaphoreType.DMA((2,)),
        ],
        compiler_params=_cparams(("arbitrary",)),
        name="combine_qkv_gqa" if gqa_extras else "combine_qkv_na",
    )(pos3, pos3, ys, x2d, route2d, g2, ln_g, ln_b, sc, sh, w_bf16, *gqa_extras)


def _routing_tables(route2d, cnt):
    n_tok = route2d.shape[0]
    n_rows = n_tok * TOP_K
    counts = cnt[0, N_GROUPS:N_GROUPS + N_EXPERTS].astype(jnp.int32)
    seg_end = jnp.cumsum(counts)
    experts = route2d[:, 2:2 + TOP_K].astype(jnp.int32)
    ranks = route2d[:, 4:4 + TOP_K].astype(jnp.int32)
    is_e = experts[:, :, None] == jnp.arange(N_EXPERTS, dtype=jnp.int32)[None, None, :]
    pos = (ranks + jnp.sum(jnp.where(is_e, (seg_end - counts)[None, None, :], 0), axis=-1)).reshape(n_rows)
    cuts = jnp.concatenate([jnp.arange(n_rows // MOE_TB, dtype=jnp.int32) * MOE_TB, seg_end - counts])
    n_items = cuts.shape[0]
    idx = jnp.arange(n_items, dtype=jnp.int32)
    rank = (jnp.sum(cuts[None, :] < cuts[:, None], axis=1)
            + jnp.sum((cuts[None, :] == cuts[:, None]) & (idx[None, :] < idx[:, None]), axis=1))
    lo_abs = jnp.sum(jnp.where(rank[None, :] == idx[:, None], cuts[None, :], 0), axis=1)
    hi_abs = jnp.concatenate([lo_abs[1:], jnp.full((1,), n_rows, jnp.int32)])
    item_blk = jnp.minimum(lo_abs // MOE_TB, n_rows // MOE_TB - 1)
    item_exp = jnp.minimum(jnp.sum(seg_end[None, :] <= lo_abs[:, None], axis=1), N_EXPERTS - 1)
    item_lo = lo_abs - item_blk * MOE_TB
    item_hi = hi_abs - item_blk * MOE_TB
    items = tuple(a.astype(jnp.int32) for a in (item_blk, item_exp, item_lo, item_hi))
    return pos, items


def _rope_tables(seq, width):
    t = jnp.arange(seq, dtype=jnp.int32)
    pos = jnp.stack([t // GRID_W, t % GRID_W], axis=-1).astype(F32)
    inv_freq = ROPE_THETA ** (-jnp.arange(ROPE_PAIRS, dtype=F32) / ROPE_PAIRS)
    ang = pos[:, :, None] * inv_freq
    cos, sin = jnp.cos(ang), jnp.sin(ang)
    cosf = jnp.stack([cos, cos], axis=2).reshape(seq, HEAD_DIM)
    sinf = jnp.stack([-sin, sin], axis=2).reshape(seq, HEAD_DIM)
    reps = width // HEAD_DIM
    return jnp.tile(cosf, (1, reps)), jnp.tile(sinf, (1, reps))


def kernel(x, c, ada_w, ada_b, ln_g, ln_b, na_w_qkv, na_rpb, na_w_o, gqa_w_qkv, gqa_q_norm, gqa_k_norm, gqa_w_o,
           moe_w_group, moe_b_group, moe_w_expert, moe_b_expert, moe_w_gate, moe_w_up, moe_w_down):
    b, s, d = x.shape
    depth = ada_w.shape[0]
    n_tok = b * s
    rows = s // GRID_W
    alpha = (2 * depth) ** 0.25

    mod = _ada_modulation(c, ada_w, ada_b).reshape(depth, b, 6, 1, d)

    cosf, sinf = _rope_tables(s, QKV_CHUNK)
    head_of_lane = jnp.arange(QKV_CHUNK, dtype=jnp.int32) // HEAD_DIM
    seg_t = (head_of_lane[None, :] == jnp.arange(LANES, dtype=jnp.int32)[:, None])
    seg = (seg_t.T.astype(F32) / HEAD_DIM).astype(BF16)
    seg_t = seg_t.astype(BF16)
    tok = jnp.arange(ROW_TILE, dtype=jnp.int32)
    before = (tok[None, :] < tok[:, None]).astype(BF16)

    def qkv_params(i):
        j = i // 2
        if i % 2 == 0:
            return na_w_qkv[j].astype(BF16), ()
        reps = QKV_CHUNK // HEAD_DIM
        return gqa_w_qkv[j].astype(BF16), (jnp.tile(gqa_q_norm[j], reps)[None, :],
                                           jnp.tile(gqa_k_norm[j], reps)[None, :], cosf, sinf, seg, seg_t)

    qkv = None
    for i in range(depth):
        sh1, sc1, g1, sh2, sc2, g2 = (mod[i, :, k] for k in range(6))
        j = i // 2
        if qkv is None:
            w_bf16, extras = qkv_params(i)
            assert not extras, "the first layer is a neighbourhood-attention layer"
            qkv = _qkv_na(x, sc1, sh1, w_bf16)
        if i % 2 == 0:
            attn = _na_attention(qkv, _na_bias_table(na_rpb[j]), rows)
            w_o = na_w_o
        else:
            attn = _gqa_attention(qkv, d, gqa_q_norm[j], gqa_k_norm[j])
            w_o = gqa_w_o

        w_r = jnp.concatenate([moe_w_group[i], moe_w_expert[i]], axis=1)
        w_r = jnp.pad(w_r, ((0, 0), (0, LANES - w_r.shape[1])))
        wr_hi, wr_lo = _split_bf16(w_r)
        b_r = jnp.concatenate([moe_b_group[i], moe_b_expert[i]])
        b_r = jnp.pad(b_r, (0, LANES - b_r.shape[0]))[None, :]
        x, h2, route, cnt = _post_attn(attn, w_o, j, x, g1, sc2, sh2, ln_g[i, 0][None, :],
                                       ln_b[i, 0][None, :], jnp.concatenate([wr_hi, wr_lo], axis=1), b_r, before,
                                       alpha)

        route2d = route.reshape(n_tok, LANES)
        pos, items = _routing_tables(route2d, cnt)
        pos3 = pos.reshape(n_tok // POS_TILE, 1, TOP_K * POS_TILE)
        xs = _dispatch(pos, h2, n_tok)
        ys = _expert_mlps(items, xs, moe_w_gate, moe_w_up, moe_w_down, i)
        lng, lnb = ln_g[i, 1][None, :], ln_b[i, 1][None, :]
        if i + 1 < depth:
            w_bf16, extras = qkv_params(i + 1)
            x2d, qkv2d = _combine_project(pos3, ys, x.reshape(n_tok, d), route2d, g2, lng, lnb,
                                          mod[i + 1, :, 1], mod[i + 1, :, 0], w_bf16, extras, s, alpha)
            x, qkv = x2d.reshape(b, s, d), qkv2d.reshape(b, s, -1)
        else:
            x = _combine(pos3, ys, x.reshape(n_tok, d), route2d, g2, lng, lnb, s, alpha).reshape(b, s, d)
    return x
```

```python
import functools

import jax
import jax.numpy as jnp
from jax import lax
from jax.experimental import pallas as pl
from jax.experimental.pallas import tpu as pltpu

GRID_W = 64
N_HEADS = 16
HEAD_DIM = 64
NA_ROWS = 8
NA_COLS = 16
GQA_KV_HEADS = 4
ROPE_THETA = 10000.0
ROPE_PAIRS = HEAD_DIM // 4
N_GROUPS = 4
EXPERTS_PER_GROUP = 8
N_EXPERTS = N_GROUPS * EXPERTS_PER_GROUP
TOP_K = 2
LN_EPS = 1e-5
RMS_EPS = 1e-6

LANES = 128
TOKEN_ROWS = 8
HEAD_PAIR = 2 * HEAD_DIM
VMEM_LIMIT_BYTES = 52 * 1024 * 1024
ADA_TN = 1536
ROW_TILE = 512
QKV_CHUNK = 512
GQA_TQ = 2048
NA_ROWS_PER_STEP = 8
NA_STAGE_PAIRS = 4
MOE_TB = 256
POS_TILE = ROW_TILE
DISPATCH_TM = 2048
COMBINE_TM = POS_TILE
DMA_UNROLL = 8
NEG_BIG = -1e30
LOG2_E = 1.4426950408889634
Q_SCALE = (HEAD_DIM ** -0.5) * LOG2_E
GQA_BOUND_MARGIN = 1.02
GQA_BOUND_MAX = 50.0

F32 = jnp.float32
BF16 = jnp.bfloat16


def _cparams(sem):
    return pltpu.CompilerParams(dimension_semantics=sem, vmem_limit_bytes=VMEM_LIMIT_BYTES)


def _split_bf16(a):
    hi = a.astype(BF16)
    lo = (a - hi.astype(F32)).astype(BF16)
    return hi, lo


def _ada_kernel(c_ref, w_ref, b_ref, o_ref):
    c = c_ref[...]
    ca = (c * jax.nn.sigmoid(c)).astype(BF16)
    o_ref[0] = jnp.dot(ca, w_ref[0].astype(BF16), preferred_element_type=F32) + b_ref[0]


def _ada_modulation(c, ada_w, ada_b):
    depth, d, n = ada_w.shape
    b = c.shape[0]
    return pl.pallas_call(
        _ada_kernel,
        grid=(depth, n // ADA_TN),
        in_specs=[
            pl.BlockSpec((b, d), lambda i, j: (0, 0)),
            pl.BlockSpec((1, d, ADA_TN), lambda i, j: (i, 0, j)),
            pl.BlockSpec((1, 1, ADA_TN), lambda i, j: (i, 0, j)),
        ],
        out_specs=pl.BlockSpec((1, b, ADA_TN), lambda i, j: (i, 0, j)),
        out_shape=jax.ShapeDtypeStruct((depth, b, n), F32),
        compiler_params=_cparams(("arbitrary", "arbitrary")),
        name="ada_modulation",
    )(c, ada_w, ada_b.reshape(depth, 1, n))


def _project_na(h, w_ref, o_ref, d_model):
    n = w_ref.shape[1]
    for n0 in range(0, n, QKV_CHUNK):
        acc = jnp.dot(h, w_ref[:, n0:n0 + QKV_CHUNK], preferred_element_type=F32)
        if n0 < d_model:
            acc = acc * Q_SCALE
        o_ref[:, n0:n0 + QKV_CHUNK] = acc.astype(BF16)


def _qkv_na_kernel(x_ref, sc_ref, sh_ref, w_ref, o_ref, *, d_model):
    h = (x_ref[0] * (1.0 + sc_ref[0]) + sh_ref[0]).astype(BF16)
    _project_na(h, w_ref, o_ref.at[0], d_model)


def _qkv_na(x, sc, sh, w_bf16):
    b, s, d = x.shape
    n = w_bf16.shape[1]
    return pl.pallas_call(
        functools.partial(_qkv_na_kernel, d_model=d),
        grid=(b, s // ROW_TILE),
        in_specs=[
            pl.BlockSpec((1, ROW_TILE, d), lambda i, j: (i, j, 0)),
            pl.BlockSpec((1, 1, d), lambda i, j: (i, 0, 0)),
            pl.BlockSpec((1, 1, d), lambda i, j: (i, 0, 0)),
            pl.BlockSpec((d, n), lambda i, j: (0, 0)),
        ],
        out_specs=pl.BlockSpec((1, ROW_TILE, n), lambda i, j: (i, j, 0)),
        out_shape=jax.ShapeDtypeStruct((b, s, n), BF16),
        compiler_params=_cparams(("arbitrary", "arbitrary")),
        name="qkv_na",
    )(x, sc, sh, w_bf16)


N_DR = 2 * NA_ROWS - 1
N_DC = 2 * NA_COLS - 1


def _na_bias_kernel(r_ref, m_ref, valid_ref, o_ref):
    r = r_ref[...]
    hi = r.astype(BF16)
    r1 = r - hi.astype(F32)
    mid = r1.astype(BF16)
    lo = (r1 - mid.astype(F32)).astype(BF16)
    m = m_ref[...]
    acc = (jnp.dot(hi, m, preferred_element_type=F32) + jnp.dot(mid, m, preferred_element_type=F32)
           + jnp.dot(lo, m, preferred_element_type=F32))
    o_ref[...] = jnp.where(valid_ref[...] > 0.0, acc * LOG2_E, NEG_BIG)


def _na_bias_table(rpb):
    qc = jnp.arange(GRID_W, dtype=jnp.int32)
    kc = jnp.arange(GRID_W, dtype=jnp.int32)
    win_start = jnp.clip(qc - NA_COLS // 2, 0, GRID_W - NA_COLS)
    valid = (kc[None, :] >= win_start[:, None]) & (kc[None, :] < win_start[:, None] + NA_COLS)
    dc = jnp.clip(kc[None, :] - qc[:, None] + NA_COLS - 1, 0, N_DC - 1)
    onehot = (jnp.arange(LANES, dtype=jnp.int32)[:, None] == dc.reshape(1, GRID_W * GRID_W)).astype(BF16)
    r2d = jnp.pad(rpb.reshape(N_HEADS * N_DR, N_DC).astype(F32), ((0, 0), (0, LANES - N_DC)))
    tiles = pl.pallas_call(
        _na_bias_kernel,
        out_shape=jax.ShapeDtypeStruct((N_HEADS * N_DR, GRID_W * GRID_W), F32),
        compiler_params=pltpu.CompilerParams(vmem_limit_bytes=VMEM_LIMIT_BYTES),
        name="na_bias_tiles",
    )(r2d, onehot, valid.reshape(1, GRID_W * GRID_W).astype(F32))
    tiles = tiles.reshape(N_HEADS, N_DR, GRID_W, GRID_W)
    pairs = jnp.concatenate([tiles[:, :N_DR - 1], tiles[:, 1:]], axis=-1)
    pairs = pairs.reshape(N_HEADS // 2, 2, N_DR - 1, GRID_W, HEAD_PAIR).transpose(0, 2, 1, 3, 4)
    return pairs.reshape(N_HEADS // 2, N_DR - 1, 2 * GRID_W, HEAD_PAIR)


def _na_kernel(q_ref, k_ref, v_ref, bias_ref, o_ref, *, rows):
    nk = NA_ROWS * GRID_W
    lane = lax.broadcasted_iota(jnp.int32, (GRID_W, HEAD_PAIR), 1)
    first = lane < HEAD_DIM
    for sub_row in range(NA_ROWS_PER_STEP):
        r = pl.program_id(1) * NA_ROWS_PER_STEP + sub_row
        row_start = jnp.clip(r - NA_ROWS // 2, 0, rows - NA_ROWS)
        start = pl.multiple_of(row_start * GRID_W, GRID_W)
        dr_base = row_start - r + (NA_ROWS - 1)
        q_rows = slice(sub_row * GRID_W, (sub_row + 1) * GRID_W)
        for hp0 in range(0, N_HEADS // 2, NA_STAGE_PAIRS):
            pairs = range(hp0, hp0 + NA_STAGE_PAIRS)
            cols = {hp: slice(hp * HEAD_PAIR, (hp + 1) * HEAD_PAIR) for hp in pairs}
            s_all = {}
            for hp in pairs:
                q2 = q_ref[0, q_rows, cols[hp]]
                k2 = k_ref[0, pl.ds(start, nk), cols[hp]]
                zero = jnp.zeros_like(q2)
                qs = jnp.concatenate([jnp.where(first, q2, zero), jnp.where(first, zero, q2)], axis=0)
                s = lax.dot_general(qs, k2, (((1,), (1,)), ((), ())), preferred_element_type=F32)
                s_all[hp] = s + jnp.concatenate([bias_ref[hp, dr_base + 2 * m] for m in range(NA_ROWS // 2)],
                                                axis=1)
            m_all = {hp: jnp.max(s_all[hp], axis=-1, keepdims=True) for hp in pairs}
            p_all = {hp: jnp.exp2(s_all[hp] - m_all[hp]) for hp in pairs}
            l_all = {hp: jnp.sum(p_all[hp], axis=-1, keepdims=True) for hp in pairs}
            o_all = {hp: jnp.dot(p_all[hp].astype(BF16), v_ref[0, pl.ds(start, nk), cols[hp]],
                                 preferred_element_type=F32) for hp in pairs}
            for hp in pairs:
                o = o_all[hp] / l_all[hp]
                o_ref[0, q_rows, cols[hp]] = jnp.where(first, o[:GRID_W], o[GRID_W:]).astype(BF16)


def _na_attention(qkv, bias, rows):
    b, s, n3 = qkv.shape
    d = n3 // 3
    assert rows >= NA_ROWS and NA_ROWS % 2 == 0 and rows % NA_ROWS_PER_STEP == 0
    q_tile = NA_ROWS_PER_STEP * GRID_W
    return pl.pallas_call(
        functools.partial(_na_kernel, rows=rows),
        grid=(b, rows // NA_ROWS_PER_STEP),
        in_specs=[
            pl.BlockSpec((1, q_tile, d), lambda i, r: (i, r, 0)),
            pl.BlockSpec((1, s, d), lambda i, r: (i, 0, 1)),
            pl.BlockSpec((1, s, d), lambda i, r: (i, 0, 2)),
            pl.BlockSpec(bias.shape, lambda i, r: (0, 0, 0, 0)),
        ],
        out_specs=pl.BlockSpec((1, q_tile, d), lambda i, r: (i, r, 0)),
        out_shape=jax.ShapeDtypeStruct((b, s, d), BF16),
        compiler_params=_cparams(("arbitrary", "arbitrary")),
        name="na_attention",
    )(qkv, qkv, qkv, bias)


def _rms_rope(a, gain, cosf, sinf, seg, seg_t):
    qq_hi, qq_lo = _split_bf16(a * a)
    ms = jnp.dot(qq_hi, seg, preferred_element_type=F32) + jnp.dot(qq_lo, seg, preferred_element_type=F32)
    r_hi, r_lo = _split_bf16(lax.rsqrt(ms + RMS_EPS))
    rf = jnp.dot(r_hi, seg_t, preferred_element_type=F32) + jnp.dot(r_lo, seg_t, preferred_element_type=F32)
    an = a * rf * gain
    n = a.shape[1]
    lane = lax.broadcasted_iota(jnp.int32, a.shape, 1)
    partner = jnp.where(lane % (2 * ROPE_PAIRS) < ROPE_PAIRS,
                        pltpu.roll(an, n - ROPE_PAIRS, 1), pltpu.roll(an, ROPE_PAIRS, 1))
    return an * cosf + partner * sinf


def _spread_heads(a, fill):
    lane = lax.broadcasted_iota(jnp.int32, (a.shape[0], HEAD_PAIR), 1)
    low = lane < HEAD_DIM
    out = []
    for g in range(a.shape[1] // HEAD_DIM):
        blk = a[:, (g // 2) * HEAD_PAIR:(g // 2 + 1) * HEAD_PAIR]
        swapped = pltpu.roll(blk, HEAD_DIM, 1)
        head_low, head_high = (blk, swapped) if g % 2 == 0 else (swapped, blk)
        out.append(jnp.where(low, head_low, head_high if fill is None else fill))
    return jnp.concatenate(out, axis=1)


def _gqa_out_cols(d_model):
    return d_model + 2 * GQA_KV_HEADS * HEAD_PAIR


def _project_gqa(h, w_ref, qg_ref, kg_ref, cos_ref, sin_ref, seg_ref, segt_ref, o_ref, d_model):
    kvd = GQA_KV_HEADS * HEAD_DIM
    cosf, sinf = cos_ref[...], sin_ref[...]
    seg, seg_t = seg_ref[...], segt_ref[...]
    for n0 in range(0, d_model, QKV_CHUNK):
        acc = jnp.dot(h, w_ref[:, n0:n0 + QKV_CHUNK], preferred_element_type=F32)
        acc = _rms_rope(acc, qg_ref[...], cosf, sinf, seg, seg_t) * Q_SCALE
        o_ref[:, n0:n0 + QKV_CHUNK] = acc.astype(BF16)
    kv = jnp.dot(h, w_ref[:, d_model:d_model + 2 * kvd], preferred_element_type=F32)
    k = _rms_rope(kv[:, :kvd], kg_ref[:, :kvd], cosf[:, :kvd], sinf[:, :kvd], seg[:kvd], seg_t[:, :kvd])
    wide = GQA_KV_HEADS * HEAD_PAIR
    o_ref[:, d_model:d_model + wide] = _spread_heads(k, None).astype(BF16)
    o_ref[:, d_model + wide:d_model + 2 * wide] = _spread_heads(kv[:, kvd:], 1.0).astype(BF16)


def _gqa_kernel(bound_ref, q_ref, k_ref, v_ref, o_ref, *, use_bound):
    k2 = k_ref[0]
    v2 = v_ref[0]
    tq = q_ref.shape[1]
    lane = lax.broadcasted_iota(jnp.int32, (tq, HEAD_PAIR), 1)
    first = lane < HEAD_DIM
    for pair in range(q_ref.shape[2] // HEAD_PAIR):
        cols = slice(pair * HEAD_PAIR, (pair + 1) * HEAD_PAIR)
        q2 = q_ref[0, :, cols]
        zero = jnp.zeros_like(q2)
        qs = jnp.concatenate([jnp.where(first, q2, zero), jnp.where(first, zero, q2)], axis=0)
        s = lax.dot_general(qs, k2, (((1,), (1,)), ((), ())), preferred_element_type=F32)
        m = bound_ref[0] if use_bound else jnp.max(s, axis=-1, keepdims=True)
        p = jnp.exp2(s - m).astype(BF16)
        o = jnp.dot(p, v2, preferred_element_type=F32)
        o_a, o_b = o[:tq], o[tq:]
        out = jnp.where(first, o_a / pltpu.roll(o_a, HEAD_DIM, 1), pltpu.roll(o_b, HEAD_DIM, 1) / o_b)
        o_ref[0, :, cols] = out.astype(BF16)


def _gqa_logit_bound(q_gain, k_gain):
    return (GQA_BOUND_MARGIN * LOG2_E * (HEAD_DIM ** 0.5)
            * jnp.max(jnp.abs(q_gain)) * jnp.max(jnp.abs(k_gain))).astype(F32).reshape(1)


def _gqa_attention(qkv, d_model, q_gain, k_gain):
    bound = _gqa_logit_bound(q_gain, k_gain)
    return lax.cond(bound[0] <= GQA_BOUND_MAX,
                    functools.partial(_gqa_attention_call, d_model=d_model, use_bound=True),
                    functools.partial(_gqa_attention_call, d_model=d_model, use_bound=False),
                    bound, qkv)


def _gqa_attention_call(bound, qkv, *, d_model, use_bound):
    b, s, n = qkv.shape
    group_cols = d_model // GQA_KV_HEADS
    k_block0 = d_model // HEAD_PAIR
    v_block0 = k_block0 + GQA_KV_HEADS
    return pl.pallas_call(
        functools.partial(_gqa_kernel, use_bound=use_bound),
        grid=(b, GQA_KV_HEADS, s // GQA_TQ),
        in_specs=[
            pl.BlockSpec(memory_space=pltpu.SMEM),
            pl.BlockSpec((1, GQA_TQ, group_cols), lambda i, g, t: (i, t, g)),
            pl.BlockSpec((1, s, HEAD_PAIR), lambda i, g, t: (i, 0, k_block0 + g)),
            pl.BlockSpec((1, s, HEAD_PAIR), lambda i, g, t: (i, 0, v_block0 + g)),
        ],
        out_specs=pl.BlockSpec((1, GQA_TQ, group_cols), lambda i, g, t: (i, t, g)),
        out_shape=jax.ShapeDtypeStruct((b, s, d_model), BF16),
        compiler_params=_cparams(("arbitrary", "arbitrary", "arbitrary")),
        name="gqa_attention_bounded" if use_bound else "gqa_attention",
    )(bound, qkv, qkv, qkv)


def _store_token_tiles(ref, val, first_token=0):
    t, d = val.shape
    width = ref.shape[-1]
    sub = d // width
    for s in range(sub):
        ref[pl.ds(first_token * sub + s, t, stride=sub), :] = val[:, s * width:(s + 1) * width]


def _load_token_tiles(ref, t, dtype):
    sub = ref.shape[0] // t
    return jnp.concatenate([ref[pl.ds(s, t, stride=sub), :].astype(dtype) for s in range(sub)], axis=1)


def _tile_copy(src_hbm, dst_vmem, src_tok, dst_tok, sub, sem):
    return pltpu.make_async_copy(src_hbm.at[pl.ds(pl.multiple_of(src_tok * sub, sub), sub)],
                                 dst_vmem.at[pl.ds(pl.multiple_of(dst_tok * sub, sub), sub)], sem)


def _layer_norm_rows(z, g, b):
    mu = jnp.mean(z, axis=-1, keepdims=True)
    zc = z - mu
    var = jnp.mean(zc * zc, axis=-1, keepdims=True)
    return zc * lax.rsqrt(var + LN_EPS) * g + b


def _router(lg):
    lane = lax.broadcasted_iota(jnp.int32, lg.shape, 1)
    lanef = lane.astype(F32)
    far = float(LANES)
    gmask = lane < N_GROUPS
    lgm = jnp.where(gmask, lg, NEG_BIG)
    mg = jnp.max(lgm, axis=-1, keepdims=True)
    sg = jnp.sum(jnp.exp(lgm - mg), axis=-1, keepdims=True)
    g_prob = 1.0 / sg
    g_idx = jnp.min(jnp.where(lgm == mg, lanef, far), axis=-1, keepdims=True)
    e_lo = N_GROUPS + g_idx * EXPERTS_PER_GROUP
    emask = (lanef >= e_lo) & (lanef < e_lo + EXPERTS_PER_GROUP)
    lem = jnp.where(emask, lg, NEG_BIG)
    me = jnp.max(lem, axis=-1, keepdims=True)
    ee = jnp.exp(lem - me)
    pe = jnp.where(emask, ee / jnp.sum(ee, axis=-1, keepdims=True), -1.0)
    p1 = jnp.max(pe, axis=-1, keepdims=True)
    i1 = jnp.min(jnp.where(pe == p1, lanef, far), axis=-1, keepdims=True)
    pe2 = jnp.where(lanef == i1, -1.0, pe)
    p2 = jnp.max(pe2, axis=-1, keepdims=True)
    i2 = jnp.min(jnp.where(pe2 == p2, lanef, far), axis=-1, keepdims=True)
    den = p1 + p2
    gate1 = g_prob * (p1 / den)
    gate2 = g_prob * (p2 / den)
    out = jnp.where(lane == 0, gate1, 0.0)
    out = jnp.where(lane == 1, gate2, out)
    out = jnp.where(lane == 2, i1 - N_GROUPS, out)
    out = jnp.where(lane == 3, i2 - N_GROUPS, out)
    return out, lanef == i1, lanef == i2


def _post_attn_kernel(a_ref, wo_ref, x_ref, g1_ref, sc2_ref, sh2_ref, lng_ref, lnb_ref, wr_ref, br_ref,
                      before_ref, xo_ref, h2_ref, route_ref, cnt_ref, run_cnt, wo_bf16, *, alpha):
    @pl.when((pl.program_id(0) == 0) & (pl.program_id(1) == 0))
    def _():
        run_cnt[...] = jnp.zeros_like(run_cnt)
        wo_bf16[...] = wo_ref[0].astype(BF16)

    y = jnp.dot(a_ref[0], wo_bf16[...], preferred_element_type=F32)
    z = alpha * x_ref[0] + (1.0 + g1_ref[0]) * y
    xn = _layer_norm_rows(z, lng_ref[...], lnb_ref[...])
    xo_ref[0] = xn
    h2 = xn * (1.0 + sc2_ref[0]) + sh2_ref[0]
    _store_token_tiles(h2_ref, h2)
    h_hi, h_lo = _split_bf16(h2)
    t = jnp.dot(h_hi, wr_ref[...], preferred_element_type=F32)
    lg = (t[:, :LANES] + t[:, LANES:] + jnp.dot(h_lo, wr_ref[:, :LANES], preferred_element_type=F32)
          + br_ref[...])
    route, m1, m2 = _router(lg)
    mem = jnp.where(m1 | m2, 1.0, 0.0)
    rank = run_cnt[0:1, :] + jnp.dot(before_ref[...], mem.astype(BF16), preferred_element_type=F32)
    rank1 = jnp.sum(jnp.where(m1, rank, 0.0), axis=-1, keepdims=True)
    rank2 = jnp.sum(jnp.where(m2, rank, 0.0), axis=-1, keepdims=True)
    lane = lax.broadcasted_iota(jnp.int32, route.shape, 1)
    route_ref[0] = jnp.where(lane == 4, rank1, jnp.where(lane == 5, rank2, route))
    run_cnt[...] = run_cnt[...] + jnp.sum(mem, axis=0, keepdims=True)
    cnt_ref[...] = run_cnt[...]


def _post_attn(a, w_o, layer, x, g1, sc2, sh2, ln_g, ln_b, wr, br, before, alpha):
    b, s, d = x.shape
    sub, width = TOKEN_ROWS, d // TOKEN_ROWS
    tiles_per_batch = s // ROW_TILE
    const = lambda i, j: (0, 0)
    tile = lambda i, j: (i, j, 0)
    per_b = lambda i, j: (i, 0, 0)
    return pl.pallas_call(
        functools.partial(_post_attn_kernel, alpha=alpha),
        grid=(b, s // ROW_TILE),
        in_specs=[
            pl.BlockSpec((1, ROW_TILE, d), tile),
            pl.BlockSpec((1, d, d), lambda i, j: (layer, 0, 0), pipeline_mode=pl.Buffered(1)),
            pl.BlockSpec((1, ROW_TILE, d), tile),
            pl.BlockSpec((1, 1, d), per_b),
            pl.BlockSpec((1, 1, d), per_b),
            pl.BlockSpec((1, 1, d), per_b),
            pl.BlockSpec((1, d), const),
            pl.BlockSpec((1, d), const),
            pl.BlockSpec((d, 2 * LANES), const),
            pl.BlockSpec((1, LANES), const),
            pl.BlockSpec((ROW_TILE, ROW_TILE), const),
        ],
        out_specs=[
            pl.BlockSpec((1, ROW_TILE, d), tile),
            pl.BlockSpec((ROW_TILE * sub, width), lambda i, j: (i * tiles_per_batch + j, 0)),
            pl.BlockSpec((1, ROW_TILE, LANES), tile),
            pl.BlockSpec((8, LANES), const),
        ],
        out_shape=[
            jax.ShapeDtypeStruct((b, s, d), F32),
            jax.ShapeDtypeStruct((b * s * sub, width), F32),
            jax.ShapeDtypeStruct((b, s, LANES), F32),
            jax.ShapeDtypeStruct((8, LANES), F32),
        ],
        scratch_shapes=[pltpu.VMEM((8, LANES), F32), pltpu.VMEM((d, d), BF16)],
        compiler_params=_cparams(("arbitrary", "arbitrary")),
        name="post_attn",
    )(a, w_o, x, g1, sc2, sh2, ln_g, ln_b, wr, br, before)


def _dispatch_kernel(pos_ref, h_ref, xs_hbm, sem):
    tm = DISPATCH_TM
    sub = h_ref.shape[0] // tm

    def issue(jb, carry):
        for u in range(DMA_UNROLL):
            j = jb * DMA_UNROLL + u
            src = h_ref.at[pl.ds(pl.multiple_of(j * sub, sub), sub)]
            for k in range(TOP_K):
                row = pos_ref[0, 0, TOP_K * j + k]
                dst = xs_hbm.at[pl.ds(pl.multiple_of(row * sub, sub), sub)]
                pltpu.make_async_copy(src, dst, sem).start(priority=k % 2)
        return carry

    lax.fori_loop(0, tm // DMA_UNROLL, issue, 0)
    for _ in range(TOP_K):
        pltpu.make_async_copy(h_ref, xs_hbm.at[pl.ds(0, tm * sub)], sem).wait()


def _dispatch(pos, h2_tiles, n_tok):
    sub, width = h2_tiles.shape[0] // n_tok, h2_tiles.shape[1]
    tm = DISPATCH_TM
    n_tiles = n_tok // tm
    return pl.pallas_call(
        _dispatch_kernel,
        grid=(n_tiles,),
        in_specs=[
            pl.BlockSpec((1, 1, TOP_K * tm), lambda i: (i, 0, 0), memory_space=pltpu.SMEM),
            pl.BlockSpec((tm * sub, width), lambda i: (i, 0)),
        ],
        out_specs=pl.BlockSpec(memory_space=pl.ANY),
        out_shape=jax.ShapeDtypeStruct((n_tok * TOP_K * sub, width), F32),
        scratch_shapes=[pltpu.SemaphoreType.DMA],
        compiler_params=_cparams(("arbitrary",)),
        name="moe_dispatch",
    )(pos.reshape(n_tiles, 1, TOP_K * tm), h2_tiles)


def _expert_kernel(blk_ref, exp_ref, lo_ref, hi_ref, x_ref, wg_ref, wu_ref, wd_ref, y_ref,
                   acc, wgb, wub, wdb, cast_expert, state):
    i = pl.program_id(0)
    n_items = pl.num_programs(0)
    tb = MOE_TB
    prev = jnp.maximum(i - 1, 0)
    nxt = jnp.minimum(i + 1, n_items - 1)
    first = (i == 0) | (blk_ref[i] != blk_ref[prev])
    last = (i == n_items - 1) | (blk_ref[i] != blk_ref[nxt])
    lo, hi = lo_ref[i], hi_ref[i]
    whole = (lo == 0) & (hi == tb)

    @pl.when(i == 0)
    def _():
        cast_expert[0] = -1

    @pl.when(first)
    def _():
        state[0] = 0
        state[1] = 0

    @pl.when(hi > lo)
    def _():
        @pl.when(cast_expert[0] != exp_ref[i])
        def _():
            wgb[...] = wg_ref[0, 0].astype(BF16)
            wub[...] = wu_ref[0, 0].astype(BF16)
            wdb[...] = wd_ref[0, 0].astype(BF16)
            cast_expert[0] = exp_ref[i]

        xb = _load_token_tiles(x_ref, tb, BF16)
        g = jnp.dot(xb, wgb[...], preferred_element_type=F32)
        u = jnp.dot(xb, wub[...], preferred_element_type=F32)
        act = (g * jax.nn.sigmoid(g) * u).astype(BF16)
        y = jnp.dot(act, wdb[...], preferred_element_type=F32)

        @pl.when(whole)
        def _():
            _store_token_tiles(y_ref, y)
            state[0] = 1

        @pl.when(jnp.logical_not(whole))
        def _():
            row = lax.broadcasted_iota(jnp.int32, (tb, 1), 0)
            part = jnp.where((row >= lo) & (row < hi), y, 0.0)

            @pl.when(state[1] == 0)
            def _():
                acc[...] = part

            @pl.when(state[1] != 0)
            def _():
                acc[...] += part

            state[1] = 1

    @pl.when(last & (state[0] == 0))
    def _():
        _store_token_tiles(y_ref, acc[...])


def _expert_mlps(items, xs_tiles, w_gate, w_up, w_down, layer):
    item_blk, item_exp, item_lo, item_hi = items
    n_items = item_blk.shape[0]
    _, _, d, de = w_gate.shape
    sub, width = TOKEN_ROWS, d // TOKEN_ROWS
    w_index = lambda i, blk, ex, lo, hi: (layer, ex[i], 0, 0)
    grid_spec = pltpu.PrefetchScalarGridSpec(
        num_scalar_prefetch=4,
        grid=(n_items,),
        in_specs=[
            pl.BlockSpec((MOE_TB * sub, width), lambda i, blk, ex, lo, hi: (blk[i], 0)),
            pl.BlockSpec((1, 1, d, de), w_index),
            pl.BlockSpec((1, 1, d, de), w_index),
            pl.BlockSpec((1, 1, de, d), w_index),
        ],
        out_specs=pl.BlockSpec((MOE_TB * sub, width), lambda i, blk, ex, lo, hi: (blk[i], 0)),
        scratch_shapes=[
            pltpu.VMEM((MOE_TB, d), F32),
            pltpu.VMEM((d, de), BF16),
            pltpu.VMEM((d, de), BF16),
            pltpu.VMEM((de, d), BF16),
            pltpu.SMEM((1,), jnp.int32),
            pltpu.SMEM((2,), jnp.int32),
        ],
    )
    return pl.pallas_call(
        _expert_kernel,
        grid_spec=grid_spec,
        out_shape=jax.ShapeDtypeStruct(xs_tiles.shape, F32),
        compiler_params=_cparams(("arbitrary",)),
        name="expert_mlps",
    )(item_blk, item_exp, item_lo, item_hi, xs_tiles, w_gate, w_up, w_down)


def _combine_kernel(pos_ref, pos_next_ref, ys_hbm, x_ref, route_ref, g2_ref, lng_ref, lnb_ref, xo_ref, buf, sems,
                    *, alpha):
    i = pl.program_id(0)
    n = pl.num_programs(0)
    tm = COMBINE_TM
    sub = buf.shape[2] // tm
    slot = i % 2

    def gather(p_ref, dst_slot):
        def issue(jb, carry):
            for u in range(DMA_UNROLL):
                j = jb * DMA_UNROLL + u
                for k in range(TOP_K):
                    _tile_copy(ys_hbm, buf.at[dst_slot, k], p_ref[0, 0, TOP_K * j + k], j, sub,
                               sems.at[dst_slot]).start(priority=k % 2)
            return carry

        lax.fori_loop(0, tm // DMA_UNROLL, issue, 0)

    @pl.when(i == 0)
    def _():
        gather(pos_ref, slot)

    @pl.when(i + 1 < n)
    def _():
        gather(pos_next_ref, 1 - slot)

    route = route_ref[...]
    lane = lax.broadcasted_iota(jnp.int32, route.shape, 1)
    gates = [jnp.sum(jnp.where(lane == k, route, 0.0), axis=-1, keepdims=True) for k in range(TOP_K)]
    for k in range(TOP_K):
        pltpu.make_async_copy(ys_hbm.at[pl.ds(0, tm * sub)], buf.at[slot, k], sems.at[slot]).wait()
    y = _load_token_tiles(buf.at[slot, 0], tm, F32) * gates[0]
    for k in range(1, TOP_K):
        y = y + _load_token_tiles(buf.at[slot, k], tm, F32) * gates[k]
    z = alpha * x_ref[...] + (1.0 + g2_ref[0]) * y
    xo_ref[...] = _layer_norm_rows(z, lng_ref[...], lnb_ref[...])


def _combine(pos3, ys, x2d, route2d, g2, ln_g, ln_b, seq, alpha):
    n_tok, d = x2d.shape
    tm = COMBINE_TM
    n_tiles = n_tok // tm
    tiles_per_batch = seq // tm
    const = lambda i: (0, 0)
    return pl.pallas_call(
        functools.partial(_combine_kernel, alpha=alpha),
        grid=(n_tiles,),
        in_specs=[
            pl.BlockSpec((1, 1, TOP_K * tm), lambda i: (i, 0, 0), memory_space=pltpu.SMEM),
            pl.BlockSpec((1, 1, TOP_K * tm), lambda i: (jnp.minimum(i + 1, n_tiles - 1), 0, 0),
                         memory_space=pltpu.SMEM),
            pl.BlockSpec(memory_space=pl.ANY),
            pl.BlockSpec((tm, d), lambda i: (i, 0)),
            pl.BlockSpec((tm, LANES), lambda i: (i, 0)),
            pl.BlockSpec((1, 1, d), lambda i: (i // tiles_per_batch, 0, 0)),
            pl.BlockSpec((1, d), const),
            pl.BlockSpec((1, d), const),
        ],
        out_specs=pl.BlockSpec((tm, d), lambda i: (i, 0)),
        out_shape=jax.ShapeDtypeStruct((n_tok, d), F32),
        scratch_shapes=[
            pltpu.VMEM((2, TOP_K, tm * TOKEN_ROWS, d // TOKEN_ROWS), F32),
            pltpu.SemaphoreType.DMA((2,)),
        ],
        compiler_params=_cparams(("arbitrary",)),
        name="moe_combine",
    )(pos3, pos3, ys, x2d, route2d, g2, ln_g, ln_b)


def _combine_project_kernel(pos_ref, pos_next_ref, ys_hbm, x_ref, route_ref, g2_ref, lng_ref, lnb_ref,
                            sc_ref, sh_ref, w_ref, *rest, alpha, d_model, gqa):
    xo_ref, qkv_ref, buf, sems = rest[-4:]
    i = pl.program_id(0)
    n = pl.num_programs(0)
    tm = ROW_TILE
    sub = buf.shape[2] // tm
    slot = i % 2

    def tile_gather(p_ref, dst_slot, j, k):
        return _tile_copy(ys_hbm, buf.at[dst_slot, k], p_ref[0, 0, TOP_K * j + k], j, sub, sems.at[dst_slot])

    @pl.when(i == 0)
    def _():
        def issue(jb, carry):
            for u in range(DMA_UNROLL):
                for k in range(TOP_K):
                    tile_gather(pos_ref, slot, jb * DMA_UNROLL + u, k).start(priority=k % 2)
            return carry

        lax.fori_loop(0, tm // DMA_UNROLL, issue, 0)

    route = route_ref[...]
    lane = lax.broadcasted_iota(jnp.int32, route.shape, 1)
    gates = [jnp.sum(jnp.where(lane == k, route, 0.0), axis=-1, keepdims=True) for k in range(TOP_K)]
    for k in range(TOP_K):
        pltpu.make_async_copy(ys_hbm.at[pl.ds(0, tm * sub)], buf.at[slot, k], sems.at[slot]).wait()
    y = _load_token_tiles(buf.at[slot, 0], tm, F32) * gates[0]
    for k in range(1, TOP_K):
        y = y + _load_token_tiles(buf.at[slot, k], tm, F32) * gates[k]
    z = alpha * x_ref[...] + (1.0 + g2_ref[0]) * y
    xn = _layer_norm_rows(z, lng_ref[...], lnb_ref[...])
    xo_ref[...] = xn
    h = (xn * (1.0 + sc_ref[0]) + sh_ref[0]).astype(BF16)

    for j in range(tm):
        for k in range(TOP_K):
            tile_gather(pos_next_ref, 1 - slot, j, k).start(priority=k % 2)

    if gqa:
        _project_gqa(h, w_ref, *rest[:6], qkv_ref, d_model)
    else:
        _project_na(h, w_ref, qkv_ref, d_model)

    @pl.when(i == n - 1)
    def _():
        for k in range(TOP_K):
            pltpu.make_async_copy(ys_hbm.at[pl.ds(0, tm * sub)], buf.at[1 - slot, k], sems.at[1 - slot]).wait()


def _combine_project(pos3, ys, x2d, route2d, g2, ln_g, ln_b, sc, sh, w_bf16, gqa_extras, seq, alpha):
    n_tok, d = x2d.shape
    n = w_bf16.shape[1]
    n_out = _gqa_out_cols(d) if gqa_extras else n
    tm = ROW_TILE
    n_tiles = n_tok // tm
    tiles_per_batch = seq // tm
    const = lambda i: (0, 0)
    per_b = lambda i: (i // tiles_per_batch, 0, 0)
    extra_specs = []
    if gqa_extras:
        extra_specs = [
            pl.BlockSpec((1, QKV_CHUNK), const),
            pl.BlockSpec((1, QKV_CHUNK), const),
            pl.BlockSpec((tm, QKV_CHUNK), lambda i: (i % tiles_per_batch, 0)),
            pl.BlockSpec((tm, QKV_CHUNK), lambda i: (i % tiles_per_batch, 0)),
            pl.BlockSpec((QKV_CHUNK, LANES), const),
            pl.BlockSpec((LANES, QKV_CHUNK), const),
        ]
    return pl.pallas_call(
        functools.partial(_combine_project_kernel, alpha=alpha, d_model=d, gqa=bool(gqa_extras)),
        grid=(n_tiles,),
        in_specs=[
            pl.BlockSpec((1, 1, TOP_K * tm), lambda i: (i, 0, 0), memory_space=pltpu.SMEM),
            pl.BlockSpec((1, 1, TOP_K * tm), lambda i: (jnp.minimum(i + 1, n_tiles - 1), 0, 0),
                         memory_space=pltpu.SMEM),
            pl.BlockSpec(memory_space=pl.ANY),
            pl.BlockSpec((tm, d), lambda i: (i, 0)),
            pl.BlockSpec((tm, LANES), lambda i: (i, 0)),
            pl.BlockSpec((1, 1, d), per_b),
            pl.BlockSpec((1, d), const),
            pl.BlockSpec((1, d), const),
            pl.BlockSpec((1, 1, d), per_b),
            pl.BlockSpec((1, 1, d), per_b),
            pl.BlockSpec((d, n), const),
        ] + extra_specs,
        out_specs=[
            pl.BlockSpec((tm, d), lambda i: (i, 0)),
            pl.BlockSpec((tm, n_out), lambda i: (i, 0)),
        ],
        out_shape=[
            jax.ShapeDtypeStruct((n_tok, d), F32),
            jax.ShapeDtypeStruct((n_tok, n_out), BF16),
        ],
        scratch_shapes=[
            pltpu.VMEM((2, TOP_K, tm * TOKEN_ROWS, d // TOKEN_ROWS), F32),
            pltpu.SemaphoreType.DMA((2,)),
        ],
        compiler_params=_cparams(("arbitrary",)),
        name="combine_qkv_gqa" if gqa_extras else "combine_qkv_na",
    )(pos3, pos3, ys, x2d, route2d, g2, ln_g, ln_b, sc, sh, w_bf16, *gqa_extras)


def _routing_tables(route2d, cnt):
    n_tok = route2d.shape[0]
    n_rows = n_tok * TOP_K
    counts = cnt[0, N_GROUPS:N_GROUPS + N_EXPERTS].astype(jnp.int32)
    seg_end = jnp.cumsum(counts)
    experts = route2d[:, 2:2 + TOP_K].astype(jnp.int32)
    ranks = route2d[:, 4:4 + TOP_K].astype(jnp.int32)
    is_e = experts[:, :, None] == jnp.arange(N_EXPERTS, dtype=jnp.int32)[None, None, :]
    pos = (ranks + jnp.sum(jnp.where(is_e, (seg_end - counts)[None, None, :], 0), axis=-1)).reshape(n_rows)
    cuts = jnp.concatenate([jnp.arange(n_rows // MOE_TB, dtype=jnp.int32) * MOE_TB, seg_end - counts])
    n_items = cuts.shape[0]
    idx = jnp.arange(n_items, dtype=jnp.int32)
    rank = (jnp.sum(cuts[None, :] < cuts[:, None], axis=1)
            + jnp.sum((cuts[None, :] == cuts[:, None]) & (idx[None, :] < idx[:, None]), axis=1))
    lo_abs = jnp.sum(jnp.where(rank[None, :] == idx[:, None], cuts[None, :], 0), axis=1)
    hi_abs = jnp.concatenate([lo_abs[1:], jnp.full((1,), n_rows, jnp.int32)])
    item_blk = jnp.minimum(lo_abs // MOE_TB, n_rows // MOE_TB - 1)
    item_exp = jnp.minimum(jnp.sum(seg_end[None, :] <= lo_abs[:, None], axis=1), N_EXPERTS - 1)
    item_lo = lo_abs - item_blk * MOE_TB
    item_hi = hi_abs - item_blk * MOE_TB
    items = tuple(a.astype(jnp.int32) for a in (item_blk, item_exp, item_lo, item_hi))
    return pos, items


def _rope_tables(seq, width):
    t = jnp.arange(seq, dtype=jnp.int32)
    pos = jnp.stack([t // GRID_W, t % GRID_W], axis=-1).astype(F32)
    inv_freq = ROPE_THETA ** (-jnp.arange(ROPE_PAIRS, dtype=F32) / ROPE_PAIRS)
    ang = pos[:, :, None] * inv_freq
    cos, sin = jnp.cos(ang), jnp.sin(ang)
    cosf = jnp.stack([cos, cos], axis=2).reshape(seq, HEAD_DIM)
    sinf = jnp.stack([-sin, sin], axis=2).reshape(seq, HEAD_DIM)
    reps = width // HEAD_DIM
    return jnp.tile(cosf, (1, reps)), jnp.tile(sinf, (1, reps))


def kernel(x, c, ada_w, ada_b, ln_g, ln_b, na_w_qkv, na_rpb, na_w_o, gqa_w_qkv, gqa_q_norm, gqa_k_norm, gqa_w_o,
           moe_w_group, moe_b_group, moe_w_expert, moe_b_expert, moe_w_gate, moe_w_up, moe_w_down):
    b, s, d = x.shape
    depth = ada_w.shape[0]
    n_tok = b * s
    rows = s // GRID_W
    alpha = (2 * depth) ** 0.25

    mod = _ada_modulation(c, ada_w, ada_b).reshape(depth, b, 6, 1, d)

    cosf, sinf = _rope_tables(s, QKV_CHUNK)
    head_of_lane = jnp.arange(QKV_CHUNK, dtype=jnp.int32) // HEAD_DIM
    seg_t = (head_of_lane[None, :] == jnp.arange(LANES, dtype=jnp.int32)[:, None])
    seg = (seg_t.T.astype(F32) / HEAD_DIM).astype(BF16)
    seg_t = seg_t.astype(BF16)
    tok = jnp.arange(ROW_TILE, dtype=jnp.int32)
    before = (tok[None, :] < tok[:, None]).astype(BF16)

    def qkv_params(i):
        j = i // 2
        if i % 2 == 0:
            return na_w_qkv[j].astype(BF16), ()
        reps = QKV_CHUNK // HEAD_DIM
        return gqa_w_qkv[j].astype(BF16), (jnp.tile(gqa_q_norm[j], reps)[None, :],
                                           jnp.tile(gqa_k_norm[j], reps)[None, :], cosf, sinf, seg, seg_t)

    qkv = None
    for i in range(depth):
        sh1, sc1, g1, sh2, sc2, g2 = (mod[i, :, k] for k in range(6))
        j = i // 2
        if qkv is None:
            w_bf16, extras = qkv_params(i)
            assert not extras, "the first layer is a neighbourhood-attention layer"
            qkv = _qkv_na(x, sc1, sh1, w_bf16)
        if i % 2 == 0:
            attn = _na_attention(qkv, _na_bias_table(na_rpb[j]), rows)
            w_o = na_w_o
        else:
            attn = _gqa_attention(qkv, d, gqa_q_norm[j], gqa_k_norm[j])
            w_o = gqa_w_o

        w_r = jnp.concatenate([moe_w_group[i], moe_w_expert[i]], axis=1)
        w_r = jnp.pad(w_r, ((0, 0), (0, LANES - w_r.shape[1])))
        wr_hi, wr_lo = _split_bf16(w_r)
        b_r = jnp.concatenate([moe_b_group[i], moe_b_expert[i]])
        b_r = jnp.pad(b_r, (0, LANES - b_r.shape[0]))[None, :]
        x, h2, route, cnt = _post_attn(attn, w_o, j, x, g1, sc2, sh2, ln_g[i, 0][None, :],
                                       ln_b[i, 0][None, :], jnp.concatenate([wr_hi, wr_lo], axis=1), b_r, before,
                                       alpha)

        route2d = route.reshape(n_tok, LANES)
        pos, items = _routing_tables(route2d, cnt)
        pos3 = pos.reshape(n_tok // POS_TILE, 1, TOP_K * POS_TILE)
        xs = _dispatch(pos, h2, n_tok)
        ys = _expert_mlps(items, xs, moe_w_gate, moe_w_up, moe_w_down, i)
        lng, lnb = ln_g[i, 1][None, :], ln_b[i, 1][None, :]
        if i + 1 < depth:
            w_bf16, extras = qkv_params(i + 1)
            x2d, qkv2d = _combine_project(pos3, ys, x.reshape(n_tok, d), route2d, g2, lng, lnb,
                                          mod[i + 1, :, 1], mod[i + 1, :, 0], w_bf16, extras, s, alpha)
            x, qkv = x2d.reshape(b, s, d), qkv2d.reshape(b, s, -1)
        else:
            x = _combine(pos3, ys, x.reshape(n_tok, d), route2d, g2, lng, lnb, s, alpha).reshape(b, s, d)
    return x
```
